```python
import math
import jax, jax.numpy as jnp
from jax import lax
import numpy as np

D_MODEL = 1024
BATCH = 8
SEQ = 2048
DEPTH = 2
DEC_BATCH = 128
DEC_SEQ = 4
PAST_LEN = 16384
PAGE_SIZE = 128

MIX_WIDTH = D_MODEL
HGRN_WIDTH = MIX_WIDTH // 2
HGRN_HEADS = 4
HGRN_HEAD_DIM = HGRN_WIDTH // HGRN_HEADS
HGRN_KEY_DIM = 128
HGRN_KEY_WIDTH = HGRN_HEADS * HGRN_KEY_DIM
CONV_DIM = MIX_WIDTH - HGRN_WIDTH
CONV_GROUPS = 8
CONV_K = 3
IN_SIZES = [HGRN_KEY_WIDTH, HGRN_KEY_WIDTH, HGRN_WIDTH, HGRN_WIDTH,
            CONV_DIM, CONV_DIM, CONV_DIM]
N_IN = sum(IN_SIZES)
N_EXPERTS = 16
N_EXPERT_GROUPS = 4
EXPERTS_PER_GROUP = N_EXPERTS // N_EXPERT_GROUPS
TOP_K = 2
EXPERT_FF = 512
CHUNK = 64
ALPHA = (2 * DEPTH) ** 0.25
BETA = (8 * DEPTH) ** -0.25
LN_EPS = 1e-5
RMS_EPS = 1e-6

kernel_name = "hgrn2_shortconv_parallel_moe_decoder_step"


def layer_norm(x, g, b):
    xf = x.astype(jnp.float32)
    mu = jnp.mean(xf, axis=-1, keepdims=True)
    xc = xf - mu
    var = jnp.mean(xc * xc, axis=-1, keepdims=True)
    y = xc * lax.rsqrt(var + LN_EPS) * g.astype(jnp.float32) + b.astype(jnp.float32)
    return y.astype(x.dtype)


def group_rms(x, g, n_groups):
    shp = x.shape
    xf = x.astype(jnp.float32).reshape(shp[:-1] + (n_groups, shp[-1] // n_groups))
    xf = xf * lax.rsqrt(jnp.mean(xf * xf, axis=-1, keepdims=True) + RMS_EPS)
    return (xf.reshape(shp) * g.astype(jnp.float32)).astype(x.dtype)


def forget_lower_bounds(lower_bounds):
    lb = jnp.cumsum(jax.nn.softmax(lower_bounds.astype(jnp.float32), axis=0), axis=0)
    return lb - lb[0:1]


def hgrn_recurrence(q, k, v, logf, S0):
    B, T, H, dk = q.shape
    dv = v.shape[-1]
    C = math.gcd(T, CHUNK)
    n = T // C

    def to_chunks(a):
        return a.reshape(B, n, C, H, a.shape[-1]).transpose(1, 0, 3, 2, 4)

    causal = jnp.tril(jnp.ones((C, C), dtype=bool))[:, :, None]

    def step(S, inp):
        qc, kc, vc, gc = inp
        b = jnp.cumsum(gc, axis=2)
        o_inter = jnp.einsum('bhtk,bhkv->bhtv', qc * jnp.exp(b), S)
        diff = b[:, :, :, None, :] - b[:, :, None, :, :]
        decay = jnp.exp(jnp.where(causal, diff, -jnp.inf))
        att = jnp.einsum('bhtk,bhtsk,bhsk->bhts', qc, decay, kc)
        o = o_inter + jnp.einsum('bhts,bhsv->bhtv', att, vc)
        b_last = b[:, :, -1:, :]
        S_new = jnp.exp(b_last[:, :, 0, :])[..., None] * S + jnp.einsum(
            'bhsk,bhsv->bhkv', kc * jnp.exp(b_last - b), vc)
        return S_new, o

    S_fin, o = lax.scan(step, S0, (to_chunks(q), to_chunks(k), to_chunks(v), to_chunks(logf)))
    o = o.transpose(1, 0, 3, 2, 4).reshape(B, T, H, dv)
    return o, S_fin


def token_mixer(x, w_in, w_out, lb, hgrn_norm_g, conv_w, conv_norm_g, S0, conv_buf):
    B, T, _ = x.shape
    proj = x @ w_in
    q, fz, i, g, gb, gc, cx = jnp.split(proj, list(np.cumsum(IN_SIZES[:-1])), axis=-1)

    f32 = jnp.float32
    qh = jax.nn.silu(q.astype(f32)).reshape(B, T, HGRN_HEADS, HGRN_KEY_DIM)
    logf = jnp.logaddexp(jnp.log(lb), jnp.log1p(-lb) + jax.nn.log_sigmoid(fz.astype(f32)))
    logf = logf.reshape(B, T, HGRN_HEADS, HGRN_KEY_DIM)
    kh = -jnp.expm1(logf)
    vh = i.astype(f32).reshape(B, T, HGRN_HEADS, HGRN_HEAD_DIM)
    o, S_new = hgrn_recurrence(qh, kh, vh, logf, S0.astype(f32))
    o = group_rms(o.reshape(B, T, HGRN_WIDTH).astype(x.dtype), hgrn_norm_g, HGRN_HEADS)
    o = o * jax.nn.silu(g)

    u = gc * cx
    up = jnp.concatenate([conv_buf.astype(u.dtype), u], axis=1)
    y = sum(conv_w[j] * up[:, j:j + T] for j in range(CONV_K))
    yc = group_rms(gb * y, conv_norm_g, CONV_GROUPS)
    new_buf = up[:, T:]

    out = jnp.concatenate([o, yc.astype(o.dtype)], axis=-1) @ w_out
    return out, S_new, new_buf


def grouped_moe(x, w_router, router_bias, w_gate, w_up, w_down):
    B, T, D = x.shape
    xt = x.reshape(B * T, D)
    logits = (xt @ w_router).astype(jnp.float32) + router_bias.astype(jnp.float32)
    probs = jax.nn.softmax(logits, axis=-1)
    grp = probs.reshape(-1, N_EXPERT_GROUPS, EXPERTS_PER_GROUP)
    grp_score = lax.top_k(grp, TOP_K)[0].sum(-1)
    g_idx = lax.top_k(grp_score, 1)[1][:, 0]
    emask = jnp.repeat(jax.nn.one_hot(g_idx, N_EXPERT_GROUPS, dtype=jnp.bool_),
                       EXPERTS_PER_GROUP, axis=1)
    top_w, top_i = lax.top_k(jnp.where(emask, probs, -1.0), TOP_K)
    top_w = top_w / jnp.sum(top_w, axis=-1, keepdims=True)
    gates = jnp.sum(jax.nn.one_hot(top_i, N_EXPERTS, dtype=jnp.float32) * top_w[..., None], axis=1)
    gates = gates.astype(x.dtype)
    y = jnp.zeros_like(xt)
    for e in range(N_EXPERTS):
        h = jax.nn.silu(xt @ w_gate[e]) * (xt @ w_up[e])
        y = y + gates[:, e:e + 1] * (h @ w_down[e])
    return y.reshape(B, T, D)


def trunk(x, S_all, buf_all, w_in, w_out, lbs, hgrn_norm_g, conv_w, conv_norm_g,
          ln1_g, ln1_b, ln2_g, ln2_b, w_router, router_bias, w_gate, w_up, w_down):
    new_S, new_buf = [], []
    for l in range(DEPTH):
        m, S_l, b_l = token_mixer(x, w_in[l], w_out[l], lbs[l], hgrn_norm_g[l], conv_w[l],
                                  conv_norm_g[l], S_all[l], buf_all[l])
        x = layer_norm(ALPHA * x + m, ln1_g[l], ln1_b[l])
        x = layer_norm(ALPHA * x + grouped_moe(x, w_router, router_bias, w_gate[l], w_up[l], w_down[l]),
                       ln2_g[l], ln2_b[l])
        new_S.append(S_l)
        new_buf.append(b_l)
    return x, jnp.stack(new_S), jnp.stack(new_buf)


def setup_inputs(seed: int = 0) -> dict:
    key = jax.random.key(seed)
    ks = jax.random.split(key, 20)
    nrm = jax.random.normal
    f32 = jnp.float32
    return {
        "x_prompt": nrm(ks[0], (BATCH, SEQ, D_MODEL), f32),
        "x_sample": nrm(ks[1], (DEC_BATCH, DEC_SEQ, D_MODEL), f32),
        "state_hgrn": 0.5 * nrm(ks[2], (DEPTH, DEC_BATCH, HGRN_HEADS, HGRN_KEY_DIM, HGRN_HEAD_DIM), f32),
        "state_conv": nrm(ks[3], (DEPTH, DEC_BATCH, CONV_K - 1, CONV_DIM), f32),
        "w_in": nrm(ks[4], (DEPTH, D_MODEL, N_IN), f32) * D_MODEL ** -0.5,
        "w_out": nrm(ks[5], (DEPTH, MIX_WIDTH, D_MODEL), f32) * (MIX_WIDTH ** -0.5 * BETA),
        "lower_bounds": nrm(ks[6], (DEPTH, HGRN_KEY_WIDTH), f32),
        "hgrn_norm_g": 1.0 + 0.02 * nrm(ks[7], (DEPTH, HGRN_WIDTH), f32),
        "conv_w": nrm(ks[8], (DEPTH, CONV_K, CONV_DIM), f32) * CONV_K ** -0.5,
        "conv_norm_g": 1.0 + 0.02 * nrm(ks[9], (DEPTH, CONV_DIM), f32),
        "ln1_g": 1.0 + 0.02 * nrm(ks[10], (DEPTH, D_MODEL), f32),
        "ln1_b": 0.02 * nrm(ks[11], (DEPTH, D_MODEL), f32),
        "ln2_g": 1.0 + 0.02 * nrm(ks[12], (DEPTH, D_MODEL), f32),
        "ln2_b": 0.02 * nrm(ks[13], (DEPTH, D_MODEL), f32),
        "w_router": nrm(ks[14], (D_MODEL, N_EXPERTS), f32) * D_MODEL ** -0.5,
        "router_bias": 0.01 * nrm(ks[15], (N_EXPERTS,), f32),
        "w_gate": nrm(ks[16], (DEPTH, N_EXPERTS, D_MODEL, EXPERT_FF), f32) * D_MODEL ** -0.5,
        "w_up": nrm(ks[17], (DEPTH, N_EXPERTS, D_MODEL, EXPERT_FF), f32) * D_MODEL ** -0.5,
        "w_down": nrm(ks[18], (DEPTH, N_EXPERTS, EXPERT_FF, D_MODEL), f32) * (EXPERT_FF ** -0.5 * BETA),
    }


def reference(x_prompt, x_sample, state_hgrn, state_conv, w_in, w_out, lower_bounds, hgrn_norm_g,
              conv_w, conv_norm_g, ln1_g, ln1_b, ln2_g, ln2_b, w_router, router_bias,
              w_gate, w_up, w_down):
    lbs = forget_lower_bounds(lower_bounds)
    bp = x_prompt.shape[0]
    S0_prompt = jnp.zeros((DEPTH, bp, HGRN_HEADS, HGRN_KEY_DIM, HGRN_HEAD_DIM), jnp.float32)
    buf0_prompt = jnp.zeros((DEPTH, bp, CONV_K - 1, CONV_DIM), x_prompt.dtype)
    y_prompt, S_prompt, buf_prompt = trunk(
        x_prompt, S0_prompt, buf0_prompt, w_in, w_out, lbs, hgrn_norm_g, conv_w, conv_norm_g,
        ln1_g, ln1_b, ln2_g, ln2_b, w_router, router_bias, w_gate, w_up, w_down)
    y_sample, S_sample, buf_sample = trunk(
        x_sample, state_hgrn, state_conv, w_in, w_out, lbs, hgrn_norm_g, conv_w, conv_norm_g,
        ln1_g, ln1_b, ln2_g, ln2_b, w_router, router_bias, w_gate, w_up, w_down)
    return (y_prompt, y_sample, S_prompt, buf_prompt, S_sample, buf_sample)
```

```python
import functools

import jax
import jax.numpy as jnp
from jax import lax
from jax.experimental import pallas as pl
from jax.experimental.pallas import tpu as pltpu

F32 = jnp.float32
BF16 = jnp.bfloat16

D_MODEL = 1024
DEPTH = 2
HEADS = 4
HEAD_DIM = 128
HGRN_W = HEADS * HEAD_DIM
CONV_DIM = 512
CONV_GROUPS = 8
CONV_K = 3
N_IN = 7 * 512
N_EXPERTS = 16
N_GROUPS = 4
PER_GROUP = 4
EXPERT_FF = 512
ALPHA = (2 * DEPTH) ** 0.25
LN_EPS = 1e-5
RMS_EPS = 1e-6

LANES = 128
SUBLANES = 8
CHUNK = 64
PROMPT_TILE = 256
SAMPLE_ROWS = 8
SAMPLE_SEQS = 16
MOE_TILE = 1024
VMEM_LIMIT = 56 * 1024 * 1024

OFF_Q, OFF_F, OFF_I, OFF_G, OFF_GB, OFF_GC, OFF_CX = (i * 512 for i in range(7))

P_LOGLB, P_LOG1MLB, P_OMLB, P_HNORM, P_CNORM, P_CW0, P_CW1, P_CW2 = range(8)

NT_DIMS = (((1,), (1,)), ((), ()))
TN_DIMS = (((0,), (0,)), ((), ()))


def _sigmoid(x):
    return 1.0 / (1.0 + jnp.exp(-x))


def _seg_cumsum(x, seg):
    row = lax.broadcasted_iota(jnp.int32, x.shape, 0)
    pos = row & (seg - 1)
    sh = 1
    while sh < seg:
        x = x + jnp.where(pos >= sh, pltpu.roll(x, sh, axis=0), 0.0)
        sh *= 2
    return x


def _gate_terms(z, p_ref):
    e = jnp.exp(-jnp.abs(z))
    inv = 1.0 / (1.0 + e)
    logsig = jnp.minimum(z, 0.0) - jnp.log(1.0 + e)
    a = p_ref[P_LOGLB:P_LOGLB + 1, :]
    b = p_ref[P_LOG1MLB:P_LOG1MLB + 1, :] + logsig
    logf = jnp.maximum(a, b) + jnp.log(1.0 + jnp.exp(-jnp.abs(a - b)))
    k = p_ref[P_OMLB:P_OMLB + 1, :] * (jnp.where(z >= 0.0, e, 1.0) * inv)
    return logf, k


def _group_rms(x, n_groups):
    width = x.shape[-1] // n_groups
    x2 = x * x
    outs = []
    for s in range(x.shape[-1] // LANES):
        xs = x[:, s * LANES:(s + 1) * LANES]
        x2s = x2[:, s * LANES:(s + 1) * LANES]
        if width == LANES:
            ms = jnp.sum(x2s, axis=-1, keepdims=True) * (1.0 / width)
            scale = lax.rsqrt(ms + RMS_EPS)
        else:
            lane = lax.broadcasted_iota(jnp.int32, xs.shape, 1)
            lo = lane < width
            ms_lo = jnp.sum(jnp.where(lo, x2s, 0.0), axis=-1, keepdims=True) * (1.0 / width)
            ms_hi = jnp.sum(jnp.where(lo, 0.0, x2s), axis=-1, keepdims=True) * (1.0 / width)
            scale = jnp.where(lo, lax.rsqrt(ms_lo + RMS_EPS), lax.rsqrt(ms_hi + RMS_EPS))
        outs.append(xs * scale)
    return jnp.concatenate(outs, axis=-1)


def _layer_norm(r, g, b):
    mu = jnp.mean(r, axis=-1, keepdims=True)
    rc = r - mu
    var = jnp.mean(rc * rc, axis=-1, keepdims=True)
    return rc * lax.rsqrt(var + LN_EPS) * g + b


def _router_gates_t(x1, wr_ref, rb_ref):
    rows = x1.shape[0]
    x_hi = x1.astype(BF16)
    x_lo = (x1 - x_hi.astype(F32)).astype(BF16)
    wr = wr_ref[...]
    r1 = lax.dot_general(wr, x_hi, NT_DIMS, preferred_element_type=F32)
    r2 = lax.dot_general(wr, x_lo, NT_DIMS, preferred_element_type=F32)
    lt = r1[0:N_EXPERTS] + r1[N_EXPERTS:] + r2[0:N_EXPERTS] + r2[N_EXPERTS:] + rb_ref[...]
    lg = [lt[e:e + 1, :] for e in range(N_EXPERTS)]
    mx = lg[0]
    for e in range(1, N_EXPERTS):
        mx = jnp.maximum(mx, lg[e])
    ex = [jnp.exp(l - mx) for l in lg]
    best = None
    gi = None
    for g in range(N_GROUPS):
        a, b, c, d = ex[PER_GROUP * g:PER_GROUP * (g + 1)]
        s = jnp.maximum(jnp.maximum(jnp.maximum(a + b, a + c), jnp.maximum(a + d, b + c)),
                        jnp.maximum(b + d, c + d))
        if g == 0:
            best, gi = s, jnp.zeros(s.shape, jnp.int32)
        else:
            upd = s > best
            best = jnp.where(upd, s, best)
            gi = jnp.where(upd, g, gi)
    v = []
    for i in range(PER_GROUP):
        vi = ex[3 * PER_GROUP + i]
        for g in (2, 1, 0):
            vi = jnp.where(gi == g, ex[PER_GROUP * g + i], vi)
        v.append(vi)
    w1, i1 = v[0], jnp.zeros(v[0].shape, jnp.int32)
    for i in range(1, PER_GROUP):
        upd = v[i] > w1
        w1 = jnp.where(upd, v[i], w1)
        i1 = jnp.where(upd, i, i1)
    w2, i2 = None, None
    for i in range(PER_GROUP):
        vi = jnp.where(i1 == i, -1.0, v[i])
        if i == 0:
            w2, i2 = vi, jnp.zeros(vi.shape, jnp.int32)
        else:
            upd = vi > w2
            w2 = jnp.where(upd, vi, w2)
            i2 = jnp.where(upd, i, i2)
    inv = 1.0 / (w1 + w2)
    e1 = gi * PER_GROUP + i1
    e2 = gi * PER_GROUP + i2
    e_iota = lax.broadcasted_iota(jnp.int32, (LANES, rows), 0)
    return jnp.where(e_iota == e1, w1 * inv, 0.0) + jnp.where(e_iota == e2, w2 * inv, 0.0)


def _post_mix(xt, o, g, yc_in, p_ref, wout_ref, ln_ref, wr_ref, rb_ref, x1_ref, gates_ref):
    o = _group_rms(o, HEADS) * p_ref[P_HNORM:P_HNORM + 1, :]
    o = o * (g * _sigmoid(g))
    yc = _group_rms(yc_in, CONV_GROUPS) * p_ref[P_CNORM:P_CNORM + 1, :]
    mix = jnp.concatenate([o, yc], axis=-1).astype(BF16)
    h = jnp.dot(mix, wout_ref[...], preferred_element_type=F32)
    x1 = _layer_norm(ALPHA * xt + h, ln_ref[0:1, :], ln_ref[1:2, :])
    x1_ref[...] = x1
    gates_ref[...] = jnp.transpose(_router_gates_t(x1, wr_ref, rb_ref))


def _prompt_mixer_kernel(x_ref, win_ref, wout_ref, p_ref, ln_ref, wr_ref, rb_ref,
                         x1_ref, gates_ref, s_ref, buf_ref,
                         proj_scr, qs_scr, b_scr, k_scr, o_scr, st_scr, ubuf_scr):
    j = pl.program_id(1)

    @pl.when(j == 0)
    def _():
        st_scr[...] = jnp.zeros_like(st_scr)
        ubuf_scr[...] = jnp.zeros_like(ubuf_scr)

    xt = x_ref[...]
    proj_scr[...] = jnp.dot(xt.astype(BF16), win_ref[...], preferred_element_type=F32)

    q = proj_scr[:, OFF_Q:OFF_Q + 512]
    qs_scr[...] = q * _sigmoid(q)
    logf, k = _gate_terms(proj_scr[:, OFF_F:OFF_F + 512], p_ref)
    k_scr[...] = k
    b_scr[...] = _seg_cumsum(logf, CHUNK)

    tri = (lax.broadcasted_iota(jnp.int32, (CHUNK, CHUNK), 0)
           >= lax.broadcasted_iota(jnp.int32, (CHUNK, CHUNK), 1))

    def chunk_body(c, carry):
        r0 = pl.multiple_of(c * CHUNK, CHUNK)
        rows = pl.ds(r0, CHUNK)
        qs_c = qs_scr[rows, :]
        b_c = b_scr[rows, :]
        k_c = k_scr[rows, :]
        v_c = proj_scr[rows, OFF_I:OFF_I + 512].astype(BF16)
        b_mid = b_c[CHUNK // 2 - 1:CHUNK // 2, :]
        b_last = b_c[CHUNK - 1:CHUNK, :]
        qd = (qs_c * jnp.exp(b_c - b_mid)).astype(BF16)
        kd = (k_c * jnp.exp(b_mid - b_c)).astype(BF16)
        qb = (qs_c * jnp.exp(b_c)).astype(BF16)
        kend = (k_c * jnp.exp(b_last - b_c)).astype(BF16)
        dec = jnp.exp(b_last)
        for h in range(HEADS):
            sl = slice(h * HEAD_DIM, (h + 1) * HEAD_DIM)
            att = lax.dot_general(qd[:, sl], kd[:, sl], NT_DIMS, preferred_element_type=F32)
            att = jnp.where(tri, att, 0.0).astype(BF16)
            st = st_scr[h]
            o_h = jnp.dot(att, v_c[:, sl], preferred_element_type=F32)
            o_h = o_h + lax.dot_general(qb[:, sl], st.astype(BF16), NT_DIMS,
                                        preferred_element_type=F32)
            o_scr[rows, sl] = o_h
            d_st = lax.dot_general(v_c[:, sl], kend[:, sl], TN_DIMS, preferred_element_type=F32)
            st_scr[h] = st * dec[:, sl] + d_st
        return carry

    lax.fori_loop(0, PROMPT_TILE // CHUNK, chunk_body, 0)

    u = proj_scr[:, OFF_GC:OFF_GC + 512] * proj_scr[:, OFF_CX:OFF_CX + 512]
    row = lax.broadcasted_iota(jnp.int32, u.shape, 0)
    prev2 = ubuf_scr[SUBLANES - 2:SUBLANES - 1, :]
    prev1 = ubuf_scr[SUBLANES - 1:SUBLANES, :]
    u1 = jnp.where(row == 0, prev1, pltpu.roll(u, 1, axis=0))
    u2 = jnp.where(row == 0, prev2, jnp.where(row == 1, prev1, pltpu.roll(u, 2, axis=0)))
    y = (p_ref[P_CW0:P_CW0 + 1, :] * u2 + p_ref[P_CW1:P_CW1 + 1, :] * u1
         + p_ref[P_CW2:P_CW2 + 1, :] * u)
    ubuf_scr[...] = u[PROMPT_TILE - SUBLANES:, :]
    yc_in = proj_scr[:, OFF_GB:OFF_GB + 512] * y

    @pl.when(j == pl.num_programs(1) - 1)
    def _():
        for h in range(HEADS):
            s_ref[0, h] = jnp.transpose(st_scr[h])
        buf_ref[0] = u[PROMPT_TILE - (CONV_K - 1):, :]

    _post_mix(xt, o_scr[...], proj_scr[:, OFF_G:OFF_G + 512], yc_in, p_ref, wout_ref, ln_ref,
              wr_ref, rb_ref, x1_ref, gates_ref)


def _sample_mixer_kernel(n_valid, x_ref, s0_ref, cbuf_ref, win_ref, wout_ref, p_ref, ln_ref,
                         wr_ref, rb_ref,
                         x1_ref, gates_ref, s_ref, buf_ref,
                         proj_scr, qs_scr, b_scr, k_scr, o_scr, y_scr):
    tile = SAMPLE_SEQS * SAMPLE_ROWS
    xt = x_ref[...]
    proj_scr[...] = jnp.dot(xt.astype(BF16), win_ref[...], preferred_element_type=F32)

    q = proj_scr[:, OFF_Q:OFF_Q + 512]
    qs_scr[...] = q * _sigmoid(q)
    logf, k = _gate_terms(proj_scr[:, OFF_F:OFF_F + 512], p_ref)
    valid = (lax.broadcasted_iota(jnp.int32, (tile, 512), 0) & (SAMPLE_ROWS - 1)) < n_valid
    k_scr[...] = jnp.where(valid, k, 0.0)
    b_scr[...] = _seg_cumsum(jnp.where(valid, logf, 0.0), SAMPLE_ROWS)

    trow = lax.broadcasted_iota(jnp.int32, (SAMPLE_ROWS, 1), 0)
    urow = lax.broadcasted_iota(jnp.int32, (SAMPLE_ROWS, 512), 0)

    def seq_body(s, carry):
        r0 = pl.multiple_of(s * SAMPLE_ROWS, SAMPLE_ROWS)
        rows = pl.ds(r0, SAMPLE_ROWS)
        qs_c = qs_scr[rows, :]
        b_c = b_scr[rows, :]
        k_c = k_scr[rows, :]
        v_c = proj_scr[rows, OFF_I:OFF_I + 512]
        b_last = b_c[n_valid - 1:n_valid, :]
        qb = (qs_c * jnp.exp(b_c)).astype(BF16)
        kend = (k_c * jnp.exp(b_last - b_c)).astype(BF16)
        dec = jnp.exp(b_last)
        for h in range(HEADS):
            sl = slice(h * HEAD_DIM, (h + 1) * HEAD_DIM)
            st = jnp.transpose(s0_ref[s, h])
            o_h = lax.dot_general(qb[:, sl], st.astype(BF16), NT_DIMS,
                                  preferred_element_type=F32)
            for t in range(n_valid):
                dlt = jnp.minimum(b_c[:, sl] - b_c[t:t + 1, sl], 0.0)
                a_col = jnp.sum(qs_c[:, sl] * k_c[t:t + 1, sl] * jnp.exp(dlt),
                                axis=-1, keepdims=True)
                a_col = jnp.where(trow >= t, a_col, 0.0)
                o_h = o_h + a_col * v_c[t:t + 1, sl]
            o_scr[rows, sl] = o_h
            d_st = lax.dot_general(v_c[:, sl].astype(BF16), kend[:, sl], TN_DIMS,
                                   preferred_element_type=F32)
            s_ref[s, h] = jnp.transpose(st * dec[:, sl] + d_st)
        u = proj_scr[rows, OFF_GC:OFF_GC + 512] * proj_scr[rows, OFF_CX:OFF_CX + 512]
        prev2 = cbuf_ref[s, 0:1, :]
        prev1 = cbuf_ref[s, 1:2, :]
        u1 = jnp.where(urow == 0, prev1, pltpu.roll(u, 1, axis=0))
        u2 = jnp.where(urow == 0, prev2, jnp.where(urow == 1, prev1, pltpu.roll(u, 2, axis=0)))
        y_scr[rows, :] = (p_ref[P_CW0:P_CW0 + 1, :] * u2 + p_ref[P_CW1:P_CW1 + 1, :] * u1
                          + p_ref[P_CW2:P_CW2 + 1, :] * u)
        buf_ref[s] = u[n_valid - (CONV_K - 1):n_valid, :]
        return carry

    lax.fori_loop(0, SAMPLE_SEQS, seq_body, 0)

    yc_in = proj_scr[:, OFF_GB:OFF_GB + 512] * y_scr[...]
    _post_mix(xt, o_scr[...], proj_scr[:, OFF_G:OFF_G + 512], yc_in, p_ref, wout_ref, ln_ref,
              wr_ref, rb_ref, x1_ref, gates_ref)


def _moe_kernel(x_ref, gates_ref, wg_ref, wu_ref, wd_ref, ln_ref, out_ref, xb_scr, acc_scr):
    e = pl.program_id(1)

    @pl.when(e == 0)
    def _():
        xb_scr[...] = x_ref[...].astype(BF16)
        acc_scr[...] = jnp.zeros_like(acc_scr)

    xb = xb_scr[...]
    hg = jnp.dot(xb, wg_ref[0], preferred_element_type=F32)
    hu = jnp.dot(xb, wu_ref[0], preferred_element_type=F32)
    h = (hg * _sigmoid(hg) * hu).astype(BF16)
    y = jnp.dot(h, wd_ref[0], preferred_element_type=F32)
    gates = gates_ref[...]
    lane = lax.broadcasted_iota(jnp.int32, gates.shape, 1)
    gcol = jnp.sum(jnp.where(lane == e, gates, 0.0), axis=-1, keepdims=True)
    acc_scr[...] += gcol * y

    @pl.when(e == pl.num_programs(1) - 1)
    def _():
        r = ALPHA * x_ref[...] + acc_scr[...]
        out_ref[...] = _layer_norm(r, ln_ref[0:1, :], ln_ref[1:2, :])


def _full(shape):
    return pl.BlockSpec(shape, lambda *_: (0,) * len(shape))


def _prompt_mixer(x, win, wout, p, ln, wr, rb, batch, seq):
    n_t = seq // PROMPT_TILE
    n = batch * seq
    row_spec = lambda w: pl.BlockSpec((PROMPT_TILE, w), lambda b, j: (b * n_t + j, 0))
    return pl.pallas_call(
        _prompt_mixer_kernel,
        grid=(batch, n_t),
        in_specs=[row_spec(D_MODEL), _full((D_MODEL, N_IN)), _full((D_MODEL, D_MODEL)),
                  _full((8, 512)), _full((2, D_MODEL)), _full((2 * N_EXPERTS, D_MODEL)),
                  _full((N_EXPERTS, 1))],
        out_specs=[row_spec(D_MODEL), row_spec(LANES),
                   pl.BlockSpec((1, HEADS, HEAD_DIM, HEAD_DIM), lambda b, j: (b, 0, 0, 0)),
                   pl.BlockSpec((1, CONV_K - 1, CONV_DIM), lambda b, j: (b, 0, 0))],
        out_shape=[jax.ShapeDtypeStruct((n, D_MODEL), F32),
                   jax.ShapeDtypeStruct((n, LANES), F32),
                   jax.ShapeDtypeStruct((batch, HEADS, HEAD_DIM, HEAD_DIM), F32),
                   jax.ShapeDtypeStruct((batch, CONV_K - 1, CONV_DIM), F32)],
        scratch_shapes=[pltpu.VMEM((PROMPT_TILE, N_IN), F32)]
        + [pltpu.VMEM((PROMPT_TILE, 512), F32)] * 4
        + [pltpu.VMEM((HEADS, HEAD_DIM, HEAD_DIM), F32), pltpu.VMEM((SUBLANES, CONV_DIM), F32)],
        compiler_params=pltpu.CompilerParams(
            dimension_semantics=("arbitrary", "arbitrary"), vmem_limit_bytes=VMEM_LIMIT),
        name="prompt_mixer",
    )(x, win, wout, p, ln, wr, rb)


def _sample_mixer(x, s0, cbuf, win, wout, p, ln, wr, rb, batch, n_valid):
    tile = SAMPLE_SEQS * SAMPLE_ROWS
    n = batch * SAMPLE_ROWS
    row_spec = lambda w: pl.BlockSpec((tile, w), lambda i: (i, 0))
    st_spec = pl.BlockSpec((SAMPLE_SEQS, HEADS, HEAD_DIM, HEAD_DIM), lambda i: (i, 0, 0, 0))
    cb_spec = pl.BlockSpec((SAMPLE_SEQS, CONV_K - 1, CONV_DIM), lambda i: (i, 0, 0))
    return pl.pallas_call(
        functools.partial(_sample_mixer_kernel, n_valid),
        grid=(batch // SAMPLE_SEQS,),
        in_specs=[row_spec(D_MODEL), st_spec, cb_spec, _full((D_MODEL, N_IN)),
                  _full((D_MODEL, D_MODEL)), _full((8, 512)), _full((2, D_MODEL)),
                  _full((2 * N_EXPERTS, D_MODEL)), _full((N_EXPERTS, 1))],
        out_specs=[row_spec(D_MODEL), row_spec(LANES), st_spec, cb_spec],
        out_shape=[jax.ShapeDtypeStruct((n, D_MODEL), F32),
                   jax.ShapeDtypeStruct((n, LANES), F32),
                   jax.ShapeDtypeStruct((batch, HEADS, HEAD_DIM, HEAD_DIM), F32),
                   jax.ShapeDtypeStruct((batch, CONV_K - 1, CONV_DIM), F32)],
        scratch_shapes=[pltpu.VMEM((tile, N_IN), F32)] + [pltpu.VMEM((tile, 512), F32)] * 5,
        compiler_params=pltpu.CompilerParams(
            dimension_semantics=("arbitrary",), vmem_limit_bytes=VMEM_LIMIT),
        name="sample_mixer",
    )(x, s0, cbuf, win, wout, p, ln, wr, rb)


def _moe(x, gates, wg, wu, wd, ln):
    n = x.shape[0]
    tile = min(MOE_TILE, n)
    row_spec = lambda w: pl.BlockSpec((tile, w), lambda i, e: (i, 0))
    return pl.pallas_call(
        _moe_kernel,
        grid=(n // tile, N_EXPERTS),
        in_specs=[row_spec(D_MODEL), row_spec(LANES),
                  pl.BlockSpec((1, D_MODEL, EXPERT_FF), lambda i, e: (e, 0, 0)),
                  pl.BlockSpec((1, D_MODEL, EXPERT_FF), lambda i, e: (e, 0, 0)),
                  pl.BlockSpec((1, EXPERT_FF, D_MODEL), lambda i, e: (e, 0, 0)),
                  pl.BlockSpec((2, D_MODEL), lambda i, e: (0, 0))],
        out_specs=row_spec(D_MODEL),
        out_shape=jax.ShapeDtypeStruct((n, D_MODEL), F32),
        scratch_shapes=[pltpu.VMEM((tile, D_MODEL), BF16), pltpu.VMEM((tile, D_MODEL), F32)],
        compiler_params=pltpu.CompilerParams(
            dimension_semantics=("arbitrary", "arbitrary"), vmem_limit_bytes=VMEM_LIMIT),
        name="moe",
    )(x, gates, wg, wu, wd, ln)


def kernel(x_prompt, x_sample, state_hgrn, state_conv, w_in, w_out, lower_bounds, hgrn_norm_g,
           conv_w, conv_norm_g, ln1_g, ln1_b, ln2_g, ln2_b, w_router, router_bias,
           w_gate, w_up, w_down):
    batch, seq, _ = x_prompt.shape
    dec_batch, dec_seq, _ = x_sample.shape
    assert seq % PROMPT_TILE == 0 and dec_batch % SAMPLE_SEQS == 0
    assert CONV_K - 1 <= dec_seq <= SAMPLE_ROWS

    lb = jnp.cumsum(jax.nn.softmax(lower_bounds.astype(F32), axis=0), axis=0)
    lb = lb - lb[0:1]
    params = jnp.stack([jnp.log(lb), jnp.log1p(-lb), 1.0 - lb, hgrn_norm_g, conv_norm_g,
                        conv_w[:, 0], conv_w[:, 1], conv_w[:, 2]], axis=1)
    ln1 = jnp.stack([ln1_g, ln1_b], axis=1)
    ln2 = jnp.stack([ln2_g, ln2_b], axis=1)
    wr_hi = w_router.astype(BF16)
    wr_lo = (w_router - wr_hi.astype(F32)).astype(BF16)
    wr = jnp.concatenate([wr_hi.T, wr_lo.T], axis=0)
    rb = router_bias.astype(F32).reshape(N_EXPERTS, 1)
    win_b, wout_b = w_in.astype(BF16), w_out.astype(BF16)
    wg_b, wu_b, wd_b = w_gate.astype(BF16), w_up.astype(BF16), w_down.astype(BF16)

    xp = x_prompt.reshape(batch * seq, D_MODEL)
    xs = jnp.pad(x_sample, ((0, 0), (0, SAMPLE_ROWS - dec_seq), (0, 0)))
    xs = xs.reshape(dec_batch * SAMPLE_ROWS, D_MODEL)

    s_p, b_p, s_s, b_s = [], [], [], []
    for l in range(DEPTH):
        x1, gates, s_l, b_l = _prompt_mixer(xp, win_b[l], wout_b[l], params[l], ln1[l], wr, rb,
                                            batch, seq)
        xp = _moe(x1, gates, wg_b[l], wu_b[l], wd_b[l], ln2[l])
        s_p.append(s_l)
        b_p.append(b_l)
        x1, gates, s_l, b_l = _sample_mixer(xs, state_hgrn[l], state_conv[l], win_b[l], wout_b[l],
                                            params[l], ln1[l], wr, rb, dec_batch, dec_seq)
        xs = _moe(x1, gates, wg_b[l], wu_b[l], wd_b[l], ln2[l])
        s_s.append(s_l)
        b_s.append(b_l)

    y_prompt = xp.reshape(batch, seq, D_MODEL)
    y_sample = xs.reshape(dec_batch, SAMPLE_ROWS, D_MODEL)[:, :dec_seq]
    return (y_prompt, y_sample, jnp.stack(s_p), jnp.stack(b_p), jnp.stack(s_s), jnp.stack(b_s))
```

```python
import functools

import jax
import jax.numpy as jnp
from jax import lax
from jax.experimental import pallas as pl
from jax.experimental.pallas import tpu as pltpu

F32 = jnp.float32
BF16 = jnp.bfloat16
I32 = jnp.int32

D_MODEL = 1024
DEPTH = 2
HEADS = 4
HEAD_DIM = 128
HGRN_W = HEADS * HEAD_DIM
CONV_DIM = 512
CONV_GROUPS = 8
CONV_K = 3
N_IN = 7 * 512
N_EXPERTS = 16
N_GROUPS = 4
PER_GROUP = 4
N_PAIRS = 6
N_CLASSES = N_GROUPS * N_PAIRS
PAIR_LO = (0, 0, 0, 1, 1, 2)
PAIR_HI = (1, 2, 3, 2, 3, 3)
EXPERT_FF = 512
ALPHA = (2 * DEPTH) ** 0.25
LN_EPS = 1e-5
RMS_EPS = 1e-6

LANES = 128
SUBLANES = 8
CHUNK = 64
PROMPT_TILE = 256
SAMPLE_ROWS = 8
SAMPLE_SEQS = 16
MOE_TILE = 256
ROW_W = D_MODEL + LANES
DMA_CHUNK = 256
VMEM_LIMIT = 56 * 1024 * 1024

OFF_Q, OFF_F, OFF_I, OFF_G, OFF_GB, OFF_GC, OFF_CX = (i * 512 for i in range(7))

P_LOGLB, P_LOG1MLB, P_OMLB, P_HNORM, P_CNORM, P_CW0, P_CW1, P_CW2 = range(8)

NT_DIMS = (((1,), (1,)), ((), ()))
TN_DIMS = (((0,), (0,)), ((), ()))


def _sigmoid(x):
    return 1.0 / (1.0 + jnp.exp(-x))


def _seg_cumsum(x, seg):
    row = lax.broadcasted_iota(I32, x.shape, 0)
    pos = row & (seg - 1)
    sh = 1
    while sh < seg:
        x = x + jnp.where(pos >= sh, pltpu.roll(x, sh, axis=0), 0.0)
        sh *= 2
    return x


def _gate_terms(z, p_ref):
    e = jnp.exp(-jnp.abs(z))
    inv = 1.0 / (1.0 + e)
    logsig = jnp.minimum(z, 0.0) - jnp.log(1.0 + e)
    a = p_ref[0, P_LOGLB:P_LOGLB + 1, :]
    b = p_ref[0, P_LOG1MLB:P_LOG1MLB + 1, :] + logsig
    logf = jnp.maximum(a, b) + jnp.log(1.0 + jnp.exp(-jnp.abs(a - b)))
    k = p_ref[0, P_OMLB:P_OMLB + 1, :] * (jnp.where(z >= 0.0, e, 1.0) * inv)
    return logf, k


def _group_rms(x, n_groups):
    width = x.shape[-1] // n_groups
    x2 = x * x
    outs = []
    for s in range(x.shape[-1] // LANES):
        xs = x[:, s * LANES:(s + 1) * LANES]
        x2s = x2[:, s * LANES:(s + 1) * LANES]
        if width == LANES:
            ms = jnp.sum(x2s, axis=-1, keepdims=True) * (1.0 / width)
            scale = lax.rsqrt(ms + RMS_EPS)
        else:
            lane = lax.broadcasted_iota(I32, xs.shape, 1)
            lo = lane < width
            ms_lo = jnp.sum(jnp.where(lo, x2s, 0.0), axis=-1, keepdims=True) * (1.0 / width)
            ms_hi = jnp.sum(jnp.where(lo, 0.0, x2s), axis=-1, keepdims=True) * (1.0 / width)
            scale = jnp.where(lo, lax.rsqrt(ms_lo + RMS_EPS), lax.rsqrt(ms_hi + RMS_EPS))
        outs.append(xs * scale)
    return jnp.concatenate(outs, axis=-1)


def _layer_norm(r, g, b):
    mu = jnp.mean(r, axis=-1, keepdims=True)
    rc = r - mu
    var = jnp.mean(rc * rc, axis=-1, keepdims=True)
    return rc * lax.rsqrt(var + LN_EPS) * g + b


def _route(x1, wr_ref, rb_ref):
    x_hi = x1.astype(BF16)
    x_lo = (x1 - x_hi.astype(F32)).astype(BF16)
    wr = wr_ref[...]
    r1 = lax.dot_general(wr, x_hi, NT_DIMS, preferred_element_type=F32)
    r2 = lax.dot_general(wr, x_lo, NT_DIMS, preferred_element_type=F32)
    lt = r1[0:N_EXPERTS] + r1[N_EXPERTS:] + r2[0:N_EXPERTS] + r2[N_EXPERTS:] + rb_ref[...]
    lg = [lt[e:e + 1, :] for e in range(N_EXPERTS)]
    mx = lg[0]
    for e in range(1, N_EXPERTS):
        mx = jnp.maximum(mx, lg[e])
    ex = [jnp.exp(l - mx) for l in lg]
    best = None
    gi = None
    for g in range(N_GROUPS):
        a, b, c, d = ex[PER_GROUP * g:PER_GROUP * (g + 1)]
        s = jnp.maximum(jnp.maximum(jnp.maximum(a + b, a + c), jnp.maximum(a + d, b + c)),
                        jnp.maximum(b + d, c + d))
        if g == 0:
            best, gi = s, jnp.zeros(s.shape, I32)
        else:
            upd = s > best
            best = jnp.where(upd, s, best)
            gi = jnp.where(upd, g, gi)
    v = []
    for i in range(PER_GROUP):
        vi = ex[3 * PER_GROUP + i]
        for g in (2, 1, 0):
            vi = jnp.where(gi == g, ex[PER_GROUP * g + i], vi)
        v.append(vi)
    w1, i1 = v[0], jnp.zeros(v[0].shape, I32)
    for i in range(1, PER_GROUP):
        upd = v[i] > w1
        w1 = jnp.where(upd, v[i], w1)
        i1 = jnp.where(upd, i, i1)
    w2, i2 = None, None
    for i in range(PER_GROUP):
        vi = jnp.where(i1 == i, -1.0, v[i])
        if i == 0:
            w2, i2 = vi, jnp.zeros(vi.shape, I32)
        else:
            upd = vi > w2
            w2 = jnp.where(upd, vi, w2)
            i2 = jnp.where(upd, i, i2)
    inv = 1.0 / (w1 + w2)
    first_lo = i1 < i2
    lo = jnp.where(first_lo, i1, i2)
    hi = jnp.where(first_lo, i2, i1)
    g_lo = jnp.where(first_lo, w1, w2) * inv
    g_hi = jnp.where(first_lo, w2, w1) * inv
    pair = jnp.where(lo == 0, 0, jnp.where(lo == 1, 3, 5)) + hi - lo - 1
    return gi * N_PAIRS + pair, g_lo, g_hi


def _post_mix(xt, o, g, yc_in, p_ref, wout_scr, ln_ref, wr_ref, rb_ref, x1_ref, route_ref):
    rows = xt.shape[0]
    o = _group_rms(o, HEADS) * p_ref[0, P_HNORM:P_HNORM + 1, :]
    o = o * (g * _sigmoid(g))
    yc = _group_rms(yc_in, CONV_GROUPS) * p_ref[0, P_CNORM:P_CNORM + 1, :]
    mix = jnp.concatenate([o, yc], axis=-1).astype(BF16)
    h = jnp.dot(mix, wout_scr[...], preferred_element_type=F32)
    x1 = _layer_norm(ALPHA * xt + h, ln_ref[0, 0:1, :], ln_ref[0, 1:2, :])
    cls, g_lo, g_hi = _route(x1, wr_ref, rb_ref)
    x1_ref[:, 0:D_MODEL] = x1
    sub = lax.broadcasted_iota(I32, (LANES, rows), 0)
    gates_t = jnp.where(sub == 0, g_lo, jnp.where(sub == 1, g_hi, 0.0))
    x1_ref[:, D_MODEL:ROW_W] = jnp.transpose(gates_t)
    sub8 = lax.broadcasted_iota(I32, (SUBLANES, rows), 0)
    route_ref[...] = jnp.where(sub8 == 0, cls.astype(F32), 0.0)


def _prompt_mixer_kernel(x_ref, win_ref, wout_ref, p_ref, ln_ref, wr_ref, rb_ref,
                         x1_ref, route_ref, s_ref, buf_ref,
                         win_scr, wout_scr, proj_scr, qs_scr, b_scr, k_scr, o_scr, st_scr,
                         ubuf_scr):
    j = pl.program_id(1)

    @pl.when((pl.program_id(0) == 0) & (j == 0))
    def _():
        win_scr[...] = win_ref[0].astype(BF16)
        wout_scr[...] = wout_ref[0].astype(BF16)

    @pl.when(j == 0)
    def _():
        st_scr[...] = jnp.zeros_like(st_scr)
        ubuf_scr[...] = jnp.zeros_like(ubuf_scr)

    xt = x_ref[...]
    proj_scr[...] = jnp.dot(xt.astype(BF16), win_scr[...], preferred_element_type=F32)

    q = proj_scr[:, OFF_Q:OFF_Q + 512]
    qs_scr[...] = q * _sigmoid(q)
    logf, k = _gate_terms(proj_scr[:, OFF_F:OFF_F + 512], p_ref)
    k_scr[...] = k
    b_scr[...] = _seg_cumsum(logf, CHUNK)

    tri = (lax.broadcasted_iota(I32, (CHUNK, CHUNK), 0)
           >= lax.broadcasted_iota(I32, (CHUNK, CHUNK), 1))

    def chunk_body(c, carry):
        r0 = pl.multiple_of(c * CHUNK, CHUNK)
        rows = pl.ds(r0, CHUNK)
        qs_c = qs_scr[rows, :]
        b_c = b_scr[rows, :]
        k_c = k_scr[rows, :]
        v_c = proj_scr[rows, OFF_I:OFF_I + 512].astype(BF16)
        b_mid = b_c[CHUNK // 2 - 1:CHUNK // 2, :]
        b_last = b_c[CHUNK - 1:CHUNK, :]
        qd = (qs_c * jnp.exp(b_c - b_mid)).astype(BF16)
        kd = (k_c * jnp.exp(b_mid - b_c)).astype(BF16)
        qb = (qs_c * jnp.exp(b_c)).astype(BF16)
        kend = (k_c * jnp.exp(b_last - b_c)).astype(BF16)
        dec = jnp.exp(b_last)
        for h in range(HEADS):
            sl = slice(h * HEAD_DIM, (h + 1) * HEAD_DIM)
            att = lax.dot_general(qd[:, sl], kd[:, sl], NT_DIMS, preferred_element_type=F32)
            att = jnp.where(tri, att, 0.0).astype(BF16)
            st = st_scr[h]
            o_h = jnp.dot(att, v_c[:, sl], preferred_element_type=F32)
            o_h = o_h + lax.dot_general(qb[:, sl], st.astype(BF16), NT_DIMS,
                                        preferred_element_type=F32)
            o_scr[rows, sl] = o_h
            d_st = lax.dot_general(v_c[:, sl], kend[:, sl], TN_DIMS, preferred_element_type=F32)
            st_scr[h] = st * dec[:, sl] + d_st
        return carry

    lax.fori_loop(0, PROMPT_TILE // CHUNK, chunk_body, 0)

    u = proj_scr[:, OFF_GC:OFF_GC + 512] * proj_scr[:, OFF_CX:OFF_CX + 512]
    row = lax.broadcasted_iota(I32, u.shape, 0)
    prev2 = ubuf_scr[SUBLANES - 2:SUBLANES - 1, :]
    prev1 = ubuf_scr[SUBLANES - 1:SUBLANES, :]
    u1 = jnp.where(row == 0, prev1, pltpu.roll(u, 1, axis=0))
    u2 = jnp.where(row == 0, prev2, jnp.where(row == 1, prev1, pltpu.roll(u, 2, axis=0)))
    y = (p_ref[0, P_CW0:P_CW0 + 1, :] * u2 + p_ref[0, P_CW1:P_CW1 + 1, :] * u1
         + p_ref[0, P_CW2:P_CW2 + 1, :] * u)
    ubuf_scr[...] = u[PROMPT_TILE - SUBLANES:, :]
    yc_in = proj_scr[:, OFF_GB:OFF_GB + 512] * y

    @pl.when(j == pl.num_programs(1) - 1)
    def _():
        for h in range(HEADS):
            s_ref[0, h] = jnp.transpose(st_scr[h])
        buf_ref[0] = u[PROMPT_TILE - (CONV_K - 1):, :]

    _post_mix(xt, o_scr[...], proj_scr[:, OFF_G:OFF_G + 512], yc_in, p_ref, wout_scr, ln_ref,
              wr_ref, rb_ref, x1_ref, route_ref)


def _sample_mixer_kernel(n_valid, x_ref, s0_ref, cbuf_ref, win_ref, wout_ref, p_ref, ln_ref,
                         wr_ref, rb_ref,
                         x1_ref, route_ref, s_ref, buf_ref,
                         win_scr, wout_scr, proj_scr, qs_scr, b_scr, k_scr, o_scr, y_scr):
    tile = SAMPLE_SEQS * SAMPLE_ROWS

    @pl.when(pl.program_id(0) == 0)
    def _():
        win_scr[...] = win_ref[0].astype(BF16)
        wout_scr[...] = wout_ref[0].astype(BF16)

    xt = x_ref[...]
    proj_scr[...] = jnp.dot(xt.astype(BF16), win_scr[...], preferred_element_type=F32)

    q = proj_scr[:, OFF_Q:OFF_Q + 512]
    qs_scr[...] = q * _sigmoid(q)
    logf, k = _gate_terms(proj_scr[:, OFF_F:OFF_F + 512], p_ref)
    valid = (lax.broadcasted_iota(I32, (tile, 512), 0) & (SAMPLE_ROWS - 1)) < n_valid
    k_scr[...] = jnp.where(valid, k, 0.0)
    b_scr[...] = _seg_cumsum(jnp.where(valid, logf, 0.0), SAMPLE_ROWS)

    trow = lax.broadcasted_iota(I32, (SAMPLE_ROWS, 1), 0)
    urow = lax.broadcasted_iota(I32, (SAMPLE_ROWS, 512), 0)

    def seq_body(s, carry):
        r0 = pl.multiple_of(s * SAMPLE_ROWS, SAMPLE_ROWS)
        rows = pl.ds(r0, SAMPLE_ROWS)
        qs_c = qs_scr[rows, :]
        b_c = b_scr[rows, :]
        k_c = k_scr[rows, :]
        v_c = proj_scr[rows, OFF_I:OFF_I + 512]
        b_last = b_c[n_valid - 1:n_valid, :]
        qb = (qs_c * jnp.exp(b_c)).astype(BF16)
        kend = (k_c * jnp.exp(b_last - b_c)).astype(BF16)
        dec = jnp.exp(b_last)
        for h in range(HEADS):
            sl = slice(h * HEAD_DIM, (h + 1) * HEAD_DIM)
            st = jnp.transpose(s0_ref[0, s, h])
            o_h = lax.dot_general(qb[:, sl], st.astype(BF16), NT_DIMS,
                                  preferred_element_type=F32)
            for t in range(n_valid):
                dlt = jnp.minimum(b_c[:, sl] - b_c[t:t + 1, sl], 0.0)
                a_col = jnp.sum(qs_c[:, sl] * k_c[t:t + 1, sl] * jnp.exp(dlt),
                                axis=-1, keepdims=True)
                a_col = jnp.where(trow >= t, a_col, 0.0)
                o_h = o_h + a_col * v_c[t:t + 1, sl]
            o_scr[rows, sl] = o_h
            d_st = lax.dot_general(v_c[:, sl].astype(BF16), kend[:, sl], TN_DIMS,
                                   preferred_element_type=F32)
            s_ref[s, h] = jnp.transpose(st * dec[:, sl] + d_st)
        u = proj_scr[rows, OFF_GC:OFF_GC + 512] * proj_scr[rows, OFF_CX:OFF_CX + 512]
        prev2 = cbuf_ref[0, s, 0:1, :]
        prev1 = cbuf_ref[0, s, 1:2, :]
        u1 = jnp.where(urow == 0, prev1, pltpu.roll(u, 1, axis=0))
        u2 = jnp.where(urow == 0, prev2, jnp.where(urow == 1, prev1, pltpu.roll(u, 2, axis=0)))
        y_scr[rows, :] = (p_ref[0, P_CW0:P_CW0 + 1, :] * u2 + p_ref[0, P_CW1:P_CW1 + 1, :] * u1
                          + p_ref[0, P_CW2:P_CW2 + 1, :] * u)
        buf_ref[s] = u[n_valid - (CONV_K - 1):n_valid, :]
        return carry

    lax.fori_loop(0, SAMPLE_SEQS, seq_body, 0)

    yc_in = proj_scr[:, OFF_GB:OFF_GB + 512] * y_scr[...]
    _post_mix(xt, o_scr[...], proj_scr[:, OFF_G:OFF_G + 512], yc_in, p_ref, wout_scr, ln_ref,
              wr_ref, rb_ref, x1_ref, route_ref)


def _plan_kernel(cls_ref, pos_ref, stats_ref):
    n_rows = cls_ref.shape[0]
    cls = cls_ref[...]
    upper = (lax.broadcasted_iota(I32, (LANES, LANES), 0)
             < lax.broadcasted_iota(I32, (LANES, LANES), 1)).astype(BF16)
    lower = (lax.broadcasted_iota(I32, (n_rows, n_rows), 1)
             < lax.broadcasted_iota(I32, (n_rows, n_rows), 0)).astype(BF16)
    lane = lax.broadcasted_iota(I32, (n_rows, LANES), 1)
    row_tot = jnp.zeros((n_rows, LANES), F32)
    for c in range(N_CLASSES):
        oh = jnp.where(cls == c, 1.0, 0.0)
        row_tot = jnp.where(lane == c, jnp.sum(oh, axis=1, keepdims=True), row_tot)
    before = jnp.dot(lower, row_tot.astype(BF16), preferred_element_type=F32)
    cnt = jnp.sum(row_tot, axis=0, keepdims=True)
    tiles = jnp.floor((cnt + (MOE_TILE - 1)) * (1.0 / MOE_TILE))
    first_tile = jnp.dot(jnp.broadcast_to(tiles, (SUBLANES, LANES)).astype(BF16), upper,
                         preferred_element_type=F32)[0:1]
    base = before + first_tile * MOE_TILE
    pos = jnp.zeros((n_rows, LANES), F32)
    for c in range(N_CLASSES):
        oh = jnp.where(cls == c, 1.0, 0.0)
        local = jnp.dot(oh.astype(BF16), upper, preferred_element_type=F32)
        pos = pos + oh * (base[:, c:c + 1] + local)
    pos_ref[...] = pos.astype(I32)
    sub = lax.broadcasted_iota(I32, (SUBLANES, LANES), 0)
    stats_ref[...] = jnp.where(sub == 0, cnt, jnp.where(sub == 1, tiles,
                                                        jnp.where(sub == 2, first_tile, 0.0)))


def _row_copy(src_ref, src_row, dst_ref, dst_row, sem):
    return pltpu.make_async_copy(src_ref.at[pl.ds(src_row, 1), :],
                                 dst_ref.at[pl.ds(dst_row, 1), :], sem)


def _chunk_copy(src_ref, dst_ref, sem):
    return pltpu.make_async_copy(src_ref.at[pl.ds(0, DMA_CHUNK), :],
                                 dst_ref.at[pl.ds(0, DMA_CHUNK), :], sem)


def _permute_rows(n, row_copy, chunk_copy):
    n_chunks = n // DMA_CHUNK

    def issue(c):
        def body(i, carry):
            row_copy(c * DMA_CHUNK + i).start()
            return carry
        lax.fori_loop(0, DMA_CHUNK, body, 0, unroll=8)

    def step(c, carry):
        issue(c)
        chunk_copy.wait()
        return carry

    issue(0)
    lax.fori_loop(1, n_chunks, step, 0)
    chunk_copy.wait()


def _scatter_kernel(n_p, n_s, pos_ref, zst_ref, xp_ref, xs_ref, out_ref, zbuf, sem_z, sem_r):
    zbuf[...] = jnp.zeros_like(zbuf)

    def zero_copy(c):
        start = pl.multiple_of(zst_ref[c], MOE_TILE)
        return pltpu.make_async_copy(zbuf, out_ref.at[pl.ds(start, MOE_TILE), :], sem_z)

    def z_start(c, carry):
        @pl.when(zst_ref[c] >= 0)
        def _():
            zero_copy(c).start()
        return carry

    def z_wait(c, carry):
        @pl.when(zst_ref[c] >= 0)
        def _():
            zero_copy(c).wait()
        return carry

    lax.fori_loop(0, N_CLASSES, z_start, 0)
    lax.fori_loop(0, N_CLASSES, z_wait, 0)

    _permute_rows(n_p, lambda t: _row_copy(xp_ref, t, out_ref, pos_ref[t], sem_r),
                  _chunk_copy(xp_ref, out_ref, sem_r))
    _permute_rows(n_s, lambda t: _row_copy(xs_ref, t, out_ref, pos_ref[n_p + t], sem_r),
                  _chunk_copy(xs_ref, out_ref, sem_r))


def _gather_kernel(n_p, n_s, pos_ref, ys_ref, outp_ref, outs_ref, sem_r):
    _permute_rows(n_p, lambda t: _row_copy(ys_ref, pos_ref[t], outp_ref, t, sem_r),
                  _chunk_copy(ys_ref, outp_ref, sem_r))
    _permute_rows(n_s, lambda t: _row_copy(ys_ref, pos_ref[n_p + t], outs_ref, t, sem_r),
                  _chunk_copy(ys_ref, outs_ref, sem_r))


def _expert_kernel(elo_ref, ehi_ref, blk_ref, flag_ref,
                   xs_ref, wg_lo, wu_lo, wd_lo, wg_hi, wu_hi, wd_hi, ln_ref, out_ref,
                   w1_scr, wd_scr):
    i = pl.program_id(0)
    flag = flag_ref[i]

    @pl.when((flag & 2) != 0)
    def _():
        w1_scr[:, 0 * EXPERT_FF:1 * EXPERT_FF] = wg_lo[0, 0].astype(BF16)
        w1_scr[:, 1 * EXPERT_FF:2 * EXPERT_FF] = wu_lo[0, 0].astype(BF16)
        w1_scr[:, 2 * EXPERT_FF:3 * EXPERT_FF] = wg_hi[0, 0].astype(BF16)
        w1_scr[:, 3 * EXPERT_FF:4 * EXPERT_FF] = wu_hi[0, 0].astype(BF16)
        wd_scr[0:EXPERT_FF, :] = wd_lo[0, 0].astype(BF16)
        wd_scr[EXPERT_FF:, :] = wd_hi[0, 0].astype(BF16)

    @pl.when((flag & 1) != 0)
    def _():
        x = xs_ref[:, 0:D_MODEL]
        g_lo = xs_ref[:, D_MODEL:D_MODEL + 1]
        g_hi = xs_ref[:, D_MODEL + 1:D_MODEL + 2]
        hc = jnp.dot(x.astype(BF16), w1_scr[...], preferred_element_type=F32)
        hg_lo, hu_lo = hc[:, 0:EXPERT_FF], hc[:, EXPERT_FF:2 * EXPERT_FF]
        hg_hi, hu_hi = hc[:, 2 * EXPERT_FF:3 * EXPERT_FF], hc[:, 3 * EXPERT_FF:]
        h_lo = hg_lo * _sigmoid(hg_lo) * hu_lo * g_lo
        h_hi = hg_hi * _sigmoid(hg_hi) * hu_hi * g_hi
        h = jnp.concatenate([h_lo, h_hi], axis=-1).astype(BF16)
        y = jnp.dot(h, wd_scr[...], preferred_element_type=F32)
        out_ref[...] = _layer_norm(ALPHA * x + y, ln_ref[0, 0:1, :], ln_ref[0, 1:2, :])


def _layer_spec(shape, layer, n_grid, single_buffer=False):
    idx = (layer,) + (0,) * len(shape)
    if n_grid == 1:
        index_map = lambda i: idx
    else:
        index_map = lambda i, j: idx
    if single_buffer:
        return pl.BlockSpec((1,) + shape, index_map, pipeline_mode=pl.Buffered(1))
    return pl.BlockSpec((1,) + shape, index_map)


def _prompt_mixer(x, w_in, w_out, params, ln, wr, rb, layer, batch, seq):
    n_t = seq // PROMPT_TILE
    n = batch * seq
    row_in = pl.BlockSpec((PROMPT_TILE, D_MODEL), lambda b, j: (b * n_t + j, 0))
    row_out = pl.BlockSpec((PROMPT_TILE, ROW_W), lambda b, j: (b * n_t + j, 0))
    route_out = pl.BlockSpec((SUBLANES, PROMPT_TILE), lambda b, j: (0, b * n_t + j))
    full2 = lambda shape: pl.BlockSpec(shape, lambda b, j: (0, 0))
    return pl.pallas_call(
        _prompt_mixer_kernel,
        grid=(batch, n_t),
        in_specs=[row_in,
                  _layer_spec((D_MODEL, N_IN), layer, 2, True),
                  _layer_spec((D_MODEL, D_MODEL), layer, 2, True),
                  _layer_spec((8, 512), layer, 2), _layer_spec((2, D_MODEL), layer, 2),
                  full2((2 * N_EXPERTS, D_MODEL)), full2((N_EXPERTS, 1))],
        out_specs=[row_out, route_out,
                   pl.BlockSpec((1, HEADS, HEAD_DIM, HEAD_DIM), lambda b, j: (b, 0, 0, 0)),
                   pl.BlockSpec((1, CONV_K - 1, CONV_DIM), lambda b, j: (b, 0, 0))],
        out_shape=[jax.ShapeDtypeStruct((n, ROW_W), F32),
                   jax.ShapeDtypeStruct((SUBLANES, n), F32),
                   jax.ShapeDtypeStruct((batch, HEADS, HEAD_DIM, HEAD_DIM), F32),
                   jax.ShapeDtypeStruct((batch, CONV_K - 1, CONV_DIM), F32)],
        scratch_shapes=[pltpu.VMEM((D_MODEL, N_IN), BF16), pltpu.VMEM((D_MODEL, D_MODEL), BF16),
                        pltpu.VMEM((PROMPT_TILE, N_IN), F32)]
        + [pltpu.VMEM((PROMPT_TILE, 512), F32)] * 4
        + [pltpu.VMEM((HEADS, HEAD_DIM, HEAD_DIM), F32), pltpu.VMEM((SUBLANES, CONV_DIM), F32)],
        compiler_params=pltpu.CompilerParams(
            dimension_semantics=("arbitrary", "arbitrary"), vmem_limit_bytes=VMEM_LIMIT),
        name="prompt_mixer",
    )(x, w_in, w_out, params, ln, wr, rb)


def _sample_mixer(x, s0, cbuf, w_in, w_out, params, ln, wr, rb, layer, batch, n_valid):
    tile = SAMPLE_SEQS * SAMPLE_ROWS
    n = batch * SAMPLE_ROWS
    row_in = pl.BlockSpec((tile, D_MODEL), lambda i: (i, 0))
    row_out = pl.BlockSpec((tile, ROW_W), lambda i: (i, 0))
    route_out = pl.BlockSpec((SUBLANES, tile), lambda i: (0, i))
    st_spec = pl.BlockSpec((1, SAMPLE_SEQS, HEADS, HEAD_DIM, HEAD_DIM),
                           lambda i: (layer, i, 0, 0, 0))
    cb_spec = pl.BlockSpec((1, SAMPLE_SEQS, CONV_K - 1, CONV_DIM), lambda i: (layer, i, 0, 0))
    st_out = pl.BlockSpec((SAMPLE_SEQS, HEADS, HEAD_DIM, HEAD_DIM), lambda i: (i, 0, 0, 0))
    cb_out = pl.BlockSpec((SAMPLE_SEQS, CONV_K - 1, CONV_DIM), lambda i: (i, 0, 0))
    full1 = lambda shape: pl.BlockSpec(shape, lambda i: (0, 0))
    return pl.pallas_call(
        functools.partial(_sample_mixer_kernel, n_valid),
        grid=(batch // SAMPLE_SEQS,),
        in_specs=[row_in, st_spec, cb_spec,
                  _layer_spec((D_MODEL, N_IN), layer, 1, True),
                  _layer_spec((D_MODEL, D_MODEL), layer, 1, True),
                  _layer_spec((8, 512), layer, 1), _layer_spec((2, D_MODEL), layer, 1),
                  full1((2 * N_EXPERTS, D_MODEL)), full1((N_EXPERTS, 1))],
        out_specs=[row_out, route_out, st_out, cb_out],
        out_shape=[jax.ShapeDtypeStruct((n, ROW_W), F32),
                   jax.ShapeDtypeStruct((SUBLANES, n), F32),
                   jax.ShapeDtypeStruct((batch, HEADS, HEAD_DIM, HEAD_DIM), F32),
                   jax.ShapeDtypeStruct((batch, CONV_K - 1, CONV_DIM), F32)],
        scratch_shapes=[pltpu.VMEM((D_MODEL, N_IN), BF16), pltpu.VMEM((D_MODEL, D_MODEL), BF16),
                        pltpu.VMEM((tile, N_IN), F32)] + [pltpu.VMEM((tile, 512), F32)] * 5,
        compiler_params=pltpu.CompilerParams(
            dimension_semantics=("arbitrary",), vmem_limit_bytes=VMEM_LIMIT),
        name="sample_mixer",
    )(x, s0, cbuf, w_in, w_out, params, ln, wr, rb)


def _plan(cls2d):
    n_rows = cls2d.shape[0]
    return pl.pallas_call(
        _plan_kernel,
        out_shape=[jax.ShapeDtypeStruct((n_rows, LANES), I32),
                   jax.ShapeDtypeStruct((SUBLANES, LANES), F32)],
        name="moe_plan",
    )(cls2d)


def _scatter(pos, zstart, x1p, x1s, n_tiles):
    n_p, n_s = x1p.shape[0], x1s.shape[0]
    any_spec = pl.BlockSpec(memory_space=pl.ANY)
    return pl.pallas_call(
        functools.partial(_scatter_kernel, n_p, n_s),
        grid_spec=pltpu.PrefetchScalarGridSpec(
            num_scalar_prefetch=2, grid=(1,),
            in_specs=[any_spec, any_spec], out_specs=any_spec,
            scratch_shapes=[pltpu.VMEM((MOE_TILE, ROW_W), F32),
                            pltpu.SemaphoreType.DMA(()), pltpu.SemaphoreType.DMA(())]),
        out_shape=jax.ShapeDtypeStruct((n_tiles * MOE_TILE, ROW_W), F32),
        compiler_params=pltpu.CompilerParams(dimension_semantics=("arbitrary",)),
        name="moe_scatter",
    )(pos, zstart, x1p, x1s)


def _gather(pos, ys, n_p, n_s):
    any_spec = pl.BlockSpec(memory_space=pl.ANY)
    return pl.pallas_call(
        functools.partial(_gather_kernel, n_p, n_s),
        grid_spec=pltpu.PrefetchScalarGridSpec(
            num_scalar_prefetch=1, grid=(1,),
            in_specs=[any_spec], out_specs=[any_spec, any_spec],
            scratch_shapes=[pltpu.SemaphoreType.DMA(())]),
        out_shape=[jax.ShapeDtypeStruct((n_p, D_MODEL), F32),
                   jax.ShapeDtypeStruct((n_s, D_MODEL), F32)],
        compiler_params=pltpu.CompilerParams(dimension_semantics=("arbitrary",)),
        name="moe_gather",
    )(pos, ys)


def _experts(elo, ehi, blk, flag, xs, w_gate, w_up, w_down, ln, layer, n_tiles):
    def w_spec(shape, table_idx):
        def index_map(i, elo_r, ehi_r, blk_r, flag_r):
            return (layer, (elo_r, ehi_r)[table_idx][i], 0, 0)
        return pl.BlockSpec((1, 1) + shape, index_map)

    up = (D_MODEL, EXPERT_FF)
    down = (EXPERT_FF, D_MODEL)
    return pl.pallas_call(
        _expert_kernel,
        grid_spec=pltpu.PrefetchScalarGridSpec(
            num_scalar_prefetch=4, grid=(n_tiles,),
            in_specs=[pl.BlockSpec((MOE_TILE, ROW_W), lambda i, e0, e1, b, f: (b[i], 0)),
                      w_spec(up, 0), w_spec(up, 0), w_spec(down, 0),
                      w_spec(up, 1), w_spec(up, 1), w_spec(down, 1),
                      pl.BlockSpec((1, 2, D_MODEL), lambda i, e0, e1, b, f: (layer, 0, 0))],
            out_specs=pl.BlockSpec((MOE_TILE, D_MODEL), lambda i, e0, e1, b, f: (b[i], 0)),
            scratch_shapes=[pltpu.VMEM((D_MODEL, 4 * EXPERT_FF), BF16),
                            pltpu.VMEM((2 * EXPERT_FF, D_MODEL), BF16)]),
        out_shape=jax.ShapeDtypeStruct((n_tiles * MOE_TILE, D_MODEL), F32),
        compiler_params=pltpu.CompilerParams(
            dimension_semantics=("arbitrary",), vmem_limit_bytes=VMEM_LIMIT),
        name="moe_experts",
    )(elo, ehi, blk, flag, xs, w_gate, w_up, w_down, w_gate, w_up, w_down, ln)


def _moe(x1p, route_p, x1s, route_s, w_gate, w_up, w_down, ln2, layer):
    n_p, n_s = x1p.shape[0], x1s.shape[0]
    n = n_p + n_s
    n_tiles = n // MOE_TILE + N_CLASSES
    cls2d = jnp.concatenate([route_p[0], route_s[0]]).reshape(n // LANES, LANES)
    pos2d, stats = _plan(cls2d)
    pos = pos2d.reshape(n)

    tiles = stats[1, :N_CLASSES].astype(I32)
    end_tile = stats[2, :N_CLASSES].astype(I32) + tiles
    n_used = end_tile[N_CLASSES - 1]
    blk = jnp.minimum(jnp.arange(n_tiles, dtype=I32), n_used - 1)
    tcls = jnp.sum((end_tile[None, :] <= blk[:, None]).astype(I32), axis=1)
    valid = jnp.arange(n_tiles, dtype=I32) < n_used
    changed = jnp.concatenate([jnp.ones((1,), jnp.bool_), tcls[1:] != tcls[:-1]]) & valid
    flag = valid.astype(I32) + 2 * changed.astype(I32)
    group = tcls // N_PAIRS
    pair = tcls % N_PAIRS
    elo = group * PER_GROUP + jnp.asarray(PAIR_LO, I32)[pair]
    ehi = group * PER_GROUP + jnp.asarray(PAIR_HI, I32)[pair]
    zstart = jnp.where(tiles > 0, (end_tile - 1) * MOE_TILE, -1)

    xs = _scatter(pos, zstart, x1p, x1s, n_tiles)
    ys = _experts(elo, ehi, blk, flag, xs, w_gate, w_up, w_down, ln2, layer, n_tiles)
    return _gather(pos, ys, n_p, n_s)


def kernel(x_prompt, x_sample, state_hgrn, state_conv, w_in, w_out, lower_bounds, hgrn_norm_g,
           conv_w, conv_norm_g, ln1_g, ln1_b, ln2_g, ln2_b, w_router, router_bias,
           w_gate, w_up, w_down):
    batch, seq, _ = x_prompt.shape
    dec_batch, dec_seq, _ = x_sample.shape
    assert seq % PROMPT_TILE == 0 and dec_batch % SAMPLE_SEQS == 0
    assert CONV_K - 1 <= dec_seq <= SAMPLE_ROWS
    assert (batch * seq + dec_batch * SAMPLE_ROWS) % MOE_TILE == 0

    lb = jnp.cumsum(jax.nn.softmax(lower_bounds.astype(F32), axis=0), axis=0)
    lb = lb - lb[0:1]
    params = jnp.stack([jnp.log(lb), jnp.log1p(-lb), 1.0 - lb, hgrn_norm_g, conv_norm_g,
                        conv_w[:, 0], conv_w[:, 1], conv_w[:, 2]], axis=1)
    ln1 = jnp.stack([ln1_g, ln1_b], axis=1)
    ln2 = jnp.stack([ln2_g, ln2_b], axis=1)
    wr_hi = w_router.astype(BF16)
    wr_lo = (w_router - wr_hi.astype(F32)).astype(BF16)
    wr = jnp.concatenate([wr_hi.T, wr_lo.T], axis=0)
    rb = router_bias.astype(F32).reshape(N_EXPERTS, 1)

    xp = x_prompt.reshape(batch * seq, D_MODEL)
    xs = jnp.pad(x_sample, ((0, 0), (0, SAMPLE_ROWS - dec_seq), (0, 0)))
    xs = xs.reshape(dec_batch * SAMPLE_ROWS, D_MODEL)

    s_p, b_p, s_s, b_s = [], [], [], []
    for l in range(DEPTH):
        x1p, route_p, s_l, b_l = _prompt_mixer(xp, w_in, w_out, params, ln1, wr, rb, l, batch, seq)
        s_p.append(s_l)
        b_p.append(b_l)
        x1s, route_s, s_l, b_l = _sample_mixer(xs, state_hgrn, state_conv, w_in, w_out, params, ln1,
                                               wr, rb, l, dec_batch, dec_seq)
        s_s.append(s_l)
        b_s.append(b_l)
        xp, xs = _moe(x1p, route_p, x1s, route_s, w_gate, w_up, w_down, ln2, l)

    y_prompt = xp.reshape(batch, seq, D_MODEL)
    y_sample = xs.reshape(dec_batch, SAMPLE_ROWS, D_MODEL)[:, :dec_seq]
    return (y_prompt, y_sample, jnp.stack(s_p), jnp.stack(b_p), jnp.stack(s_s), jnp.stack(b_s))
```

```python
import functools

import jax
import jax.numpy as jnp
from jax import lax
from jax.experimental import pallas as pl
from jax.experimental.pallas import tpu as pltpu

F32 = jnp.float32
BF16 = jnp.bfloat16
I32 = jnp.int32

D_MODEL = 1024
DEPTH = 2
HEADS = 4
HEAD_DIM = 128
HGRN_W = HEADS * HEAD_DIM
CONV_DIM = 512
CONV_GROUPS = 8
CONV_K = 3
N_IN = 7 * 512
N_EXPERTS = 16
N_GROUPS = 4
PER_GROUP = 4
N_PAIRS = 6
N_CLASSES = N_GROUPS * N_PAIRS
PAIR_LO = (0, 0, 0, 1, 1, 2)
PAIR_HI = (1, 2, 3, 2, 3, 3)
EXPERT_FF = 512
ALPHA = (2 * DEPTH) ** 0.25
LN_EPS = 1e-5
RMS_EPS = 1e-6

LANES = 128
SUBLANES = 8
CHUNK = 64
PROMPT_TILE = 256
SAMPLE_ROWS = 8
SAMPLE_SEQS = 16
MOE_TILE = 256
ROW_W = D_MODEL + LANES
PERM_TILE = 512
VMEM_LIMIT = 56 * 1024 * 1024

OFF_Q, OFF_F, OFF_I, OFF_G, OFF_GB, OFF_GC, OFF_CX = (i * 512 for i in range(7))

P_LOGLB, P_LOG1MLB, P_OMLB, P_HNORM, P_CNORM, P_CW0, P_CW1, P_CW2 = range(8)

NT_DIMS = (((1,), (1,)), ((), ()))
TN_DIMS = (((0,), (0,)), ((), ()))


def _sigmoid(x):
    return 1.0 / (1.0 + jnp.exp(-x))


def _seg_cumsum(x, seg):
    row = lax.broadcasted_iota(I32, x.shape, 0)
    pos = row & (seg - 1)
    sh = 1
    while sh < seg:
        x = x + jnp.where(pos >= sh, pltpu.roll(x, sh, axis=0), 0.0)
        sh *= 2
    return x


def _gate_terms(z, p_ref):
    e = jnp.exp(-jnp.abs(z))
    inv = 1.0 / (1.0 + e)
    logsig = jnp.minimum(z, 0.0) - jnp.log(1.0 + e)
    a = p_ref[0, P_LOGLB:P_LOGLB + 1, :]
    b = p_ref[0, P_LOG1MLB:P_LOG1MLB + 1, :] + logsig
    logf = jnp.maximum(a, b) + jnp.log(1.0 + jnp.exp(-jnp.abs(a - b)))
    k = p_ref[0, P_OMLB:P_OMLB + 1, :] * (jnp.where(z >= 0.0, e, 1.0) * inv)
    return logf, k


def _group_rms(x, n_groups):
    width = x.shape[-1] // n_groups
    x2 = x * x
    outs = []
    for s in range(x.shape[-1] // LANES):
        xs = x[:, s * LANES:(s + 1) * LANES]
        x2s = x2[:, s * LANES:(s + 1) * LANES]
        if width == LANES:
            ms = jnp.sum(x2s, axis=-1, keepdims=True) * (1.0 / width)
            scale = lax.rsqrt(ms + RMS_EPS)
        else:
            lane = lax.broadcasted_iota(I32, xs.shape, 1)
            lo = lane < width
            ms_lo = jnp.sum(jnp.where(lo, x2s, 0.0), axis=-1, keepdims=True) * (1.0 / width)
            ms_hi = jnp.sum(jnp.where(lo, 0.0, x2s), axis=-1, keepdims=True) * (1.0 / width)
            scale = jnp.where(lo, lax.rsqrt(ms_lo + RMS_EPS), lax.rsqrt(ms_hi + RMS_EPS))
        outs.append(xs * scale)
    return jnp.concatenate(outs, axis=-1)


def _layer_norm(r, g, b):
    mu = jnp.mean(r, axis=-1, keepdims=True)
    rc = r - mu
    var = jnp.mean(rc * rc, axis=-1, keepdims=True)
    return rc * lax.rsqrt(var + LN_EPS) * g + b


def _route(x1, wr_ref, rb_ref):
    x_hi = x1.astype(BF16)
    x_lo = (x1 - x_hi.astype(F32)).astype(BF16)
    wr = wr_ref[...]
    r1 = lax.dot_general(wr, x_hi, NT_DIMS, preferred_element_type=F32)
    r2 = lax.dot_general(wr, x_lo, NT_DIMS, preferred_element_type=F32)
    lt = r1[0:N_EXPERTS] + r1[N_EXPERTS:] + r2[0:N_EXPERTS] + r2[N_EXPERTS:] + rb_ref[...]
    lg = [lt[e:e + 1, :] for e in range(N_EXPERTS)]
    mx = lg[0]
    for e in range(1, N_EXPERTS):
        mx = jnp.maximum(mx, lg[e])
    ex = [jnp.exp(l - mx) for l in lg]
    best = None
    gi = None
    for g in range(N_GROUPS):
        a, b, c, d = ex[PER_GROUP * g:PER_GROUP * (g + 1)]
        s = jnp.maximum(jnp.maximum(jnp.maximum(a + b, a + c), jnp.maximum(a + d, b + c)),
                        jnp.maximum(b + d, c + d))
        if g == 0:
            best, gi = s, jnp.zeros(s.shape, I32)
        else:
            upd = s > best
            best = jnp.where(upd, s, best)
            gi = jnp.where(upd, g, gi)
    v = []
    for i in range(PER_GROUP):
        vi = ex[3 * PER_GROUP + i]
        for g in (2, 1, 0):
            vi = jnp.where(gi == g, ex[PER_GROUP * g + i], vi)
        v.append(vi)
    w1, i1 = v[0], jnp.zeros(v[0].shape, I32)
    for i in range(1, PER_GROUP):
        upd = v[i] > w1
        w1 = jnp.where(upd, v[i], w1)
        i1 = jnp.where(upd, i, i1)
    w2, i2 = None, None
    for i in range(PER_GROUP):
        vi = jnp.where(i1 == i, -1.0, v[i])
        if i == 0:
            w2, i2 = vi, jnp.zeros(vi.shape, I32)
        else:
            upd = vi > w2
            w2 = jnp.where(upd, vi, w2)
            i2 = jnp.where(upd, i, i2)
    inv = 1.0 / (w1 + w2)
    first_lo = i1 < i2
    lo = jnp.where(first_lo, i1, i2)
    hi = jnp.where(first_lo, i2, i1)
    g_lo = jnp.where(first_lo, w1, w2) * inv
    g_hi = jnp.where(first_lo, w2, w1) * inv
    pair = jnp.where(lo == 0, 0, jnp.where(lo == 1, 3, 5)) + hi - lo - 1
    return gi * N_PAIRS + pair, g_lo, g_hi


def _post_mix(xt, o, g, yc_in, p_ref, wout_scr, ln_ref, wr_ref, rb_ref, x1_ref, route_ref):
    rows = xt.shape[0]
    o = _group_rms(o, HEADS) * p_ref[0, P_HNORM:P_HNORM + 1, :]
    o = o * (g * _sigmoid(g))
    yc = _group_rms(yc_in, CONV_GROUPS) * p_ref[0, P_CNORM:P_CNORM + 1, :]
    mix = jnp.concatenate([o, yc], axis=-1).astype(BF16)
    h = jnp.dot(mix, wout_scr[...], preferred_element_type=F32)
    x1 = _layer_norm(ALPHA * xt + h, ln_ref[0, 0:1, :], ln_ref[0, 1:2, :])
    cls, g_lo, g_hi = _route(x1, wr_ref, rb_ref)
    x1_ref[:, 0:D_MODEL] = x1
    sub = lax.broadcasted_iota(I32, (LANES, rows), 0)
    gates_t = jnp.where(sub == 0, g_lo, jnp.where(sub == 1, g_hi, 0.0))
    x1_ref[:, D_MODEL:ROW_W] = jnp.transpose(gates_t)
    sub8 = lax.broadcasted_iota(I32, (SUBLANES, rows), 0)
    route_ref[...] = jnp.where(sub8 == 0, cls.astype(F32), 0.0)


def _prompt_mixer_kernel(x_ref, win_ref, wout_ref, p_ref, ln_ref, wr_ref, rb_ref,
                         x1_ref, route_ref, s_ref, buf_ref,
                         win_scr, wout_scr, proj_scr, qs_scr, b_scr, k_scr, o_scr, st_scr,
                         ubuf_scr):
    j = pl.program_id(1)

    @pl.when((pl.program_id(0) == 0) & (j == 0))
    def _():
        win_scr[...] = win_ref[0].astype(BF16)
        wout_scr[...] = wout_ref[0].astype(BF16)

    @pl.when(j == 0)
    def _():
        st_scr[...] = jnp.zeros_like(st_scr)
        ubuf_scr[...] = jnp.zeros_like(ubuf_scr)

    xt = x_ref[...]
    proj_scr[...] = jnp.dot(xt.astype(BF16), win_scr[...], preferred_element_type=F32)

    q = proj_scr[:, OFF_Q:OFF_Q + 512]
    qs_scr[...] = q * _sigmoid(q)
    logf, k = _gate_terms(proj_scr[:, OFF_F:OFF_F + 512], p_ref)
    k_scr[...] = k
    b_scr[...] = _seg_cumsum(logf, CHUNK)

    tri = (lax.broadcasted_iota(I32, (CHUNK, CHUNK), 0)
           >= lax.broadcasted_iota(I32, (CHUNK, CHUNK), 1))

    def chunk_body(c, carry):
        r0 = pl.multiple_of(c * CHUNK, CHUNK)
        rows = pl.ds(r0, CHUNK)
        qs_c = qs_scr[rows, :]
        b_c = b_scr[rows, :]
        k_c = k_scr[rows, :]
        v_c = proj_scr[rows, OFF_I:OFF_I + 512].astype(BF16)
        b_mid = b_c[CHUNK // 2 - 1:CHUNK // 2, :]
        b_last = b_c[CHUNK - 1:CHUNK, :]
        qd = (qs_c * jnp.exp(b_c - b_mid)).astype(BF16)
        kd = (k_c * jnp.exp(b_mid - b_c)).astype(BF16)
        qb = (qs_c * jnp.exp(b_c)).astype(BF16)
        kend = (k_c * jnp.exp(b_last - b_c)).astype(BF16)
        dec = jnp.exp(b_last)
        for h in range(HEADS):
            sl = slice(h * HEAD_DIM, (h + 1) * HEAD_DIM)
            att = lax.dot_general(qd[:, sl], kd[:, sl], NT_DIMS, preferred_element_type=F32)
            att = jnp.where(tri, att, 0.0).astype(BF16)
            st = st_scr[h]
            o_h = jnp.dot(att, v_c[:, sl], preferred_element_type=F32)
            o_h = o_h + lax.dot_general(qb[:, sl], st.astype(BF16), NT_DIMS,
                                        preferred_element_type=F32)
            o_scr[rows, sl] = o_h
            d_st = lax.dot_general(v_c[:, sl], kend[:, sl], TN_DIMS, preferred_element_type=F32)
            st_scr[h] = st * dec[:, sl] + d_st
        return carry

    lax.fori_loop(0, PROMPT_TILE // CHUNK, chunk_body, 0)

    u = proj_scr[:, OFF_GC:OFF_GC + 512] * proj_scr[:, OFF_CX:OFF_CX + 512]
    row = lax.broadcasted_iota(I32, u.shape, 0)
    prev2 = ubuf_scr[SUBLANES - 2:SUBLANES - 1, :]
    prev1 = ubuf_scr[SUBLANES - 1:SUBLANES, :]
    u1 = jnp.where(row == 0, prev1, pltpu.roll(u, 1, axis=0))
    u2 = jnp.where(row == 0, prev2, jnp.where(row == 1, prev1, pltpu.roll(u, 2, axis=0)))
    y = (p_ref[0, P_CW0:P_CW0 + 1, :] * u2 + p_ref[0, P_CW1:P_CW1 + 1, :] * u1
         + p_ref[0, P_CW2:P_CW2 + 1, :] * u)
    ubuf_scr[...] = u[PROMPT_TILE - SUBLANES:, :]
    yc_in = proj_scr[:, OFF_GB:OFF_GB + 512] * y

    @pl.when(j == pl.num_programs(1) - 1)
    def _():
        for h in range(HEADS):
            s_ref[0, h] = jnp.transpose(st_scr[h])
        buf_ref[0] = u[PROMPT_TILE - (CONV_K - 1):, :]

    _post_mix(xt, o_scr[...], proj_scr[:, OFF_G:OFF_G + 512], yc_in, p_ref, wout_scr, ln_ref,
              wr_ref, rb_ref, x1_ref, route_ref)


def _sample_mixer_kernel(n_valid, x_ref, s0_ref, cbuf_ref, win_ref, wout_ref, p_ref, ln_ref,
                         wr_ref, rb_ref,
                         x1_ref, route_ref, s_ref, buf_ref,
                         win_scr, wout_scr, proj_scr, qs_scr, b_scr, k_scr, o_scr, y_scr):
    tile = SAMPLE_SEQS * SAMPLE_ROWS

    @pl.when(pl.program_id(0) == 0)
    def _():
        win_scr[...] = win_ref[0].astype(BF16)
        wout_scr[...] = wout_ref[0].astype(BF16)

    xt = x_ref[...]
    proj_scr[...] = jnp.dot(xt.astype(BF16), win_scr[...], preferred_element_type=F32)

    q = proj_scr[:, OFF_Q:OFF_Q + 512]
    qs_scr[...] = q * _sigmoid(q)
    logf, k = _gate_terms(proj_scr[:, OFF_F:OFF_F + 512], p_ref)
    valid = (lax.broadcasted_iota(I32, (tile, 512), 0) & (SAMPLE_ROWS - 1)) < n_valid
    k_scr[...] = jnp.where(valid, k, 0.0)
    b_scr[...] = _seg_cumsum(jnp.where(valid, logf, 0.0), SAMPLE_ROWS)

    trow = lax.broadcasted_iota(I32, (SAMPLE_ROWS, 1), 0)
    urow = lax.broadcasted_iota(I32, (SAMPLE_ROWS, 512), 0)

    def seq_body(s, carry):
        r0 = pl.multiple_of(s * SAMPLE_ROWS, SAMPLE_ROWS)
        rows = pl.ds(r0, SAMPLE_ROWS)
        qs_c = qs_scr[rows, :]
        b_c = b_scr[rows, :]
        k_c = k_scr[rows, :]
        v_c = proj_scr[rows, OFF_I:OFF_I + 512]
        b_last = b_c[n_valid - 1:n_valid, :]
        qb = (qs_c * jnp.exp(b_c)).astype(BF16)
        kend = (k_c * jnp.exp(b_last - b_c)).astype(BF16)
        dec = jnp.exp(b_last)
        for h in range(HEADS):
            sl = slice(h * HEAD_DIM, (h + 1) * HEAD_DIM)
            st = jnp.transpose(s0_ref[0, s, h])
            o_h = lax.dot_general(qb[:, sl], st.astype(BF16), NT_DIMS,
                                  preferred_element_type=F32)
            for t in range(n_valid):
                dlt = jnp.minimum(b_c[:, sl] - b_c[t:t + 1, sl], 0.0)
                a_col = jnp.sum(qs_c[:, sl] * k_c[t:t + 1, sl] * jnp.exp(dlt),
                                axis=-1, keepdims=True)
                a_col = jnp.where(trow >= t, a_col, 0.0)
                o_h = o_h + a_col * v_c[t:t + 1, sl]
            o_scr[rows, sl] = o_h
            d_st = lax.dot_general(v_c[:, sl].astype(BF16), kend[:, sl], TN_DIMS,
                                   preferred_element_type=F32)
            s_ref[s, h] = jnp.transpose(st * dec[:, sl] + d_st)
        u = proj_scr[rows, OFF_GC:OFF_GC + 512] * proj_scr[rows, OFF_CX:OFF_CX + 512]
        prev2 = cbuf_ref[0, s, 0:1, :]
        prev1 = cbuf_ref[0, s, 1:2, :]
        u1 = jnp.where(urow == 0, prev1, pltpu.roll(u, 1, axis=0))
        u2 = jnp.where(urow == 0, prev2, jnp.where(urow == 1, prev1, pltpu.roll(u, 2, axis=0)))
        y_scr[rows, :] = (p_ref[0, P_CW0:P_CW0 + 1, :] * u2 + p_ref[0, P_CW1:P_CW1 + 1, :] * u1
                          + p_ref[0, P_CW2:P_CW2 + 1, :] * u)
        buf_ref[s] = u[n_valid - (CONV_K - 1):n_valid, :]
        return carry

    lax.fori_loop(0, SAMPLE_SEQS, seq_body, 0)

    yc_in = proj_scr[:, OFF_GB:OFF_GB + 512] * y_scr[...]
    _post_mix(xt, o_scr[...], proj_scr[:, OFF_G:OFF_G + 512], yc_in, p_ref, wout_scr, ln_ref,
              wr_ref, rb_ref, x1_ref, route_ref)


def _plan_kernel(cls_ref, pos_ref, stats_ref):
    n_rows = cls_ref.shape[0]
    cls = cls_ref[...]
    upper = (lax.broadcasted_iota(I32, (LANES, LANES), 0)
             < lax.broadcasted_iota(I32, (LANES, LANES), 1)).astype(BF16)
    lower = (lax.broadcasted_iota(I32, (n_rows, n_rows), 1)
             < lax.broadcasted_iota(I32, (n_rows, n_rows), 0)).astype(BF16)
    lane = lax.broadcasted_iota(I32, (n_rows, LANES), 1)
    row_tot = jnp.zeros((n_rows, LANES), F32)
    for c in range(N_CLASSES):
        oh = jnp.where(cls == c, 1.0, 0.0)
        row_tot = jnp.where(lane == c, jnp.sum(oh, axis=1, keepdims=True), row_tot)
    before = jnp.dot(lower, row_tot.astype(BF16), preferred_element_type=F32)
    cnt = jnp.sum(row_tot, axis=0, keepdims=True)
    tiles = jnp.floor((cnt + (MOE_TILE - 1)) * (1.0 / MOE_TILE))
    first_tile = jnp.dot(jnp.broadcast_to(tiles, (SUBLANES, LANES)).astype(BF16), upper,
                         preferred_element_type=F32)[0:1]
    base = before + first_tile * MOE_TILE
    pos = jnp.zeros((n_rows, LANES), F32)
    for c in range(N_CLASSES):
        oh = jnp.where(cls == c, 1.0, 0.0)
        local = jnp.dot(oh.astype(BF16), upper, preferred_element_type=F32)
        pos = pos + oh * (base[:, c:c + 1] + local)
    pos_ref[...] = pos.astype(I32)
    sub = lax.broadcasted_iota(I32, (SUBLANES, LANES), 0)
    stats_ref[...] = jnp.where(sub == 0, cnt, jnp.where(sub == 1, tiles,
                                                        jnp.where(sub == 2, first_tile, 0.0)))


def _row_copy(src_ref, src_row, dst_ref, dst_row, sem):
    return pltpu.make_async_copy(src_ref.at[pl.ds(src_row, 1), :],
                                 dst_ref.at[pl.ds(dst_row, 1), :], sem)


def _tile_rows(row_copy, chunk_copy):
    def body(r, carry):
        row_copy(r).start()
        return carry
    lax.fori_loop(0, PERM_TILE, body, 0, unroll=8)
    chunk_copy.wait()


def _scatter_kernel(n_p_tiles, pos_ref, zst_ref, xp_ref, xs_ref, out_ref, zbuf, sem_z, sem_r):
    i = pl.program_id(0)

    def zero_copy(c):
        start = pl.multiple_of(zst_ref[c], MOE_TILE)
        return pltpu.make_async_copy(zbuf, out_ref.at[pl.ds(start, MOE_TILE), :], sem_z)

    @pl.when(i == 0)
    def _():
        zbuf[...] = jnp.zeros_like(zbuf)

        def z_start(c, carry):
            @pl.when(zst_ref[c] >= 0)
            def _():
                zero_copy(c).start()
            return carry

        def z_wait(c, carry):
            @pl.when(zst_ref[c] >= 0)
            def _():
                zero_copy(c).wait()
            return carry

        lax.fori_loop(0, N_CLASSES, z_start, 0)
        lax.fori_loop(0, N_CLASSES, z_wait, 0)

    base = i * PERM_TILE

    def from_tile(src_ref):
        _tile_rows(lambda r: _row_copy(src_ref, r, out_ref, pos_ref[base + r], sem_r),
                   pltpu.make_async_copy(src_ref, out_ref.at[pl.ds(0, PERM_TILE), :], sem_r))

    @pl.when(i < n_p_tiles)
    def _():
        from_tile(xp_ref)

    @pl.when(i >= n_p_tiles)
    def _():
        from_tile(xs_ref)


def _gather_kernel(n_p_tiles, pos_ref, ys_ref, outp_ref, outs_ref, sem_r):
    i = pl.program_id(0)
    base = i * PERM_TILE

    def to_tile(dst_ref):
        _tile_rows(lambda r: _row_copy(ys_ref, pos_ref[base + r], dst_ref, r, sem_r),
                   pltpu.make_async_copy(ys_ref.at[pl.ds(0, PERM_TILE), :], dst_ref, sem_r))

    @pl.when(i < n_p_tiles)
    def _():
        to_tile(outp_ref)

    @pl.when(i >= n_p_tiles)
    def _():
        to_tile(outs_ref)


def _expert_kernel(elo_ref, ehi_ref, blk_ref, flag_ref,
                   xs_ref, wg_lo, wu_lo, wd_lo, wg_hi, wu_hi, wd_hi, ln_ref, out_ref,
                   w1_scr, wd_scr):
    i = pl.program_id(0)
    flag = flag_ref[i]

    @pl.when((flag & 2) != 0)
    def _():
        w1_scr[:, 0 * EXPERT_FF:1 * EXPERT_FF] = wg_lo[0, 0].astype(BF16)
        w1_scr[:, 1 * EXPERT_FF:2 * EXPERT_FF] = wu_lo[0, 0].astype(BF16)
        w1_scr[:, 2 * EXPERT_FF:3 * EXPERT_FF] = wg_hi[0, 0].astype(BF16)
        w1_scr[:, 3 * EXPERT_FF:4 * EXPERT_FF] = wu_hi[0, 0].astype(BF16)
        wd_scr[0:EXPERT_FF, :] = wd_lo[0, 0].astype(BF16)
        wd_scr[EXPERT_FF:, :] = wd_hi[0, 0].astype(BF16)

    @pl.when((flag & 1) != 0)
    def _():
        x = xs_ref[:, 0:D_MODEL]
        g_lo = xs_ref[:, D_MODEL:D_MODEL + 1]
        g_hi = xs_ref[:, D_MODEL + 1:D_MODEL + 2]
        hc = jnp.dot(x.astype(BF16), w1_scr[...], preferred_element_type=F32)
        hg_lo, hu_lo = hc[:, 0:EXPERT_FF], hc[:, EXPERT_FF:2 * EXPERT_FF]
        hg_hi, hu_hi = hc[:, 2 * EXPERT_FF:3 * EXPERT_FF], hc[:, 3 * EXPERT_FF:]
        h_lo = hg_lo * _sigmoid(hg_lo) * hu_lo * g_lo
        h_hi = hg_hi * _sigmoid(hg_hi) * hu_hi * g_hi
        h = jnp.concatenate([h_lo, h_hi], axis=-1).astype(BF16)
        y = jnp.dot(h, wd_scr[...], preferred_element_type=F32)
        out_ref[...] = _layer_norm(ALPHA * x + y, ln_ref[0, 0:1, :], ln_ref[0, 1:2, :])


def _layer_spec(shape, layer, n_grid, single_buffer=False):
    idx = (layer,) + (0,) * len(shape)
    if n_grid == 1:
        index_map = lambda i: idx
    else:
        index_map = lambda i, j: idx
    if single_buffer:
        return pl.BlockSpec((1,) + shape, index_map, pipeline_mode=pl.Buffered(1))
    return pl.BlockSpec((1,) + shape, index_map)


def _prompt_mixer(x, w_in, w_out, params, ln, wr, rb, layer, batch, seq):
    n_t = seq // PROMPT_TILE
    n = batch * seq
    row_in = pl.BlockSpec((PROMPT_TILE, D_MODEL), lambda b, j: (b * n_t + j, 0))
    row_out = pl.BlockSpec((PROMPT_TILE, ROW_W), lambda b, j: (b * n_t + j, 0))
    route_out = pl.BlockSpec((SUBLANES, PROMPT_TILE), lambda b, j: (0, b * n_t + j))
    full2 = lambda shape: pl.BlockSpec(shape, lambda b, j: (0, 0))
    return pl.pallas_call(
        _prompt_mixer_kernel,
        grid=(batch, n_t),
        in_specs=[row_in,
                  _layer_spec((D_MODEL, N_IN), layer, 2, True),
                  _layer_spec((D_MODEL, D_MODEL), layer, 2, True),
                  _layer_spec((8, 512), layer, 2), _layer_spec((2, D_MODEL), layer, 2),
                  full2((2 * N_EXPERTS, D_MODEL)), full2((N_EXPERTS, 1))],
        out_specs=[row_out, route_out,
                   pl.BlockSpec((1, HEADS, HEAD_DIM, HEAD_DIM), lambda b, j: (b, 0, 0, 0)),
                   pl.BlockSpec((1, CONV_K - 1, CONV_DIM), lambda b, j: (b, 0, 0))],
        out_shape=[jax.ShapeDtypeStruct((n, ROW_W), F32),
                   jax.ShapeDtypeStruct((SUBLANES, n), F32),
                   jax.ShapeDtypeStruct((batch, HEADS, HEAD_DIM, HEAD_DIM), F32),
                   jax.ShapeDtypeStruct((batch, CONV_K - 1, CONV_DIM), F32)],
        scratch_shapes=[pltpu.VMEM((D_MODEL, N_IN), BF16), pltpu.VMEM((D_MODEL, D_MODEL), BF16),
                        pltpu.VMEM((PROMPT_TILE, N_IN), F32)]
        + [pltpu.VMEM((PROMPT_TILE, 512), F32)] * 4
        + [pltpu.VMEM((HEADS, HEAD_DIM, HEAD_DIM), F32), pltpu.VMEM((SUBLANES, CONV_DIM), F32)],
        compiler_params=pltpu.CompilerParams(
            dimension_semantics=("arbitrary", "arbitrary"), vmem_limit_bytes=VMEM_LIMIT),
        name="prompt_mixer",
    )(x, w_in, w_out, params, ln, wr, rb)


def _sample_mixer(x, s0, cbuf, w_in, w_out, params, ln, wr, rb, layer, batch, n_valid):
    tile = SAMPLE_SEQS * SAMPLE_ROWS
    n = batch * SAMPLE_ROWS
    row_in = pl.BlockSpec((tile, D_MODEL), lambda i: (i, 0))
    row_out = pl.BlockSpec((tile, ROW_W), lambda i: (i, 0))
    route_out = pl.BlockSpec((SUBLANES, tile), lambda i: (0, i))
    st_spec = pl.BlockSpec((1, SAMPLE_SEQS, HEADS, HEAD_DIM, HEAD_DIM),
                           lambda i: (layer, i, 0, 0, 0))
    cb_spec = pl.BlockSpec((1, SAMPLE_SEQS, CONV_K - 1, CONV_DIM), lambda i: (layer, i, 0, 0))
    st_out = pl.BlockSpec((SAMPLE_SEQS, HEADS, HEAD_DIM, HEAD_DIM), lambda i: (i, 0, 0, 0))
    cb_out = pl.BlockSpec((SAMPLE_SEQS, CONV_K - 1, CONV_DIM), lambda i: (i, 0, 0))
    full1 = lambda shape: pl.BlockSpec(shape, lambda i: (0, 0))
    return pl.pallas_call(
        functools.partial(_sample_mixer_kernel, n_valid),
        grid=(batch // SAMPLE_SEQS,),
        in_specs=[row_in, st_spec, cb_spec,
                  _layer_spec((D_MODEL, N_IN), layer, 1, True),
                  _layer_spec((D_MODEL, D_MODEL), layer, 1, True),
                  _layer_spec((8, 512), layer, 1), _layer_spec((2, D_MODEL), layer, 1),
                  full1((2 * N_EXPERTS, D_MODEL)), full1((N_EXPERTS, 1))],
        out_specs=[row_out, route_out, st_out, cb_out],
        out_shape=[jax.ShapeDtypeStruct((n, ROW_W), F32),
                   jax.ShapeDtypeStruct((SUBLANES, n), F32),
                   jax.ShapeDtypeStruct((batch, HEADS, HEAD_DIM, HEAD_DIM), F32),
                   jax.ShapeDtypeStruct((batch, CONV_K - 1, CONV_DIM), F32)],
        scratch_shapes=[pltpu.VMEM((D_MODEL, N_IN), BF16), pltpu.VMEM((D_MODEL, D_MODEL), BF16),
                        pltpu.VMEM((tile, N_IN), F32)] + [pltpu.VMEM((tile, 512), F32)] * 5,
        compiler_params=pltpu.CompilerParams(
            dimension_semantics=("arbitrary",), vmem_limit_bytes=VMEM_LIMIT),
        name="sample_mixer",
    )(x, s0, cbuf, w_in, w_out, params, ln, wr, rb)


def _plan(cls2d):
    n_rows = cls2d.shape[0]
    return pl.pallas_call(
        _plan_kernel,
        out_shape=[jax.ShapeDtypeStruct((n_rows, LANES), I32),
                   jax.ShapeDtypeStruct((SUBLANES, LANES), F32)],
        name="moe_plan",
    )(cls2d)


def _perm_specs(n_p_tiles, width):
    p_spec = pl.BlockSpec((PERM_TILE, width), lambda i, *_: (jnp.minimum(i, n_p_tiles - 1), 0))
    s_spec = pl.BlockSpec((PERM_TILE, width), lambda i, *_: (jnp.maximum(i - n_p_tiles, 0), 0))
    return p_spec, s_spec


def _scatter(pos, zstart, x1p, x1s, n_tiles):
    n_p, n_s = x1p.shape[0], x1s.shape[0]
    n_p_tiles = n_p // PERM_TILE
    p_spec, s_spec = _perm_specs(n_p_tiles, ROW_W)
    return pl.pallas_call(
        functools.partial(_scatter_kernel, n_p_tiles),
        grid_spec=pltpu.PrefetchScalarGridSpec(
            num_scalar_prefetch=2, grid=((n_p + n_s) // PERM_TILE,),
            in_specs=[p_spec, s_spec], out_specs=pl.BlockSpec(memory_space=pl.ANY),
            scratch_shapes=[pltpu.VMEM((MOE_TILE, ROW_W), F32),
                            pltpu.SemaphoreType.DMA(()), pltpu.SemaphoreType.DMA(())]),
        out_shape=jax.ShapeDtypeStruct((n_tiles * MOE_TILE, ROW_W), F32),
        compiler_params=pltpu.CompilerParams(dimension_semantics=("arbitrary",)),
        name="moe_scatter",
    )(pos, zstart, x1p, x1s)


def _gather(pos, ys, n_p, n_s):
    n_p_tiles = n_p // PERM_TILE
    p_spec, s_spec = _perm_specs(n_p_tiles, D_MODEL)
    return pl.pallas_call(
        functools.partial(_gather_kernel, n_p_tiles),
        grid_spec=pltpu.PrefetchScalarGridSpec(
            num_scalar_prefetch=1, grid=((n_p + n_s) // PERM_TILE,),
            in_specs=[pl.BlockSpec(memory_space=pl.ANY)], out_specs=[p_spec, s_spec],
            scratch_shapes=[pltpu.SemaphoreType.DMA(())]),
        out_shape=[jax.ShapeDtypeStruct((n_p, D_MODEL), F32),
                   jax.ShapeDtypeStruct((n_s, D_MODEL), F32)],
        compiler_params=pltpu.CompilerParams(dimension_semantics=("arbitrary",)),
        name="moe_gather",
    )(pos, ys)


def _experts(elo, ehi, blk, flag, xs, w_gate, w_up, w_down, ln, layer, n_tiles):
    def w_spec(shape, table_idx):
        def index_map(i, elo_r, ehi_r, blk_r, flag_r):
            return (layer, (elo_r, ehi_r)[table_idx][i], 0, 0)
        return pl.BlockSpec((1, 1) + shape, index_map)

    up = (D_MODEL, EXPERT_FF)
    down = (EXPERT_FF, D_MODEL)
    return pl.pallas_call(
        _expert_kernel,
        grid_spec=pltpu.PrefetchScalarGridSpec(
            num_scalar_prefetch=4, grid=(n_tiles,),
            in_specs=[pl.BlockSpec((MOE_TILE, ROW_W), lambda i, e0, e1, b, f: (b[i], 0)),
                      w_spec(up, 0), w_spec(up, 0), w_spec(down, 0),
                      w_spec(up, 1), w_spec(up, 1), w_spec(down, 1),
                      pl.BlockSpec((1, 2, D_MODEL), lambda i, e0, e1, b, f: (layer, 0, 0))],
            out_specs=pl.BlockSpec((MOE_TILE, D_MODEL), lambda i, e0, e1, b, f: (b[i], 0)),
            scratch_shapes=[pltpu.VMEM((D_MODEL, 4 * EXPERT_FF), BF16),
                            pltpu.VMEM((2 * EXPERT_FF, D_MODEL), BF16)]),
        out_shape=jax.ShapeDtypeStruct((n_tiles * MOE_TILE, D_MODEL), F32),
        compiler_params=pltpu.CompilerParams(
            dimension_semantics=("arbitrary",), vmem_limit_bytes=VMEM_LIMIT),
        name="moe_experts",
    )(elo, ehi, blk, flag, xs, w_gate, w_up, w_down, w_gate, w_up, w_down, ln)


def _moe(x1p, route_p, x1s, route_s, w_gate, w_up, w_down, ln2, layer):
    n_p, n_s = x1p.shape[0], x1s.shape[0]
    n = n_p + n_s
    n_tiles = n // MOE_TILE + N_CLASSES
    cls2d = jnp.concatenate([route_p[0], route_s[0]]).reshape(n // LANES, LANES)
    pos2d, stats = _plan(cls2d)
    pos = pos2d.reshape(n)

    tiles = stats[1, :N_CLASSES].astype(I32)
    end_tile = stats[2, :N_CLASSES].astype(I32) + tiles
    n_used = end_tile[N_CLASSES - 1]
    blk = jnp.minimum(jnp.arange(n_tiles, dtype=I32), n_used - 1)
    tcls = jnp.sum((end_tile[None, :] <= blk[:, None]).astype(I32), axis=1)
    valid = jnp.arange(n_tiles, dtype=I32) < n_used
    changed = jnp.concatenate([jnp.ones((1,), jnp.bool_), tcls[1:] != tcls[:-1]]) & valid
    flag = valid.astype(I32) + 2 * changed.astype(I32)
    group = tcls // N_PAIRS
    pair = tcls % N_PAIRS
    elo = group * PER_GROUP + jnp.asarray(PAIR_LO, I32)[pair]
    ehi = group * PER_GROUP + jnp.asarray(PAIR_HI, I32)[pair]
    zstart = jnp.where(tiles > 0, (end_tile - 1) * MOE_TILE, -1)

    xs = _scatter(pos, zstart, x1p, x1s, n_tiles)
    ys = _experts(elo, ehi, blk, flag, xs, w_gate, w_up, w_down, ln2, layer, n_tiles)
    return _gather(pos, ys, n_p, n_s)


def kernel(x_prompt, x_sample, state_hgrn, state_conv, w_in, w_out, lower_bounds, hgrn_norm_g,
           conv_w, conv_norm_g, ln1_g, ln1_b, ln2_g, ln2_b, w_router, router_bias,
           w_gate, w_up, w_down):
    batch, seq, _ = x_prompt.shape
    dec_batch, dec_seq, _ = x_sample.shape
    assert seq % PROMPT_TILE == 0 and dec_batch % SAMPLE_SEQS == 0
    assert CONV_K - 1 <= dec_seq <= SAMPLE_ROWS
    assert (batch * seq + dec_batch * SAMPLE_ROWS) % MOE_TILE == 0

    lb = jnp.cumsum(jax.nn.softmax(lower_bounds.astype(F32), axis=0), axis=0)
    lb = lb - lb[0:1]
    params = jnp.stack([jnp.log(lb), jnp.log1p(-lb), 1.0 - lb, hgrn_norm_g, conv_norm_g,
                        conv_w[:, 0], conv_w[:, 1], conv_w[:, 2]], axis=1)
    ln1 = jnp.stack([ln1_g, ln1_b], axis=1)
    ln2 = jnp.stack([ln2_g, ln2_b], axis=1)
    wr_hi = w_router.astype(BF16)
    wr_lo = (w_router - wr_hi.astype(F32)).astype(BF16)
    wr = jnp.concatenate([wr_hi.T, wr_lo.T], axis=0)
    rb = router_bias.astype(F32).reshape(N_EXPERTS, 1)

    xp = x_prompt.reshape(batch * seq, D_MODEL)
    xs = jnp.pad(x_sample, ((0, 0), (0, SAMPLE_ROWS - dec_seq), (0, 0)))
    xs = xs.reshape(dec_batch * SAMPLE_ROWS, D_MODEL)

    s_p, b_p, s_s, b_s = [], [], [], []
    for l in range(DEPTH):
        x1p, route_p, s_l, b_l = _prompt_mixer(xp, w_in, w_out, params, ln1, wr, rb, l, batch, seq)
        s_p.append(s_l)
        b_p.append(b_l)
        x1s, route_s, s_l, b_l = _sample_mixer(xs, state_hgrn, state_conv, w_in, w_out, params, ln1,
                                               wr, rb, l, dec_batch, dec_seq)
        s_s.append(s_l)
        b_s.append(b_l)
        xp, xs = _moe(x1p, route_p, x1s, route_s, w_gate, w_up, w_down, ln2, l)

    y_prompt = xp.reshape(batch, seq, D_MODEL)
    y_sample = xs.reshape(dec_batch, SAMPLE_ROWS, D_MODEL)[:, :dec_seq]
    return (y_prompt, y_sample, jnp.stack(s_p), jnp.stack(b_p), jnp.stack(s_s), jnp.stack(b_s))
```

```python
import functools

import jax
import jax.numpy as jnp
from jax import lax
from jax.experimental import pallas as pl
from jax.experimental.pallas import tpu as pltpu

F32 = jnp.float32
BF16 = jnp.bfloat16
I32 = jnp.int32

D_MODEL = 1024
DEPTH = 2
HEADS = 4
HEAD_DIM = 128
HGRN_W = HEADS * HEAD_DIM
CONV_DIM = 512
CONV_GROUPS = 8
CONV_K = 3
N_IN = 7 * 512
N_EXPERTS = 16
N_GROUPS = 4
PER_GROUP = 4
N_PAIRS = 6
N_CLASSES = N_GROUPS * N_PAIRS
PAIR_LO = (0, 0, 0, 1, 1, 2)
PAIR_HI = (1, 2, 3, 2, 3, 3)
EXPERT_FF = 512
ALPHA = (2 * DEPTH) ** 0.25
LN_EPS = 1e-5
RMS_EPS = 1e-6

LANES = 128
SUBLANES = 8
CHUNK = 64
PROMPT_TILE = 256
SAMPLE_ROWS = 8
SAMPLE_SEQS = 16
MOE_TILE = 256
ROW_W = D_MODEL + LANES
PERM_TILE = 512
VMEM_LIMIT = 56 * 1024 * 1024

OFF_Q, OFF_F, OFF_I, OFF_G, OFF_GB, OFF_GC, OFF_CX = (i * 512 for i in range(7))

P_LOGLB, P_LOG1MLB, P_OMLB, P_HNORM, P_CNORM, P_CW0, P_CW1, P_CW2 = range(8)

NT_DIMS = (((1,), (1,)), ((), ()))
TN_DIMS = (((0,), (0,)), ((), ()))


def _sigmoid(x):
    return 1.0 / (1.0 + jnp.exp(-x))


def _seg_cumsum(x, seg):
    row = lax.broadcasted_iota(I32, x.shape, 0)
    pos = row & (seg - 1)
    sh = 1
    while sh < seg:
        x = x + jnp.where(pos >= sh, pltpu.roll(x, sh, axis=0), 0.0)
        sh *= 2
    return x


def _gate_terms(z, p_ref):
    e = jnp.exp(-jnp.abs(z))
    inv = 1.0 / (1.0 + e)
    logsig = jnp.minimum(z, 0.0) - jnp.log(1.0 + e)
    a = p_ref[0, P_LOGLB:P_LOGLB + 1, :]
    b = p_ref[0, P_LOG1MLB:P_LOG1MLB + 1, :] + logsig
    logf = jnp.maximum(a, b) + jnp.log(1.0 + jnp.exp(-jnp.abs(a - b)))
    k = p_ref[0, P_OMLB:P_OMLB + 1, :] * (jnp.where(z >= 0.0, e, 1.0) * inv)
    return logf, k


def _group_rms(x, n_groups):
    width = x.shape[-1] // n_groups
    x2 = x * x
    outs = []
    for s in range(x.shape[-1] // LANES):
        xs = x[:, s * LANES:(s + 1) * LANES]
        x2s = x2[:, s * LANES:(s + 1) * LANES]
        if width == LANES:
            ms = jnp.sum(x2s, axis=-1, keepdims=True) * (1.0 / width)
            scale = lax.rsqrt(ms + RMS_EPS)
        else:
            lane = lax.broadcasted_iota(I32, xs.shape, 1)
            lo = lane < width
            ms_lo = jnp.sum(jnp.where(lo, x2s, 0.0), axis=-1, keepdims=True) * (1.0 / width)
            ms_hi = jnp.sum(jnp.where(lo, 0.0, x2s), axis=-1, keepdims=True) * (1.0 / width)
            scale = jnp.where(lo, lax.rsqrt(ms_lo + RMS_EPS), lax.rsqrt(ms_hi + RMS_EPS))
        outs.append(xs * scale)
    return jnp.concatenate(outs, axis=-1)


def _layer_norm(r, g, b):
    mu = jnp.mean(r, axis=-1, keepdims=True)
    rc = r - mu
    var = jnp.mean(rc * rc, axis=-1, keepdims=True)
    return rc * lax.rsqrt(var + LN_EPS) * g + b


def _route(x1, wr_ref, rb_ref):
    x_hi = x1.astype(BF16)
    x_lo = (x1 - x_hi.astype(F32)).astype(BF16)
    wr = wr_ref[...]
    r1 = lax.dot_general(wr, x_hi, NT_DIMS, preferred_element_type=F32)
    r2 = lax.dot_general(wr, x_lo, NT_DIMS, preferred_element_type=F32)
    lt = r1[0:N_EXPERTS] + r1[N_EXPERTS:] + r2[0:N_EXPERTS] + r2[N_EXPERTS:] + rb_ref[...]
    lg = [lt[e:e + 1, :] for e in range(N_EXPERTS)]
    mx = lg[0]
    for e in range(1, N_EXPERTS):
        mx = jnp.maximum(mx, lg[e])
    ex = [jnp.exp(l - mx) for l in lg]
    best = None
    gi = None
    for g in range(N_GROUPS):
        a, b, c, d = ex[PER_GROUP * g:PER_GROUP * (g + 1)]
        s = jnp.maximum(jnp.maximum(jnp.maximum(a + b, a + c), jnp.maximum(a + d, b + c)),
                        jnp.maximum(b + d, c + d))
        if g == 0:
            best, gi = s, jnp.zeros(s.shape, I32)
        else:
            upd = s > best
            best = jnp.where(upd, s, best)
            gi = jnp.where(upd, g, gi)
    v = []
    for i in range(PER_GROUP):
        vi = ex[3 * PER_GROUP + i]
        for g in (2, 1, 0):
            vi = jnp.where(gi == g, ex[PER_GROUP * g + i], vi)
        v.append(vi)
    w1, i1 = v[0], jnp.zeros(v[0].shape, I32)
    for i in range(1, PER_GROUP):
        upd = v[i] > w1
        w1 = jnp.where(upd, v[i], w1)
        i1 = jnp.where(upd, i, i1)
    w2, i2 = None, None
    for i in range(PER_GROUP):
        vi = jnp.where(i1 == i, -1.0, v[i])
        if i == 0:
            w2, i2 = vi, jnp.zeros(vi.shape, I32)
        else:
            upd = vi > w2
            w2 = jnp.where(upd, vi, w2)
            i2 = jnp.where(upd, i, i2)
    inv = 1.0 / (w1 + w2)
    first_lo = i1 < i2
    lo = jnp.where(first_lo, i1, i2)
    hi = jnp.where(first_lo, i2, i1)
    g_lo = jnp.where(first_lo, w1, w2) * inv
    g_hi = jnp.where(first_lo, w2, w1) * inv
    pair = jnp.where(lo == 0, 0, jnp.where(lo == 1, 3, 5)) + hi - lo - 1
    return gi * N_PAIRS + pair, g_lo, g_hi


def _post_mix(xt, o, g, yc_in, p_ref, wout_scr, ln_ref, wr_ref, rb_ref, x1_ref, route_ref):
    rows = xt.shape[0]
    o = _group_rms(o, HEADS) * p_ref[0, P_HNORM:P_HNORM + 1, :]
    o = o * (g * _sigmoid(g))
    yc = _group_rms(yc_in, CONV_GROUPS) * p_ref[0, P_CNORM:P_CNORM + 1, :]
    mix = jnp.concatenate([o, yc], axis=-1).astype(BF16)
    h = jnp.dot(mix, wout_scr[...], preferred_element_type=F32)
    x1 = _layer_norm(ALPHA * xt + h, ln_ref[0, 0:1, :], ln_ref[0, 1:2, :])
    cls, g_lo, g_hi = _route(x1, wr_ref, rb_ref)
    x1_ref[:, 0:D_MODEL] = x1
    sub = lax.broadcasted_iota(I32, (LANES, rows), 0)
    gates_t = jnp.where(sub == 0, g_lo, jnp.where(sub == 1, g_hi, 0.0))
    x1_ref[:, D_MODEL:ROW_W] = jnp.transpose(gates_t)
    sub8 = lax.broadcasted_iota(I32, (SUBLANES, rows), 0)
    route_ref[...] = jnp.where(sub8 == 0, cls.astype(F32), 0.0)


def _prompt_mixer_kernel(x_ref, win_ref, wout_ref, p_ref, ln_ref, wr_ref, rb_ref,
                         x1_ref, route_ref, s_ref, buf_ref,
                         win_scr, wout_scr, proj_scr, qs_scr, b_scr, k_scr, o_scr, st_scr,
                         ubuf_scr):
    j = pl.program_id(1)

    @pl.when((pl.program_id(0) == 0) & (j == 0))
    def _():
        win_scr[...] = win_ref[0].astype(BF16)
        wout_scr[...] = wout_ref[0].astype(BF16)

    @pl.when(j == 0)
    def _():
        st_scr[...] = jnp.zeros_like(st_scr)
        ubuf_scr[...] = jnp.zeros_like(ubuf_scr)

    xt = x_ref[...]
    proj_scr[...] = jnp.dot(xt.astype(BF16), win_scr[...], preferred_element_type=F32)

    q = proj_scr[:, OFF_Q:OFF_Q + 512]
    qs_scr[...] = q * _sigmoid(q)
    logf, k = _gate_terms(proj_scr[:, OFF_F:OFF_F + 512], p_ref)
    k_scr[...] = k
    b_scr[...] = _seg_cumsum(logf, CHUNK)

    tri = (lax.broadcasted_iota(I32, (CHUNK, CHUNK), 0)
           >= lax.broadcasted_iota(I32, (CHUNK, CHUNK), 1))

    states = [st_scr[h] for h in range(HEADS)]
    for c in range(PROMPT_TILE // CHUNK):
        rows = slice(c * CHUNK, (c + 1) * CHUNK)
        qs_c = qs_scr[rows, :]
        b_c = b_scr[rows, :]
        k_c = k_scr[rows, :]
        v_c = proj_scr[rows, OFF_I:OFF_I + 512].astype(BF16)
        b_mid = b_c[CHUNK // 2 - 1:CHUNK // 2, :]
        b_last = b_c[CHUNK - 1:CHUNK, :]
        qd = (qs_c * jnp.exp(b_c - b_mid)).astype(BF16)
        kd = (k_c * jnp.exp(b_mid - b_c)).astype(BF16)
        qb = (qs_c * jnp.exp(b_c)).astype(BF16)
        kend = (k_c * jnp.exp(b_last - b_c)).astype(BF16)
        dec = jnp.exp(b_last)
        for h in range(HEADS):
            sl = slice(h * HEAD_DIM, (h + 1) * HEAD_DIM)
            att = lax.dot_general(qd[:, sl], kd[:, sl], NT_DIMS, preferred_element_type=F32)
            att = jnp.where(tri, att, 0.0).astype(BF16)
            st = states[h]
            o_h = jnp.dot(att, v_c[:, sl], preferred_element_type=F32)
            o_h = o_h + lax.dot_general(qb[:, sl], st.astype(BF16), NT_DIMS,
                                        preferred_element_type=F32)
            o_scr[rows, sl] = o_h
            d_st = lax.dot_general(v_c[:, sl], kend[:, sl], TN_DIMS, preferred_element_type=F32)
            states[h] = st * dec[:, sl] + d_st
    for h in range(HEADS):
        st_scr[h] = states[h]

    u = proj_scr[:, OFF_GC:OFF_GC + 512] * proj_scr[:, OFF_CX:OFF_CX + 512]
    row = lax.broadcasted_iota(I32, u.shape, 0)
    prev2 = ubuf_scr[SUBLANES - 2:SUBLANES - 1, :]
    prev1 = ubuf_scr[SUBLANES - 1:SUBLANES, :]
    u1 = jnp.where(row == 0, prev1, pltpu.roll(u, 1, axis=0))
    u2 = jnp.where(row == 0, prev2, jnp.where(row == 1, prev1, pltpu.roll(u, 2, axis=0)))
    y = (p_ref[0, P_CW0:P_CW0 + 1, :] * u2 + p_ref[0, P_CW1:P_CW1 + 1, :] * u1
         + p_ref[0, P_CW2:P_CW2 + 1, :] * u)
    ubuf_scr[...] = u[PROMPT_TILE - SUBLANES:, :]
    yc_in = proj_scr[:, OFF_GB:OFF_GB + 512] * y

    @pl.when(j == pl.num_programs(1) - 1)
    def _():
        for h in range(HEADS):
            s_ref[0, h] = jnp.transpose(st_scr[h])
        buf_ref[0] = u[PROMPT_TILE - (CONV_K - 1):, :]

    _post_mix(xt, o_scr[...], proj_scr[:, OFF_G:OFF_G + 512], yc_in, p_ref, wout_scr, ln_ref,
              wr_ref, rb_ref, x1_ref, route_ref)


def _sample_mixer_kernel(n_valid, x_ref, s0_ref, cbuf_ref, win_ref, wout_ref, p_ref, ln_ref,
                         wr_ref, rb_ref,
                         x1_ref, route_ref, s_ref, buf_ref,
                         win_scr, wout_scr, proj_scr, qs_scr, b_scr, k_scr, o_scr, y_scr):
    tile = SAMPLE_SEQS * SAMPLE_ROWS

    @pl.when(pl.program_id(0) == 0)
    def _():
        win_scr[...] = win_ref[0].astype(BF16)
        wout_scr[...] = wout_ref[0].astype(BF16)

    xt = x_ref[...]
    proj_scr[...] = jnp.dot(xt.astype(BF16), win_scr[...], preferred_element_type=F32)

    q = proj_scr[:, OFF_Q:OFF_Q + 512]
    qs_scr[...] = q * _sigmoid(q)
    logf, k = _gate_terms(proj_scr[:, OFF_F:OFF_F + 512], p_ref)
    valid = (lax.broadcasted_iota(I32, (tile, 512), 0) & (SAMPLE_ROWS - 1)) < n_valid
    k_scr[...] = jnp.where(valid, k, 0.0)
    b_scr[...] = _seg_cumsum(jnp.where(valid, logf, 0.0), SAMPLE_ROWS)

    trow = lax.broadcasted_iota(I32, (SAMPLE_ROWS, 1), 0)
    urow = lax.broadcasted_iota(I32, (SAMPLE_ROWS, 512), 0)

    def seq_body(s, carry):
        r0 = pl.multiple_of(s * SAMPLE_ROWS, SAMPLE_ROWS)
        rows = pl.ds(r0, SAMPLE_ROWS)
        qs_c = qs_scr[rows, :]
        b_c = b_scr[rows, :]
        k_c = k_scr[rows, :]
        v_c = proj_scr[rows, OFF_I:OFF_I + 512]
        b_last = b_c[n_valid - 1:n_valid, :]
        qb = (qs_c * jnp.exp(b_c)).astype(BF16)
        kend = (k_c * jnp.exp(b_last - b_c)).astype(BF16)
        dec = jnp.exp(b_last)
        for h in range(HEADS):
            sl = slice(h * HEAD_DIM, (h + 1) * HEAD_DIM)
            st = jnp.transpose(s0_ref[0, s, h])
            o_h = lax.dot_general(qb[:, sl], st.astype(BF16), NT_DIMS,
                                  preferred_element_type=F32)
            for t in range(n_valid):
                dlt = jnp.minimum(b_c[:, sl] - b_c[t:t + 1, sl], 0.0)
                a_col = jnp.sum(qs_c[:, sl] * k_c[t:t + 1, sl] * jnp.exp(dlt),
                                axis=-1, keepdims=True)
                a_col = jnp.where(trow >= t, a_col, 0.0)
                o_h = o_h + a_col * v_c[t:t + 1, sl]
            o_scr[rows, sl] = o_h
            d_st = lax.dot_general(v_c[:, sl].astype(BF16), kend[:, sl], TN_DIMS,
                                   preferred_element_type=F32)
            s_ref[s, h] = jnp.transpose(st * dec[:, sl] + d_st)
        u = proj_scr[rows, OFF_GC:OFF_GC + 512] * proj_scr[rows, OFF_CX:OFF_CX + 512]
        prev2 = cbuf_ref[0, s, 0:1, :]
        prev1 = cbuf_ref[0, s, 1:2, :]
        u1 = jnp.where(urow == 0, prev1, pltpu.roll(u, 1, axis=0))
        u2 = jnp.where(urow == 0, prev2, jnp.where(urow == 1, prev1, pltpu.roll(u, 2, axis=0)))
        y_scr[rows, :] = (p_ref[0, P_CW0:P_CW0 + 1, :] * u2 + p_ref[0, P_CW1:P_CW1 + 1, :] * u1
                          + p_ref[0, P_CW2:P_CW2 + 1, :] * u)
        buf_ref[s] = u[n_valid - (CONV_K - 1):n_valid, :]
        return carry

    lax.fori_loop(0, SAMPLE_SEQS, seq_body, 0, unroll=2)

    yc_in = proj_scr[:, OFF_GB:OFF_GB + 512] * y_scr[...]
    _post_mix(xt, o_scr[...], proj_scr[:, OFF_G:OFF_G + 512], yc_in, p_ref, wout_scr, ln_ref,
              wr_ref, rb_ref, x1_ref, route_ref)


def _plan_kernel(cls_ref, pos_ref, stats_ref):
    n_rows = cls_ref.shape[0]
    cls = cls_ref[...]
    upper = (lax.broadcasted_iota(I32, (LANES, LANES), 0)
             < lax.broadcasted_iota(I32, (LANES, LANES), 1)).astype(BF16)
    lower = (lax.broadcasted_iota(I32, (n_rows, n_rows), 1)
             < lax.broadcasted_iota(I32, (n_rows, n_rows), 0)).astype(BF16)
    lane = lax.broadcasted_iota(I32, (n_rows, LANES), 1)
    row_tot = jnp.zeros((n_rows, LANES), F32)
    for c in range(N_CLASSES):
        oh = jnp.where(cls == c, 1.0, 0.0)
        row_tot = jnp.where(lane == c, jnp.sum(oh, axis=1, keepdims=True), row_tot)
    before = jnp.dot(lower, row_tot.astype(BF16), preferred_element_type=F32)
    cnt = jnp.sum(row_tot, axis=0, keepdims=True)
    tiles = jnp.floor((cnt + (MOE_TILE - 1)) * (1.0 / MOE_TILE))
    first_tile = jnp.dot(jnp.broadcast_to(tiles, (SUBLANES, LANES)).astype(BF16), upper,
                         preferred_element_type=F32)[0:1]
    base = before + first_tile * MOE_TILE
    pos = jnp.zeros((n_rows, LANES), F32)
    for c in range(N_CLASSES):
        oh = jnp.where(cls == c, 1.0, 0.0)
        local = jnp.dot(oh.astype(BF16), upper, preferred_element_type=F32)
        pos = pos + oh * (base[:, c:c + 1] + local)
    pos_ref[...] = pos.astype(I32)
    sub = lax.broadcasted_iota(I32, (SUBLANES, LANES), 0)
    stats_ref[...] = jnp.where(sub == 0, cnt, jnp.where(sub == 1, tiles,
                                                        jnp.where(sub == 2, first_tile, 0.0)))


def _row_copy(src_ref, src_row, dst_ref, dst_row, sem):
    return pltpu.make_async_copy(src_ref.at[pl.ds(src_row, 1), :],
                                 dst_ref.at[pl.ds(dst_row, 1), :], sem)


def _tile_rows(row_copy, chunk_copy):
    def body(r, carry):
        row_copy(r).start()
        return carry
    lax.fori_loop(0, PERM_TILE, body, 0, unroll=8)
    chunk_copy.wait()


def _scatter_kernel(n_p_tiles, pos_ref, zst_ref, xp_ref, xs_ref, out_ref, zbuf, sem_z, sem_r):
    i = pl.program_id(0)

    def zero_copy(c):
        start = pl.multiple_of(zst_ref[c], MOE_TILE)
        return pltpu.make_async_copy(zbuf, out_ref.at[pl.ds(start, MOE_TILE), :], sem_z)

    @pl.when(i == 0)
    def _():
        zbuf[...] = jnp.zeros_like(zbuf)

        def z_start(c, carry):
            @pl.when(zst_ref[c] >= 0)
            def _():
                zero_copy(c).start()
            return carry

        def z_wait(c, carry):
            @pl.when(zst_ref[c] >= 0)
            def _():
                zero_copy(c).wait()
            return carry

        lax.fori_loop(0, 2 * N_CLASSES, z_start, 0)
        lax.fori_loop(0, 2 * N_CLASSES, z_wait, 0)

    base = i * PERM_TILE

    def from_tile(src_ref):
        _tile_rows(lambda r: _row_copy(src_ref, r, out_ref, pos_ref[base + r], sem_r),
                   pltpu.make_async_copy(src_ref, out_ref.at[pl.ds(0, PERM_TILE), :], sem_r))

    @pl.when(i < n_p_tiles)
    def _():
        from_tile(xp_ref)

    @pl.when(i >= n_p_tiles)
    def _():
        from_tile(xs_ref)


def _gather_kernel(n_p_tiles, pos_ref, ys_ref, outp_ref, outs_ref, sem_r):
    i = pl.program_id(0)
    base = i * PERM_TILE

    def to_tile(dst_ref):
        _tile_rows(lambda r: _row_copy(ys_ref, pos_ref[base + r], dst_ref, r, sem_r),
                   pltpu.make_async_copy(ys_ref.at[pl.ds(0, PERM_TILE), :], dst_ref, sem_r))

    @pl.when(i < n_p_tiles)
    def _():
        to_tile(outp_ref)

    @pl.when(i >= n_p_tiles)
    def _():
        to_tile(outs_ref)


def _expert_kernel(elo_ref, ehi_ref, blk_ref, flag_ref,
                   xs_ref, wg_lo, wu_lo, wd_lo, wg_hi, wu_hi, wd_hi, ln_ref, out_ref,
                   w1_scr, wd_scr):
    i = pl.program_id(0)
    flag = flag_ref[i]

    @pl.when((flag & 2) != 0)
    def _():
        w1_scr[:, 0 * EXPERT_FF:1 * EXPERT_FF] = wg_lo[0, 0].astype(BF16)
        w1_scr[:, 1 * EXPERT_FF:2 * EXPERT_FF] = wu_lo[0, 0].astype(BF16)
        w1_scr[:, 2 * EXPERT_FF:3 * EXPERT_FF] = wg_hi[0, 0].astype(BF16)
        w1_scr[:, 3 * EXPERT_FF:4 * EXPERT_FF] = wu_hi[0, 0].astype(BF16)
        wd_scr[0:EXPERT_FF, :] = wd_lo[0, 0].astype(BF16)
        wd_scr[EXPERT_FF:, :] = wd_hi[0, 0].astype(BF16)

    @pl.when((flag & 1) != 0)
    def _():
        x = xs_ref[:, 0:D_MODEL]
        g_lo = xs_ref[:, D_MODEL:D_MODEL + 1]
        g_hi = xs_ref[:, D_MODEL + 1:D_MODEL + 2]
        hc = jnp.dot(x.astype(BF16), w1_scr[...], preferred_element_type=F32)
        hg_lo, hu_lo = hc[:, 0:EXPERT_FF], hc[:, EXPERT_FF:2 * EXPERT_FF]
        hg_hi, hu_hi = hc[:, 2 * EXPERT_FF:3 * EXPERT_FF], hc[:, 3 * EXPERT_FF:]
        h_lo = hg_lo * _sigmoid(hg_lo) * hu_lo * g_lo
        h_hi = hg_hi * _sigmoid(hg_hi) * hu_hi * g_hi
        h = jnp.concatenate([h_lo, h_hi], axis=-1).astype(BF16)
        y = jnp.dot(h, wd_scr[...], preferred_element_type=F32)
        out_ref[...] = _layer_norm(ALPHA * x + y, ln_ref[0, 0:1, :], ln_ref[0, 1:2, :])

    @pl.when((flag & 1) == 0)
    def _():
        out_ref[...] = jnp.zeros_like(out_ref)


def _layer_spec(shape, layer, n_grid, single_buffer=False):
    idx = (layer,) + (0,) * len(shape)
    if n_grid == 1:
        index_map = lambda i: idx
    else:
        index_map = lambda i, j: idx
    if single_buffer:
        return pl.BlockSpec((1,) + shape, index_map, pipeline_mode=pl.Buffered(1))
    return pl.BlockSpec((1,) + shape, index_map)


def _prompt_mixer(x, w_in, w_out, params, ln, wr, rb, layer, batch, seq):
    n_t = seq // PROMPT_TILE
    n = batch * seq
    row_in = pl.BlockSpec((PROMPT_TILE, D_MODEL), lambda b, j: (b * n_t + j, 0))
    row_out = pl.BlockSpec((PROMPT_TILE, ROW_W), lambda b, j: (b * n_t + j, 0))
    route_out = pl.BlockSpec((SUBLANES, PROMPT_TILE), lambda b, j: (0, b * n_t + j))
    full2 = lambda shape: pl.BlockSpec(shape, lambda b, j: (0, 0))
    return pl.pallas_call(
        _prompt_mixer_kernel,
        grid=(batch, n_t),
        in_specs=[row_in,
                  _layer_spec((D_MODEL, N_IN), layer, 2, True),
                  _layer_spec((D_MODEL, D_MODEL), layer, 2, True),
                  _layer_spec((8, 512), layer, 2), _layer_spec((2, D_MODEL), layer, 2),
                  full2((2 * N_EXPERTS, D_MODEL)), full2((N_EXPERTS, 1))],
        out_specs=[row_out, route_out,
                   pl.BlockSpec((1, HEADS, HEAD_DIM, HEAD_DIM), lambda b, j: (b, 0, 0, 0)),
                   pl.BlockSpec((1, CONV_K - 1, CONV_DIM), lambda b, j: (b, 0, 0))],
        out_shape=[jax.ShapeDtypeStruct((n, ROW_W), F32),
                   jax.ShapeDtypeStruct((SUBLANES, n), F32),
                   jax.ShapeDtypeStruct((batch, HEADS, HEAD_DIM, HEAD_DIM), F32),
                   jax.ShapeDtypeStruct((batch, CONV_K - 1, CONV_DIM), F32)],
        scratch_shapes=[pltpu.VMEM((D_MODEL, N_IN), BF16), pltpu.VMEM((D_MODEL, D_MODEL), BF16),
                        pltpu.VMEM((PROMPT_TILE, N_IN), F32)]
        + [pltpu.VMEM((PROMPT_TILE, 512), F32)] * 4
        + [pltpu.VMEM((HEADS, HEAD_DIM, HEAD_DIM), F32), pltpu.VMEM((SUBLANES, CONV_DIM), F32)],
        compiler_params=pltpu.CompilerParams(
            dimension_semantics=("arbitrary", "arbitrary"), vmem_limit_bytes=VMEM_LIMIT),
        name="prompt_mixer",
    )(x, w_in, w_out, params, ln, wr, rb)


def _sample_mixer(x, s0, cbuf, w_in, w_out, params, ln, wr, rb, layer, batch, n_valid):
    tile = SAMPLE_SEQS * SAMPLE_ROWS
    n = batch * SAMPLE_ROWS
    row_in = pl.BlockSpec((tile, D_MODEL), lambda i: (i, 0))
    row_out = pl.BlockSpec((tile, ROW_W), lambda i: (i, 0))
    route_out = pl.BlockSpec((SUBLANES, tile), lambda i: (0, i))
    st_spec = pl.BlockSpec((1, SAMPLE_SEQS, HEADS, HEAD_DIM, HEAD_DIM),
                           lambda i: (layer, i, 0, 0, 0))
    cb_spec = pl.BlockSpec((1, SAMPLE_SEQS, CONV_K - 1, CONV_DIM), lambda i: (layer, i, 0, 0))
    st_out = pl.BlockSpec((SAMPLE_SEQS, HEADS, HEAD_DIM, HEAD_DIM), lambda i: (i, 0, 0, 0))
    cb_out = pl.BlockSpec((SAMPLE_SEQS, CONV_K - 1, CONV_DIM), lambda i: (i, 0, 0))
    full1 = lambda shape: pl.BlockSpec(shape, lambda i: (0, 0))
    return pl.pallas_call(
        functools.partial(_sample_mixer_kernel, n_valid),
        grid=(batch // SAMPLE_SEQS,),
        in_specs=[row_in, st_spec, cb_spec,
                  _layer_spec((D_MODEL, N_IN), layer, 1, True),
                  _layer_spec((D_MODEL, D_MODEL), layer, 1, True),
                  _layer_spec((8, 512), layer, 1), _layer_spec((2, D_MODEL), layer, 1),
                  full1((2 * N_EXPERTS, D_MODEL)), full1((N_EXPERTS, 1))],
        out_specs=[row_out, route_out, st_out, cb_out],
        out_shape=[jax.ShapeDtypeStruct((n, ROW_W), F32),
                   jax.ShapeDtypeStruct((SUBLANES, n), F32),
                   jax.ShapeDtypeStruct((batch, HEADS, HEAD_DIM, HEAD_DIM), F32),
                   jax.ShapeDtypeStruct((batch, CONV_K - 1, CONV_DIM), F32)],
        scratch_shapes=[pltpu.VMEM((D_MODEL, N_IN), BF16), pltpu.VMEM((D_MODEL, D_MODEL), BF16),
                        pltpu.VMEM((tile, N_IN), F32)] + [pltpu.VMEM((tile, 512), F32)] * 5,
        compiler_params=pltpu.CompilerParams(
            dimension_semantics=("arbitrary",), vmem_limit_bytes=VMEM_LIMIT),
        name="sample_mixer",
    )(x, s0, cbuf, w_in, w_out, params, ln, wr, rb)


def _plan(cls2d):
    n_rows = cls2d.shape[0]
    return pl.pallas_call(
        _plan_kernel,
        out_shape=[jax.ShapeDtypeStruct((n_rows, LANES), I32),
                   jax.ShapeDtypeStruct((SUBLANES, LANES), F32)],
        name="moe_plan",
    )(cls2d)


def _perm_specs(n_p_tiles, width):
    p_spec = pl.BlockSpec((PERM_TILE, width), lambda i, *_: (jnp.minimum(i, n_p_tiles - 1), 0))
    s_spec = pl.BlockSpec((PERM_TILE, width), lambda i, *_: (jnp.maximum(i - n_p_tiles, 0), 0))
    return p_spec, s_spec


def _scatter(pos, zstart, x1p, x1s, n_tiles):
    n_p, n_s = x1p.shape[0], x1s.shape[0]
    n_p_tiles = n_p // PERM_TILE
    p_spec, s_spec = _perm_specs(n_p_tiles, ROW_W)
    return pl.pallas_call(
        functools.partial(_scatter_kernel, n_p_tiles),
        grid_spec=pltpu.PrefetchScalarGridSpec(
            num_scalar_prefetch=2, grid=((n_p + n_s) // PERM_TILE,),
            in_specs=[p_spec, s_spec], out_specs=pl.BlockSpec(memory_space=pl.ANY),
            scratch_shapes=[pltpu.VMEM((MOE_TILE, ROW_W), F32),
                            pltpu.SemaphoreType.DMA(()), pltpu.SemaphoreType.DMA(())]),
        out_shape=jax.ShapeDtypeStruct((n_tiles * MOE_TILE, ROW_W), F32),
        compiler_params=pltpu.CompilerParams(dimension_semantics=("arbitrary",)),
        name="moe_scatter",
    )(pos, zstart, x1p, x1s)


def _gather(pos, ys, n_p, n_s):
    n_p_tiles = n_p // PERM_TILE
    p_spec, s_spec = _perm_specs(n_p_tiles, D_MODEL)
    return pl.pallas_call(
        functools.partial(_gather_kernel, n_p_tiles),
        grid_spec=pltpu.PrefetchScalarGridSpec(
            num_scalar_prefetch=1, grid=((n_p + n_s) // PERM_TILE,),
            in_specs=[pl.BlockSpec(memory_space=pl.ANY)], out_specs=[p_spec, s_spec],
            scratch_shapes=[pltpu.SemaphoreType.DMA(())]),
        out_shape=[jax.ShapeDtypeStruct((n_p, D_MODEL), F32),
                   jax.ShapeDtypeStruct((n_s, D_MODEL), F32)],
        compiler_params=pltpu.CompilerParams(dimension_semantics=("arbitrary",)),
        name="moe_gather",
    )(pos, ys)


def _experts(elo, ehi, blk, flag, xs, w_gate, w_up, w_down, ln, layer, n_tiles):
    def w_spec(shape, table_idx):
        def index_map(i, elo_r, ehi_r, blk_r, flag_r):
            return (layer, (elo_r, ehi_r)[table_idx][i], 0, 0)
        return pl.BlockSpec((1, 1) + shape, index_map)

    up = (D_MODEL, EXPERT_FF)
    down = (EXPERT_FF, D_MODEL)
    return pl.pallas_call(
        _expert_kernel,
        grid_spec=pltpu.PrefetchScalarGridSpec(
            num_scalar_prefetch=4, grid=(n_tiles,),
            in_specs=[pl.BlockSpec((MOE_TILE, ROW_W), lambda i, e0, e1, b, f: (b[i], 0)),
                      w_spec(up, 0), w_spec(up, 0), w_spec(down, 0),
                      w_spec(up, 1), w_spec(up, 1), w_spec(down, 1),
                      pl.BlockSpec((1, 2, D_MODEL), lambda i, e0, e1, b, f: (layer, 0, 0))],
            out_specs=pl.BlockSpec((MOE_TILE, D_MODEL), lambda i, e0, e1, b, f: (i, 0)),
            scratch_shapes=[pltpu.VMEM((D_MODEL, 4 * EXPERT_FF), BF16),
                            pltpu.VMEM((2 * EXPERT_FF, D_MODEL), BF16)]),
        out_shape=jax.ShapeDtypeStruct((n_tiles * MOE_TILE, D_MODEL), F32),
        compiler_params=pltpu.CompilerParams(
            dimension_semantics=("arbitrary",), vmem_limit_bytes=VMEM_LIMIT),
        name="moe_experts",
    )(elo, ehi, blk, flag, xs, w_gate, w_up, w_down, w_gate, w_up, w_down, ln)


def _moe(x1p, route_p, x1s, route_s, w_gate, w_up, w_down, ln2, layer):
    n_p, n_s = x1p.shape[0], x1s.shape[0]
    n = n_p + n_s
    n_tiles = n // MOE_TILE + N_CLASSES
    cls2d = jnp.concatenate([route_p[0], route_s[0]]).reshape(n // LANES, LANES)
    pos2d, stats = _plan(cls2d)
    pos = pos2d.reshape(n)

    tiles = stats[1, :N_CLASSES].astype(I32)
    end_tile = stats[2, :N_CLASSES].astype(I32) + tiles
    n_used = end_tile[N_CLASSES - 1]
    blk = jnp.minimum(jnp.arange(n_tiles, dtype=I32), n_used - 1)
    tcls = jnp.sum((end_tile[None, :] <= blk[:, None]).astype(I32), axis=1)
    valid = jnp.arange(n_tiles, dtype=I32) < n_used
    changed = jnp.concatenate([jnp.ones((1,), jnp.bool_), tcls[1:] != tcls[:-1]]) & valid
    flag = valid.astype(I32) + 2 * changed.astype(I32)
    group = tcls // N_PAIRS
    pair = tcls % N_PAIRS
    elo = group * PER_GROUP + jnp.asarray(PAIR_LO, I32)[pair]
    ehi = group * PER_GROUP + jnp.asarray(PAIR_HI, I32)[pair]
    trailing = n_tiles - N_CLASSES + jnp.arange(N_CLASSES, dtype=I32)
    zstart = jnp.concatenate([jnp.where(tiles > 0, (end_tile - 1) * MOE_TILE, -1),
                              jnp.where(trailing >= n_used, trailing * MOE_TILE, -1)])

    xs = _scatter(pos, zstart, x1p, x1s, n_tiles)
    ys = _experts(elo, ehi, blk, flag, xs, w_gate, w_up, w_down, ln2, layer, n_tiles)
    return _gather(pos, ys, n_p, n_s)


def kernel(x_prompt, x_sample, state_hgrn, state_conv, w_in, w_out, lower_bounds, hgrn_norm_g,
           conv_w, conv_norm_g, ln1_g, ln1_b, ln2_g, ln2_b, w_router, router_bias,
           w_gate, w_up, w_down):
    batch, seq, _ = x_prompt.shape
    dec_batch, dec_seq, _ = x_sample.shape
    assert seq % PROMPT_TILE == 0 and dec_batch % SAMPLE_SEQS == 0
    assert CONV_K - 1 <= dec_seq <= SAMPLE_ROWS
    assert (batch * seq + dec_batch * SAMPLE_ROWS) % MOE_TILE == 0

    lb = jnp.cumsum(jax.nn.softmax(lower_bounds.astype(F32), axis=0), axis=0)
    lb = lb - lb[0:1]
    params = jnp.stack([jnp.log(lb), jnp.log1p(-lb), 1.0 - lb, hgrn_norm_g, conv_norm_g,
                        conv_w[:, 0], conv_w[:, 1], conv_w[:, 2]], axis=1)
    ln1 = jnp.stack([ln1_g, ln1_b], axis=1)
    ln2 = jnp.stack([ln2_g, ln2_b], axis=1)
    wr_hi = w_router.astype(BF16)
    wr_lo = (w_router - wr_hi.astype(F32)).astype(BF16)
    wr = jnp.concatenate([wr_hi.T, wr_lo.T], axis=0)
    rb = router_bias.astype(F32).reshape(N_EXPERTS, 1)

    xp = x_prompt.reshape(batch * seq, D_MODEL)
    xs = jnp.pad(x_sample, ((0, 0), (0, SAMPLE_ROWS - dec_seq), (0, 0)))
    xs = xs.reshape(dec_batch * SAMPLE_ROWS, D_MODEL)

    s_p, b_p, s_s, b_s = [], [], [], []
    for l in range(DEPTH):
        x1p, route_p, s_l, b_l = _prompt_mixer(xp, w_in, w_out, params, ln1, wr, rb, l, batch, seq)
        s_p.append(s_l)
        b_p.append(b_l)
        x1s, route_s, s_l, b_l = _sample_mixer(xs, state_hgrn, state_conv, w_in, w_out, params, ln1,
                                               wr, rb, l, dec_batch, dec_seq)
        s_s.append(s_l)
        b_s.append(b_l)
        xp, xs = _moe(x1p, route_p, x1s, route_s, w_gate, w_up, w_down, ln2, l)

    y_prompt = xp.reshape(batch, seq, D_MODEL)
    y_sample = xs.reshape(dec_batch, SAMPLE_ROWS, D_MODEL)[:, :dec_seq]
    return (y_prompt, y_sample, jnp.stack(s_p), jnp.stack(b_p), jnp.stack(s_s), jnp.stack(b_s))
```

```python
import functools

import jax
import jax.numpy as jnp
from jax import lax
from jax.experimental import pallas as pl
from jax.experimental.pallas import tpu as pltpu

F32 = jnp.float32
BF16 = jnp.bfloat16
I32 = jnp.int32

D_MODEL = 1024
DEPTH = 2
HEADS = 4
HEAD_DIM = 128
HGRN_W = HEADS * HEAD_DIM
CONV_DIM = 512
CONV_GROUPS = 8
CONV_K = 3
N_IN = 7 * 512
N_EXPERTS = 16
N_GROUPS = 4
PER_GROUP = 4
N_PAIRS = 6
N_CLASSES = N_GROUPS * N_PAIRS
PAIR_LO = (0, 0, 0, 1, 1, 2)
PAIR_HI = (1, 2, 3, 2, 3, 3)
EXPERT_FF = 512
ALPHA = (2 * DEPTH) ** 0.25
LN_EPS = 1e-5
RMS_EPS = 1e-6

LANES = 128
SUBLANES = 8
CHUNK = 64
PROMPT_TILE = 256
SAMPLE_ROWS = 8
SAMPLE_SEQS = 16
MOE_TILE = 256
ROW_W = D_MODEL + LANES
PAD_ROWS = N_CLASSES * MOE_TILE
VMEM_LIMIT = 56 * 1024 * 1024

OFF_Q, OFF_F, OFF_I, OFF_G, OFF_GB, OFF_GC, OFF_CX = (i * 512 for i in range(7))

P_LOGLB, P_LOG1MLB, P_OMLB, P_HNORM, P_CNORM, P_CW0, P_CW1, P_CW2 = range(8)

NT_DIMS = (((1,), (1,)), ((), ()))
TN_DIMS = (((0,), (0,)), ((), ()))


def _sigmoid(x):
    return 1.0 / (1.0 + jnp.exp(-x))


def _seg_cumsum(x, seg):
    row = lax.broadcasted_iota(I32, x.shape, 0)
    pos = row & (seg - 1)
    sh = 1
    while sh < seg:
        x = x + jnp.where(pos >= sh, pltpu.roll(x, sh, axis=0), 0.0)
        sh *= 2
    return x


def _gate_terms(z, p_ref):
    e = jnp.exp(-jnp.abs(z))
    inv = 1.0 / (1.0 + e)
    logsig = jnp.minimum(z, 0.0) - jnp.log(1.0 + e)
    a = p_ref[0, P_LOGLB:P_LOGLB + 1, :]
    b = p_ref[0, P_LOG1MLB:P_LOG1MLB + 1, :] + logsig
    logf = jnp.maximum(a, b) + jnp.log(1.0 + jnp.exp(-jnp.abs(a - b)))
    k = p_ref[0, P_OMLB:P_OMLB + 1, :] * (jnp.where(z >= 0.0, e, 1.0) * inv)
    return logf, k


def _group_rms(x, n_groups):
    width = x.shape[-1] // n_groups
    x2 = x * x
    outs = []
    for s in range(x.shape[-1] // LANES):
        xs = x[:, s * LANES:(s + 1) * LANES]
        x2s = x2[:, s * LANES:(s + 1) * LANES]
        if width == LANES:
            ms = jnp.sum(x2s, axis=-1, keepdims=True) * (1.0 / width)
            scale = lax.rsqrt(ms + RMS_EPS)
        else:
            lane = lax.broadcasted_iota(I32, xs.shape, 1)
            lo = lane < width
            ms_lo = jnp.sum(jnp.where(lo, x2s, 0.0), axis=-1, keepdims=True) * (1.0 / width)
            ms_hi = jnp.sum(jnp.where(lo, 0.0, x2s), axis=-1, keepdims=True) * (1.0 / width)
            scale = jnp.where(lo, lax.rsqrt(ms_lo + RMS_EPS), lax.rsqrt(ms_hi + RMS_EPS))
        outs.append(xs * scale)
    return jnp.concatenate(outs, axis=-1)


def _layer_norm(r, g, b):
    mu = jnp.mean(r, axis=-1, keepdims=True)
    rc = r - mu
    var = jnp.mean(rc * rc, axis=-1, keepdims=True)
    return rc * lax.rsqrt(var + LN_EPS) * g + b


def _route(x1, wr_ref, rb_ref):
    x_hi = x1.astype(BF16)
    x_lo = (x1 - x_hi.astype(F32)).astype(BF16)
    wr = wr_ref[...]
    r1 = lax.dot_general(wr, x_hi, NT_DIMS, preferred_element_type=F32)
    r2 = lax.dot_general(wr, x_lo, NT_DIMS, preferred_element_type=F32)
    lt = r1[0:N_EXPERTS] + r1[N_EXPERTS:] + r2[0:N_EXPERTS] + r2[N_EXPERTS:] + rb_ref[...]
    lg = [lt[e:e + 1, :] for e in range(N_EXPERTS)]
    mx = lg[0]
    for e in range(1, N_EXPERTS):
        mx = jnp.maximum(mx, lg[e])
    ex = [jnp.exp(l - mx) for l in lg]
    best = None
    gi = None
    for g in range(N_GROUPS):
        a, b, c, d = ex[PER_GROUP * g:PER_GROUP * (g + 1)]
        s = jnp.maximum(jnp.maximum(jnp.maximum(a + b, a + c), jnp.maximum(a + d, b + c)),
                        jnp.maximum(b + d, c + d))
        if g == 0:
            best, gi = s, jnp.zeros(s.shape, I32)
        else:
            upd = s > best
            best = jnp.where(upd, s, best)
            gi = jnp.where(upd, g, gi)
    v = []
    for i in range(PER_GROUP):
        vi = ex[3 * PER_GROUP + i]
        for g in (2, 1, 0):
            vi = jnp.where(gi == g, ex[PER_GROUP * g + i], vi)
        v.append(vi)
    w1, i1 = v[0], jnp.zeros(v[0].shape, I32)
    for i in range(1, PER_GROUP):
        upd = v[i] > w1
        w1 = jnp.where(upd, v[i], w1)
        i1 = jnp.where(upd, i, i1)
    w2, i2 = None, None
    for i in range(PER_GROUP):
        vi = jnp.where(i1 == i, -1.0, v[i])
        if i == 0:
            w2, i2 = vi, jnp.zeros(vi.shape, I32)
        else:
            upd = vi > w2
            w2 = jnp.where(upd, vi, w2)
            i2 = jnp.where(upd, i, i2)
    inv = 1.0 / (w1 + w2)
    first_lo = i1 < i2
    lo = jnp.where(first_lo, i1, i2)
    hi = jnp.where(first_lo, i2, i1)
    g_lo = jnp.where(first_lo, w1, w2) * inv
    g_hi = jnp.where(first_lo, w2, w1) * inv
    pair = jnp.where(lo == 0, 0, jnp.where(lo == 1, 3, 5)) + hi - lo - 1
    return gi * N_PAIRS + pair, g_lo, g_hi


def _post_mix(xt, o, g, yc_in, p_ref, wout_scr, ln_ref, wr_ref, rb_ref, x1_ref, route_ref):
    rows = xt.shape[0]
    o = _group_rms(o, HEADS) * p_ref[0, P_HNORM:P_HNORM + 1, :]
    o = o * (g * _sigmoid(g))
    yc = _group_rms(yc_in, CONV_GROUPS) * p_ref[0, P_CNORM:P_CNORM + 1, :]
    mix = jnp.concatenate([o, yc], axis=-1).astype(BF16)
    h = jnp.dot(mix, wout_scr[...], preferred_element_type=F32)
    x1 = _layer_norm(ALPHA * xt + h, ln_ref[0, 0:1, :], ln_ref[0, 1:2, :])
    cls, g_lo, g_hi = _route(x1, wr_ref, rb_ref)
    x1_ref[:, 0:D_MODEL] = x1
    sub = lax.broadcasted_iota(I32, (LANES, rows), 0)
    gates_t = jnp.where(sub == 0, g_lo, jnp.where(sub == 1, g_hi, 0.0))
    x1_ref[:, D_MODEL:ROW_W] = jnp.transpose(gates_t)
    sub8 = lax.broadcasted_iota(I32, (SUBLANES, rows), 0)
    route_ref[...] = jnp.where(sub8 == 0, cls.astype(F32), 0.0)


def _prompt_mixer_kernel(n_real, n_t, x_ref, *refs):
    i = pl.program_id(0)
    x1_ref = refs[6]

    @pl.when(i < n_real)
    def _():
        _prompt_tile(i, lax.rem(i, n_t), n_t, x_ref, *refs)

    @pl.when(i >= n_real)
    def _():
        x1_ref[...] = jnp.zeros_like(x1_ref)


def _prompt_tile(i, j, n_t, x_ref, win_ref, wout_ref, p_ref, ln_ref, wr_ref, rb_ref,
                 x1_ref, route_ref, s_ref, buf_ref,
                 win_scr, wout_scr, proj_scr, qs_scr, b_scr, k_scr, o_scr, st_scr,
                 ubuf_scr):
    @pl.when(i == 0)
    def _():
        win_scr[...] = win_ref[0].astype(BF16)
        wout_scr[...] = wout_ref[0].astype(BF16)

    @pl.when(j == 0)
    def _():
        st_scr[...] = jnp.zeros_like(st_scr)
        ubuf_scr[...] = jnp.zeros_like(ubuf_scr)

    xt = x_ref[...]
    proj_scr[...] = jnp.dot(xt.astype(BF16), win_scr[...], preferred_element_type=F32)

    q = proj_scr[:, OFF_Q:OFF_Q + 512]
    qs_scr[...] = q * _sigmoid(q)
    logf, k = _gate_terms(proj_scr[:, OFF_F:OFF_F + 512], p_ref)
    k_scr[...] = k
    b_scr[...] = _seg_cumsum(logf, CHUNK)

    tri = (lax.broadcasted_iota(I32, (CHUNK, CHUNK), 0)
           >= lax.broadcasted_iota(I32, (CHUNK, CHUNK), 1))

    states = [st_scr[h] for h in range(HEADS)]
    for c in range(PROMPT_TILE // CHUNK):
        rows = slice(c * CHUNK, (c + 1) * CHUNK)
        qs_c = qs_scr[rows, :]
        b_c = b_scr[rows, :]
        k_c = k_scr[rows, :]
        v_c = proj_scr[rows, OFF_I:OFF_I + 512].astype(BF16)
        b_mid = b_c[CHUNK // 2 - 1:CHUNK // 2, :]
        b_last = b_c[CHUNK - 1:CHUNK, :]
        qd = (qs_c * jnp.exp(b_c - b_mid)).astype(BF16)
        kd = (k_c * jnp.exp(b_mid - b_c)).astype(BF16)
        qb = (qs_c * jnp.exp(b_c)).astype(BF16)
        kend = (k_c * jnp.exp(b_last - b_c)).astype(BF16)
        dec = jnp.exp(b_last)
        for h in range(HEADS):
            sl = slice(h * HEAD_DIM, (h + 1) * HEAD_DIM)
            att = lax.dot_general(qd[:, sl], kd[:, sl], NT_DIMS, preferred_element_type=F32)
            att = jnp.where(tri, att, 0.0).astype(BF16)
            st = states[h]
            o_h = jnp.dot(att, v_c[:, sl], preferred_element_type=F32)
            o_h = o_h + lax.dot_general(qb[:, sl], st.astype(BF16), NT_DIMS,
                                        preferred_element_type=F32)
            o_scr[rows, sl] = o_h
            d_st = lax.dot_general(v_c[:, sl], kend[:, sl], TN_DIMS, preferred_element_type=F32)
            states[h] = st * dec[:, sl] + d_st
    for h in range(HEADS):
        st_scr[h] = states[h]

    u = proj_scr[:, OFF_GC:OFF_GC + 512] * proj_scr[:, OFF_CX:OFF_CX + 512]
    row = lax.broadcasted_iota(I32, u.shape, 0)
    prev2 = ubuf_scr[SUBLANES - 2:SUBLANES - 1, :]
    prev1 = ubuf_scr[SUBLANES - 1:SUBLANES, :]
    u1 = jnp.where(row == 0, prev1, pltpu.roll(u, 1, axis=0))
    u2 = jnp.where(row == 0, prev2, jnp.where(row == 1, prev1, pltpu.roll(u, 2, axis=0)))
    y = (p_ref[0, P_CW0:P_CW0 + 1, :] * u2 + p_ref[0, P_CW1:P_CW1 + 1, :] * u1
         + p_ref[0, P_CW2:P_CW2 + 1, :] * u)
    ubuf_scr[...] = u[PROMPT_TILE - SUBLANES:, :]
    yc_in = proj_scr[:, OFF_GB:OFF_GB + 512] * y

    @pl.when(j == n_t - 1)
    def _():
        for h in range(HEADS):
            s_ref[0, h] = jnp.transpose(st_scr[h])
        buf_ref[0] = u[PROMPT_TILE - (CONV_K - 1):, :]

    _post_mix(xt, o_scr[...], proj_scr[:, OFF_G:OFF_G + 512], yc_in, p_ref, wout_scr, ln_ref,
              wr_ref, rb_ref, x1_ref, route_ref)


def _sample_mixer_kernel(n_valid, x_ref, s0_ref, cbuf_ref, win_ref, wout_ref, p_ref, ln_ref,
                         wr_ref, rb_ref, x1_alias_ref,
                         x1_ref, route_ref, s_ref, buf_ref,
                         win_scr, wout_scr, proj_scr, qs_scr, b_scr, k_scr, o_scr, y_scr):
    tile = SAMPLE_SEQS * SAMPLE_ROWS

    @pl.when(pl.program_id(0) == 0)
    def _():
        win_scr[...] = win_ref[0].astype(BF16)
        wout_scr[...] = wout_ref[0].astype(BF16)

    xt = x_ref[...]
    proj_scr[...] = jnp.dot(xt.astype(BF16), win_scr[...], preferred_element_type=F32)

    q = proj_scr[:, OFF_Q:OFF_Q + 512]
    qs_scr[...] = q * _sigmoid(q)
    logf, k = _gate_terms(proj_scr[:, OFF_F:OFF_F + 512], p_ref)
    valid = (lax.broadcasted_iota(I32, (tile, 512), 0) & (SAMPLE_ROWS - 1)) < n_valid
    k_scr[...] = jnp.where(valid, k, 0.0)
    b_scr[...] = _seg_cumsum(jnp.where(valid, logf, 0.0), SAMPLE_ROWS)

    trow = lax.broadcasted_iota(I32, (SAMPLE_ROWS, 1), 0)
    urow = lax.broadcasted_iota(I32, (SAMPLE_ROWS, 512), 0)

    def seq_body(s, carry):
        r0 = pl.multiple_of(s * SAMPLE_ROWS, SAMPLE_ROWS)
        rows = pl.ds(r0, SAMPLE_ROWS)
        qs_c = qs_scr[rows, :]
        b_c = b_scr[rows, :]
        k_c = k_scr[rows, :]
        v_c = proj_scr[rows, OFF_I:OFF_I + 512]
        b_last = b_c[n_valid - 1:n_valid, :]
        qb = (qs_c * jnp.exp(b_c)).astype(BF16)
        kend = (k_c * jnp.exp(b_last - b_c)).astype(BF16)
        dec = jnp.exp(b_last)
        for h in range(HEADS):
            sl = slice(h * HEAD_DIM, (h + 1) * HEAD_DIM)
            st = jnp.transpose(s0_ref[0, s, h])
            o_h = lax.dot_general(qb[:, sl], st.astype(BF16), NT_DIMS,
                                  preferred_element_type=F32)
            for t in range(n_valid):
                dlt = jnp.minimum(b_c[:, sl] - b_c[t:t + 1, sl], 0.0)
                a_col = jnp.sum(qs_c[:, sl] * k_c[t:t + 1, sl] * jnp.exp(dlt),
                                axis=-1, keepdims=True)
                a_col = jnp.where(trow >= t, a_col, 0.0)
                o_h = o_h + a_col * v_c[t:t + 1, sl]
            o_scr[rows, sl] = o_h
            d_st = lax.dot_general(v_c[:, sl].astype(BF16), kend[:, sl], TN_DIMS,
                                   preferred_element_type=F32)
            s_ref[s, h] = jnp.transpose(st * dec[:, sl] + d_st)
        u = proj_scr[rows, OFF_GC:OFF_GC + 512] * proj_scr[rows, OFF_CX:OFF_CX + 512]
        prev2 = cbuf_ref[0, s, 0:1, :]
        prev1 = cbuf_ref[0, s, 1:2, :]
        u1 = jnp.where(urow == 0, prev1, pltpu.roll(u, 1, axis=0))
        u2 = jnp.where(urow == 0, prev2, jnp.where(urow == 1, prev1, pltpu.roll(u, 2, axis=0)))
        y_scr[rows, :] = (p_ref[0, P_CW0:P_CW0 + 1, :] * u2 + p_ref[0, P_CW1:P_CW1 + 1, :] * u1
                          + p_ref[0, P_CW2:P_CW2 + 1, :] * u)
        buf_ref[s] = u[n_valid - (CONV_K - 1):n_valid, :]
        return carry

    lax.fori_loop(0, SAMPLE_SEQS, seq_body, 0)

    yc_in = proj_scr[:, OFF_GB:OFF_GB + 512] * y_scr[...]
    _post_mix(xt, o_scr[...], proj_scr[:, OFF_G:OFF_G + 512], yc_in, p_ref, wout_scr, ln_ref,
              wr_ref, rb_ref, x1_ref, route_ref)


def _plan_kernel(cls_ref, pos_ref, stats_ref):
    n_rows = cls_ref.shape[0]
    cls = cls_ref[...]
    upper = (lax.broadcasted_iota(I32, (LANES, LANES), 0)
             < lax.broadcasted_iota(I32, (LANES, LANES), 1)).astype(BF16)
    lower = (lax.broadcasted_iota(I32, (n_rows, n_rows), 1)
             < lax.broadcasted_iota(I32, (n_rows, n_rows), 0)).astype(BF16)
    lane = lax.broadcasted_iota(I32, (n_rows, LANES), 1)
    row_tot = jnp.zeros((n_rows, LANES), F32)
    for c in range(N_CLASSES):
        oh = jnp.where(cls == c, 1.0, 0.0)
        row_tot = jnp.where(lane == c, jnp.sum(oh, axis=1, keepdims=True), row_tot)
    before = jnp.dot(lower, row_tot.astype(BF16), preferred_element_type=F32)
    cnt = jnp.sum(row_tot, axis=0, keepdims=True)
    tiles = jnp.floor((cnt + (MOE_TILE - 1)) * (1.0 / MOE_TILE))
    first_tile = jnp.dot(jnp.broadcast_to(tiles, (SUBLANES, LANES)).astype(BF16), upper,
                         preferred_element_type=F32)[0:1]
    base = before + first_tile * MOE_TILE
    pos = jnp.zeros((n_rows, LANES), F32)
    for c in range(N_CLASSES):
        oh = jnp.where(cls == c, 1.0, 0.0)
        local = jnp.dot(oh.astype(BF16), upper, preferred_element_type=F32)
        pos = pos + oh * (base[:, c:c + 1] + local)
    pos_ref[...] = pos.astype(I32)
    sub = lax.broadcasted_iota(I32, (SUBLANES, LANES), 0)
    stats_ref[...] = jnp.where(sub == 0, cnt, jnp.where(sub == 1, tiles,
                                                        jnp.where(sub == 2, first_tile, 0.0)))


def _row_copy(src_ref, src_row, dst_ref, dst_row, sem):
    return pltpu.make_async_copy(src_ref.at[pl.ds(src_row, 1), :],
                                 dst_ref.at[pl.ds(dst_row, 1), :], sem)


def _perm_kernel(n, pos_ref, cstart_ref, ccnt_ref, cpad_ref, nused_ref, perm_ref):
    def stand_in_body(r, carry):
        perm_ref[r] = n + PAD_ROWS + r
        return carry

    lax.fori_loop(0, MOE_TILE, stand_in_body, 0, unroll=8)

    def unused_body(p, carry):
        perm_ref[p] = n
        return carry

    lax.fori_loop((nused_ref[0] + 1) * MOE_TILE, perm_ref.shape[0], unused_body, 0)

    def cls_body(c, k):
        base = MOE_TILE + cstart_ref[c] + ccnt_ref[c]

        def pad_body(q, carry):
            perm_ref[base + q] = n + k + q
            return carry

        lax.fori_loop(0, cpad_ref[c], pad_body, 0)
        return k + cpad_ref[c]

    lax.fori_loop(0, N_CLASSES, cls_body, 0)

    def tok_body(t, carry):
        perm_ref[MOE_TILE + pos_ref[t]] = t
        return carry

    lax.fori_loop(0, n, tok_body, 0, unroll=8)


def _expert_kernel(n, perm_ref, elo_ref, ehi_ref, flag_ref,
                   x_ref, wg_lo, wu_lo, wd_lo, wg_hi, wu_hi, wd_hi, ln_ref, out_ref,
                   xbuf, ybuf, w1_scr, wd_scr, gsem, ssem, zsem):
    i = pl.program_id(0)
    flag = flag_ref[i]
    slot = i & 1
    other = 1 - slot

    def gather_wait(s):
        pltpu.make_async_copy(x_ref.at[pl.ds(0, MOE_TILE), :], xbuf.at[s], gsem.at[s]).wait()

    def scatter_wait(s):
        pltpu.make_async_copy(ybuf.at[s], out_ref.at[pl.ds(0, MOE_TILE), :], ssem.at[s]).wait()

    def gather_start(tile, s):
        for r in range(MOE_TILE):
            _row_copy(x_ref, perm_ref[(tile + 1) * MOE_TILE + r], xbuf.at[s], r, gsem.at[s]).start()

    def scatter_start(tile, s):
        for r in range(MOE_TILE):
            _row_copy(ybuf.at[s], r, out_ref, perm_ref[(tile + 1) * MOE_TILE + r], ssem.at[s]).start()

    @pl.when(i == 0)
    def _():
        ybuf[...] = jnp.zeros_like(ybuf)
        for c in range(PAD_ROWS // MOE_TILE):
            pltpu.make_async_copy(ybuf.at[0], out_ref.at[pl.ds(n + c * MOE_TILE, MOE_TILE), :],
                                  zsem).start()
        for c in range(PAD_ROWS // MOE_TILE):
            pltpu.make_async_copy(ybuf.at[0], out_ref.at[pl.ds(n + c * MOE_TILE, MOE_TILE), :],
                                  zsem).wait()
        pltpu.make_async_copy(ybuf.at[0],
                              out_ref.at[pl.ds(n + PAD_ROWS + MOE_TILE, MOE_TILE), :],
                              ssem.at[0]).start()
        gather_start(0, 0)

    @pl.when((flag & 2) != 0)
    def _():
        w1_scr[:, 0 * EXPERT_FF:1 * EXPERT_FF] = wg_lo[0, 0].astype(BF16)
        w1_scr[:, 1 * EXPERT_FF:2 * EXPERT_FF] = wu_lo[0, 0].astype(BF16)
        w1_scr[:, 2 * EXPERT_FF:3 * EXPERT_FF] = wg_hi[0, 0].astype(BF16)
        w1_scr[:, 3 * EXPERT_FF:4 * EXPERT_FF] = wu_hi[0, 0].astype(BF16)
        wd_scr[0:EXPERT_FF, :] = wd_lo[0, 0].astype(BF16)
        wd_scr[EXPERT_FF:, :] = wd_hi[0, 0].astype(BF16)

    @pl.when((flag & 1) != 0)
    def _():
        gather_wait(slot)
        scatter_wait(slot)
        gather_start(i + 1, other)
        scatter_start(i - 1, other)
        xs = xbuf[slot]
        x = xs[:, 0:D_MODEL]
        g_lo = xs[:, D_MODEL:D_MODEL + 1]
        g_hi = xs[:, D_MODEL + 1:D_MODEL + 2]
        hc = jnp.dot(x.astype(BF16), w1_scr[...], preferred_element_type=F32)
        hg_lo, hu_lo = hc[:, 0:EXPERT_FF], hc[:, EXPERT_FF:2 * EXPERT_FF]
        hg_hi, hu_hi = hc[:, 2 * EXPERT_FF:3 * EXPERT_FF], hc[:, 3 * EXPERT_FF:]
        h_lo = hg_lo * _sigmoid(hg_lo) * hu_lo * g_lo
        h_hi = hg_hi * _sigmoid(hg_hi) * hu_hi * g_hi
        h = jnp.concatenate([h_lo, h_hi], axis=-1).astype(BF16)
        y = jnp.dot(h, wd_scr[...], preferred_element_type=F32)
        ybuf[slot] = _layer_norm(ALPHA * x + y, ln_ref[0, 0:1, :], ln_ref[0, 1:2, :])

    @pl.when((flag & 4) != 0)
    def _():
        scatter_start(i - 1, other)
        scatter_wait(slot)
        scatter_wait(other)
        gather_wait(slot)


def _layer_spec(shape, layer, n_grid, single_buffer=False):
    idx = (layer,) + (0,) * len(shape)
    if n_grid == 1:
        index_map = lambda i: idx
    else:
        index_map = lambda i, j: idx
    if single_buffer:
        return pl.BlockSpec((1,) + shape, index_map, pipeline_mode=pl.Buffered(1))
    return pl.BlockSpec((1,) + shape, index_map)


def _prompt_mixer(x, w_in, w_out, params, ln, wr, rb, layer, batch, seq, n_rows_out):
    n_t = seq // PROMPT_TILE
    n_real = batch * n_t
    last = n_real - 1
    row_in = pl.BlockSpec((PROMPT_TILE, D_MODEL), lambda i: (jnp.minimum(i, last), 0))
    row_out = pl.BlockSpec((PROMPT_TILE, ROW_W), lambda i: (i, 0))
    route_out = pl.BlockSpec((SUBLANES, PROMPT_TILE), lambda i: (0, jnp.minimum(i, last)))
    full1 = lambda shape: pl.BlockSpec(shape, lambda i: (0, 0))
    seq_of = lambda i: jnp.minimum(i, last) // n_t
    return pl.pallas_call(
        functools.partial(_prompt_mixer_kernel, n_real, n_t),
        grid=(n_rows_out // PROMPT_TILE,),
        in_specs=[row_in,
                  _layer_spec((D_MODEL, N_IN), layer, 1, True),
                  _layer_spec((D_MODEL, D_MODEL), layer, 1, True),
                  _layer_spec((8, 512), layer, 1), _layer_spec((2, D_MODEL), layer, 1),
                  full1((2 * N_EXPERTS, D_MODEL)), full1((N_EXPERTS, 1))],
        out_specs=[row_out, route_out,
                   pl.BlockSpec((1, HEADS, HEAD_DIM, HEAD_DIM), lambda i: (seq_of(i), 0, 0, 0)),
                   pl.BlockSpec((1, CONV_K - 1, CONV_DIM), lambda i: (seq_of(i), 0, 0))],
        out_shape=[jax.ShapeDtypeStruct((n_rows_out, ROW_W), F32),
                   jax.ShapeDtypeStruct((SUBLANES, batch * seq), F32),
                   jax.ShapeDtypeStruct((batch, HEADS, HEAD_DIM, HEAD_DIM), F32),
                   jax.ShapeDtypeStruct((batch, CONV_K - 1, CONV_DIM), F32)],
        scratch_shapes=[pltpu.VMEM((D_MODEL, N_IN), BF16), pltpu.VMEM((D_MODEL, D_MODEL), BF16),
                        pltpu.VMEM((PROMPT_TILE, N_IN), F32)]
        + [pltpu.VMEM((PROMPT_TILE, 512), F32)] * 4
        + [pltpu.VMEM((HEADS, HEAD_DIM, HEAD_DIM), F32), pltpu.VMEM((SUBLANES, CONV_DIM), F32)],
        compiler_params=pltpu.CompilerParams(
            dimension_semantics=("arbitrary",), vmem_limit_bytes=VMEM_LIMIT),
        name="prompt_mixer",
    )(x, w_in, w_out, params, ln, wr, rb)


def _sample_mixer(x, x_row0, x1_all, s0, cbuf, w_in, w_out, params, ln, wr, rb, layer, batch,
                  n_valid):
    tile = SAMPLE_SEQS * SAMPLE_ROWS
    n = batch * SAMPLE_ROWS
    x1_row0 = x1_all.shape[0] - PAD_ROWS - n
    in_blk0, out_blk0 = x_row0 // tile, x1_row0 // tile
    row_in = pl.BlockSpec((tile, D_MODEL), lambda i: (in_blk0 + i, 0))
    row_out = pl.BlockSpec((tile, ROW_W), lambda i: (out_blk0 + i, 0))
    route_out = pl.BlockSpec((SUBLANES, tile), lambda i: (0, i))
    st_spec = pl.BlockSpec((1, SAMPLE_SEQS, HEADS, HEAD_DIM, HEAD_DIM),
                           lambda i: (layer, i, 0, 0, 0))
    cb_spec = pl.BlockSpec((1, SAMPLE_SEQS, CONV_K - 1, CONV_DIM), lambda i: (layer, i, 0, 0))
    st_out = pl.BlockSpec((SAMPLE_SEQS, HEADS, HEAD_DIM, HEAD_DIM), lambda i: (i, 0, 0, 0))
    cb_out = pl.BlockSpec((SAMPLE_SEQS, CONV_K - 1, CONV_DIM), lambda i: (i, 0, 0))
    full1 = lambda shape: pl.BlockSpec(shape, lambda i: (0, 0))
    return pl.pallas_call(
        functools.partial(_sample_mixer_kernel, n_valid),
        grid=(batch // SAMPLE_SEQS,),
        in_specs=[row_in, st_spec, cb_spec,
                  _layer_spec((D_MODEL, N_IN), layer, 1, True),
                  _layer_spec((D_MODEL, D_MODEL), layer, 1, True),
                  _layer_spec((8, 512), layer, 1), _layer_spec((2, D_MODEL), layer, 1),
                  full1((2 * N_EXPERTS, D_MODEL)), full1((N_EXPERTS, 1)),
                  pl.BlockSpec(memory_space=pl.ANY)],
        out_specs=[row_out, route_out, st_out, cb_out],
        input_output_aliases={9: 0},
        out_shape=[jax.ShapeDtypeStruct(x1_all.shape, F32),
                   jax.ShapeDtypeStruct((SUBLANES, n), F32),
                   jax.ShapeDtypeStruct((batch, HEADS, HEAD_DIM, HEAD_DIM), F32),
                   jax.ShapeDtypeStruct((batch, CONV_K - 1, CONV_DIM), F32)],
        scratch_shapes=[pltpu.VMEM((D_MODEL, N_IN), BF16), pltpu.VMEM((D_MODEL, D_MODEL), BF16),
                        pltpu.VMEM((tile, N_IN), F32)] + [pltpu.VMEM((tile, 512), F32)] * 5,
        compiler_params=pltpu.CompilerParams(
            dimension_semantics=("arbitrary",), vmem_limit_bytes=VMEM_LIMIT),
        name="sample_mixer",
    )(x, s0, cbuf, w_in, w_out, params, ln, wr, rb, x1_all)


def _plan(cls2d):
    n_rows = cls2d.shape[0]
    return pl.pallas_call(
        _plan_kernel,
        out_shape=[jax.ShapeDtypeStruct((n_rows, LANES), I32),
                   jax.ShapeDtypeStruct((SUBLANES, LANES), F32)],
        name="moe_plan",
    )(cls2d)


def _perm(n, pos, cstart, ccnt, cpad, n_used, n_tiles):
    smem = pl.BlockSpec(memory_space=pltpu.SMEM)
    return pl.pallas_call(
        functools.partial(_perm_kernel, n),
        in_specs=[smem] * 5, out_specs=smem,
        out_shape=jax.ShapeDtypeStruct(((n_tiles + 2) * MOE_TILE,), I32),
        name="moe_perm",
    )(pos, cstart, ccnt, cpad, n_used.reshape(1))


def _experts(n, perm, elo, ehi, flag, x1_all, w_gate, w_up, w_down, ln, layer, n_steps):
    def w_spec(shape, table_idx):
        def index_map(i, perm_r, elo_r, ehi_r, flag_r):
            return (layer, (elo_r, ehi_r)[table_idx][i], 0, 0)
        return pl.BlockSpec((1, 1) + shape, index_map)

    up = (D_MODEL, EXPERT_FF)
    down = (EXPERT_FF, D_MODEL)
    any_spec = pl.BlockSpec(memory_space=pl.ANY)
    return pl.pallas_call(
        functools.partial(_expert_kernel, n),
        grid_spec=pltpu.PrefetchScalarGridSpec(
            num_scalar_prefetch=4, grid=(n_steps,),
            in_specs=[any_spec,
                      w_spec(up, 0), w_spec(up, 0), w_spec(down, 0),
                      w_spec(up, 1), w_spec(up, 1), w_spec(down, 1),
                      pl.BlockSpec((1, 2, D_MODEL), lambda i, p, e0, e1, f: (layer, 0, 0))],
            out_specs=any_spec,
            scratch_shapes=[pltpu.VMEM((2, MOE_TILE, ROW_W), F32),
                            pltpu.VMEM((2, MOE_TILE, D_MODEL), F32),
                            pltpu.VMEM((D_MODEL, 4 * EXPERT_FF), BF16),
                            pltpu.VMEM((2 * EXPERT_FF, D_MODEL), BF16),
                            pltpu.SemaphoreType.DMA((2,)), pltpu.SemaphoreType.DMA((2,)),
                            pltpu.SemaphoreType.DMA(())]),
        out_shape=jax.ShapeDtypeStruct((n + PAD_ROWS + 2 * MOE_TILE, D_MODEL), F32),
        compiler_params=pltpu.CompilerParams(
            dimension_semantics=("arbitrary",), vmem_limit_bytes=VMEM_LIMIT),
        name="moe_experts",
    )(perm, elo, ehi, flag, x1_all, w_gate, w_up, w_down, w_gate, w_up, w_down, ln)


def _moe(x1_all, route_p, route_s, w_gate, w_up, w_down, ln2, layer):
    n = x1_all.shape[0] - PAD_ROWS
    n_tiles = n // MOE_TILE + N_CLASSES
    n_steps = n_tiles + 1
    cls2d = jnp.concatenate([route_p[0], route_s[0]]).reshape(n // LANES, LANES)
    pos2d, stats = _plan(cls2d)

    cnt = stats[0, :N_CLASSES].astype(I32)
    tiles = stats[1, :N_CLASSES].astype(I32)
    first_tile = stats[2, :N_CLASSES].astype(I32)
    end_tile = first_tile + tiles
    n_used = end_tile[N_CLASSES - 1]
    step = jnp.arange(n_steps, dtype=I32)
    tcls = jnp.sum((end_tile[None, :] <= jnp.minimum(step, n_used - 1)[:, None]).astype(I32), axis=1)
    valid = step < n_used
    changed = jnp.concatenate([jnp.ones((1,), jnp.bool_), tcls[1:] != tcls[:-1]]) & valid
    flag = valid.astype(I32) + 2 * changed.astype(I32) + 4 * (step == n_used).astype(I32)
    group = tcls // N_PAIRS
    pair = tcls % N_PAIRS
    elo = group * PER_GROUP + jnp.asarray(PAIR_LO, I32)[pair]
    ehi = group * PER_GROUP + jnp.asarray(PAIR_HI, I32)[pair]

    perm = _perm(n, pos2d.reshape(n), first_tile * MOE_TILE, cnt, tiles * MOE_TILE - cnt, n_used,
                 n_tiles)
    return _experts(n, perm, elo, ehi, flag, x1_all, w_gate, w_up, w_down, ln2, layer, n_steps)


def kernel(x_prompt, x_sample, state_hgrn, state_conv, w_in, w_out, lower_bounds, hgrn_norm_g,
           conv_w, conv_norm_g, ln1_g, ln1_b, ln2_g, ln2_b, w_router, router_bias,
           w_gate, w_up, w_down):
    batch, seq, _ = x_prompt.shape
    dec_batch, dec_seq, _ = x_sample.shape
    assert seq % PROMPT_TILE == 0 and dec_batch % SAMPLE_SEQS == 0
    assert CONV_K - 1 <= dec_seq <= SAMPLE_ROWS
    assert (batch * seq + dec_batch * SAMPLE_ROWS) % MOE_TILE == 0

    lb = jnp.cumsum(jax.nn.softmax(lower_bounds.astype(F32), axis=0), axis=0)
    lb = lb - lb[0:1]
    params = jnp.stack([jnp.log(lb), jnp.log1p(-lb), 1.0 - lb, hgrn_norm_g, conv_norm_g,
                        conv_w[:, 0], conv_w[:, 1], conv_w[:, 2]], axis=1)
    ln1 = jnp.stack([ln1_g, ln1_b], axis=1)
    ln2 = jnp.stack([ln2_g, ln2_b], axis=1)
    wr_hi = w_router.astype(BF16)
    wr_lo = (w_router - wr_hi.astype(F32)).astype(BF16)
    wr = jnp.concatenate([wr_hi.T, wr_lo.T], axis=0)
    rb = router_bias.astype(F32).reshape(N_EXPERTS, 1)

    n_p, n_s = batch * seq, dec_batch * SAMPLE_ROWS
    n = n_p + n_s
    xp = x_prompt.reshape(n_p, D_MODEL)
    xs = jnp.pad(x_sample, ((0, 0), (0, SAMPLE_ROWS - dec_seq), (0, 0))).reshape(n_s, D_MODEL)
    xs_row0 = 0

    s_p, b_p, s_s, b_s = [], [], [], []
    for l in range(DEPTH):
        x1_all, route_p, s_l, b_l = _prompt_mixer(xp, w_in, w_out, params, ln1, wr, rb, l, batch,
                                                  seq, n + PAD_ROWS)
        s_p.append(s_l)
        b_p.append(b_l)
        x1_all, route_s, s_l, b_l = _sample_mixer(xs, xs_row0, x1_all, state_hgrn, state_conv, w_in,
                                                  w_out, params, ln1, wr, rb, l, dec_batch, dec_seq)
        s_s.append(s_l)
        b_s.append(b_l)
        xp = xs = _moe(x1_all, route_p, route_s, w_gate, w_up, w_down, ln2, l)
        xs_row0 = n_p

    y_prompt = xp[:n_p].reshape(batch, seq, D_MODEL)
    y_sample = xs[n_p:n].reshape(dec_batch, SAMPLE_ROWS, D_MODEL)[:, :dec_seq]
    return (y_prompt, y_sample, jnp.stack(s_p), jnp.stack(b_p), jnp.stack(s_s), jnp.stack(b_s))
```

```python
import functools

import jax
import jax.numpy as jnp
from jax import lax
from jax.experimental import pallas as pl
from jax.experimental.pallas import tpu as pltpu

F32 = jnp.float32
BF16 = jnp.bfloat16
I32 = jnp.int32

D_MODEL = 1024
DEPTH = 2
HEADS = 4
HEAD_DIM = 128
HGRN_W = HEADS * HEAD_DIM
CONV_DIM = 512
CONV_GROUPS = 8
CONV_K = 3
N_IN = 7 * 512
N_EXPERTS = 16
N_GROUPS = 4
PER_GROUP = 4
N_PAIRS = 6
N_CLASSES = N_GROUPS * N_PAIRS
PAIR_LO = (0, 0, 0, 1, 1, 2)
PAIR_HI = (1, 2, 3, 2, 3, 3)
EXPERT_FF = 512
ALPHA = (2 * DEPTH) ** 0.25
LN_EPS = 1e-5
RMS_EPS = 1e-6
SAFE_EXPONENT = 80.0

LANES = 128
SUBLANES = 8
CHUNK = 64
PROMPT_TILE = 256
SAMPLE_ROWS = 8
SAMPLE_SEQS = 16
MOE_TILE = 256
ROW_W = D_MODEL + LANES
PAD_ROWS = N_CLASSES * MOE_TILE
VMEM_LIMIT = 56 * 1024 * 1024

OFF_Q, OFF_F, OFF_I, OFF_G, OFF_GB, OFF_GC, OFF_CX = (i * 512 for i in range(7))

P_LOGLB, P_LOG1MLB, P_OMLB, P_HNORM, P_CNORM, P_CW0, P_CW1, P_CW2 = range(8)

NT_DIMS = (((1,), (1,)), ((), ()))
TN_DIMS = (((0,), (0,)), ((), ()))


def _sigmoid(x):
    return 1.0 / (1.0 + jnp.exp(-x))


def _seg_cumsum(x, seg):
    row = lax.broadcasted_iota(I32, x.shape, 0)
    pos = row & (seg - 1)
    sh = 1
    while sh < seg:
        x = x + jnp.where(pos >= sh, pltpu.roll(x, sh, axis=0), 0.0)
        sh *= 2
    return x


def _gate_terms(z, p_ref):
    e = jnp.exp(-jnp.abs(z))
    inv = 1.0 / (1.0 + e)
    logsig = jnp.minimum(z, 0.0) - jnp.log(1.0 + e)
    a = p_ref[0, P_LOGLB:P_LOGLB + 1, :]
    b = p_ref[0, P_LOG1MLB:P_LOG1MLB + 1, :] + logsig
    logf = jnp.maximum(a, b) + jnp.log(1.0 + jnp.exp(-jnp.abs(a - b)))
    k = p_ref[0, P_OMLB:P_OMLB + 1, :] * (jnp.where(z >= 0.0, e, 1.0) * inv)
    return logf, k


def _exact_block(qs_c, b_c, k_c, v_c, states, n_valid):
    trow = lax.broadcasted_iota(I32, (SAMPLE_ROWS, 1), 0)
    b_last = b_c[n_valid - 1:n_valid, :]
    qb = (qs_c * jnp.exp(b_c)).astype(BF16)
    kend = (k_c * jnp.exp(b_last - b_c)).astype(BF16)
    dec = jnp.exp(b_last)
    outs, new_states = [], []
    for h in range(HEADS):
        sl = slice(h * HEAD_DIM, (h + 1) * HEAD_DIM)
        st = states[h]
        o_h = lax.dot_general(qb[:, sl], st.astype(BF16), NT_DIMS, preferred_element_type=F32)
        for t in range(n_valid):
            dlt = jnp.minimum(b_c[:, sl] - b_c[t:t + 1, sl], 0.0)
            a_col = jnp.sum(qs_c[:, sl] * k_c[t:t + 1, sl] * jnp.exp(dlt), axis=-1, keepdims=True)
            a_col = jnp.where(trow >= t, a_col, 0.0)
            o_h = o_h + a_col * v_c[t:t + 1, sl]
        d_st = lax.dot_general(v_c[:, sl].astype(BF16), kend[:, sl], TN_DIMS,
                               preferred_element_type=F32)
        outs.append(o_h)
        new_states.append(st * dec[:, sl] + d_st)
    return outs, new_states


def _group_rms(x, n_groups):
    width = x.shape[-1] // n_groups
    x2 = x * x
    outs = []
    for s in range(x.shape[-1] // LANES):
        xs = x[:, s * LANES:(s + 1) * LANES]
        x2s = x2[:, s * LANES:(s + 1) * LANES]
        if width == LANES:
            ms = jnp.sum(x2s, axis=-1, keepdims=True) * (1.0 / width)
            scale = lax.rsqrt(ms + RMS_EPS)
        else:
            lane = lax.broadcasted_iota(I32, xs.shape, 1)
            lo = lane < width
            ms_lo = jnp.sum(jnp.where(lo, x2s, 0.0), axis=-1, keepdims=True) * (1.0 / width)
            ms_hi = jnp.sum(jnp.where(lo, 0.0, x2s), axis=-1, keepdims=True) * (1.0 / width)
            scale = jnp.where(lo, lax.rsqrt(ms_lo + RMS_EPS), lax.rsqrt(ms_hi + RMS_EPS))
        outs.append(xs * scale)
    return jnp.concatenate(outs, axis=-1)


def _layer_norm(r, g, b):
    mu = jnp.mean(r, axis=-1, keepdims=True)
    rc = r - mu
    var = jnp.mean(rc * rc, axis=-1, keepdims=True)
    return rc * lax.rsqrt(var + LN_EPS) * g + b


def _route(x1, wr_ref, rb_ref):
    x_hi = x1.astype(BF16)
    x_lo = (x1 - x_hi.astype(F32)).astype(BF16)
    wr = wr_ref[...]
    r1 = lax.dot_general(wr, x_hi, NT_DIMS, preferred_element_type=F32)
    r2 = lax.dot_general(wr, x_lo, NT_DIMS, preferred_element_type=F32)
    lt = r1[0:N_EXPERTS] + r1[N_EXPERTS:] + r2[0:N_EXPERTS] + r2[N_EXPERTS:] + rb_ref[...]
    lg = [lt[e:e + 1, :] for e in range(N_EXPERTS)]
    mx = lg[0]
    for e in range(1, N_EXPERTS):
        mx = jnp.maximum(mx, lg[e])
    ex = [jnp.exp(l - mx) for l in lg]
    best = None
    gi = None
    for g in range(N_GROUPS):
        a, b, c, d = ex[PER_GROUP * g:PER_GROUP * (g + 1)]
        s = jnp.maximum(jnp.maximum(jnp.maximum(a + b, a + c), jnp.maximum(a + d, b + c)),
                        jnp.maximum(b + d, c + d))
        if g == 0:
            best, gi = s, jnp.zeros(s.shape, I32)
        else:
            upd = s > best
            best = jnp.where(upd, s, best)
            gi = jnp.where(upd, g, gi)
    v = []
    for i in range(PER_GROUP):
        vi = ex[3 * PER_GROUP + i]
        for g in (2, 1, 0):
            vi = jnp.where(gi == g, ex[PER_GROUP * g + i], vi)
        v.append(vi)
    w1, i1 = v[0], jnp.zeros(v[0].shape, I32)
    for i in range(1, PER_GROUP):
        upd = v[i] > w1
        w1 = jnp.where(upd, v[i], w1)
        i1 = jnp.where(upd, i, i1)
    w2, i2 = None, None
    for i in range(PER_GROUP):
        vi = jnp.where(i1 == i, -1.0, v[i])
        if i == 0:
            w2, i2 = vi, jnp.zeros(vi.shape, I32)
        else:
            upd = vi > w2
            w2 = jnp.where(upd, vi, w2)
            i2 = jnp.where(upd, i, i2)
    inv = 1.0 / (w1 + w2)
    first_lo = i1 < i2
    lo = jnp.where(first_lo, i1, i2)
    hi = jnp.where(first_lo, i2, i1)
    g_lo = jnp.where(first_lo, w1, w2) * inv
    g_hi = jnp.where(first_lo, w2, w1) * inv
    pair = jnp.where(lo == 0, 0, jnp.where(lo == 1, 3, 5)) + hi - lo - 1
    return gi * N_PAIRS + pair, g_lo, g_hi


def _post_mix(xt, o, g, yc_in, p_ref, wout_scr, ln_ref, wr_ref, rb_ref, x1_ref, route_ref):
    rows = xt.shape[0]
    o = _group_rms(o, HEADS) * p_ref[0, P_HNORM:P_HNORM + 1, :]
    o = o * (g * _sigmoid(g))
    yc = _group_rms(yc_in, CONV_GROUPS) * p_ref[0, P_CNORM:P_CNORM + 1, :]
    mix = jnp.concatenate([o, yc], axis=-1).astype(BF16)
    h = jnp.dot(mix, wout_scr[...], preferred_element_type=F32)
    x1 = _layer_norm(ALPHA * xt + h, ln_ref[0, 0:1, :], ln_ref[0, 1:2, :])
    cls, g_lo, g_hi = _route(x1, wr_ref, rb_ref)
    x1_ref[:, 0:D_MODEL] = x1
    sub = lax.broadcasted_iota(I32, (LANES, rows), 0)
    gates_t = jnp.where(sub == 0, g_lo, jnp.where(sub == 1, g_hi, 0.0))
    x1_ref[:, D_MODEL:ROW_W] = jnp.transpose(gates_t)
    sub8 = lax.broadcasted_iota(I32, (SUBLANES, rows), 0)
    route_ref[...] = jnp.where(sub8 == 0, cls.astype(F32), 0.0)


def _prompt_mixer_kernel(n_real, n_t, x_ref, *refs):
    i = pl.program_id(0)
    x1_ref = refs[6]

    @pl.when(i < n_real)
    def _():
        _prompt_tile(i, lax.rem(i, n_t), n_t, x_ref, *refs)

    @pl.when(i >= n_real)
    def _():
        x1_ref[...] = jnp.zeros_like(x1_ref)


def _prompt_tile(i, j, n_t, x_ref, win_ref, wout_ref, p_ref, ln_ref, wr_ref, rb_ref,
                 x1_ref, route_ref, s_ref, buf_ref,
                 win_scr, wout_scr, proj_scr, qs_scr, b_scr, k_scr, o_scr, yc_scr, st_scr, st0_scr,
                 ubuf_scr):
    @pl.when(i == 0)
    def _():
        win_scr[...] = win_ref[0].astype(BF16)
        wout_scr[...] = wout_ref[0].astype(BF16)

    @pl.when(j == 0)
    def _():
        st_scr[...] = jnp.zeros_like(st_scr)
        ubuf_scr[...] = jnp.zeros_like(ubuf_scr)

    xt = x_ref[...]
    proj_scr[...] = jnp.dot(xt.astype(BF16), win_scr[...], preferred_element_type=F32)

    q = proj_scr[:, OFF_Q:OFF_Q + 512]
    qs_scr[...] = q * _sigmoid(q)
    logf, k = _gate_terms(proj_scr[:, OFF_F:OFF_F + 512], p_ref)
    k_scr[...] = k
    b_scr[...] = _seg_cumsum(logf, CHUNK)

    tri = (lax.broadcasted_iota(I32, (CHUNK, CHUNK), 0)
           >= lax.broadcasted_iota(I32, (CHUNK, CHUNK), 1))

    states = [st_scr[h] for h in range(HEADS)]
    for h in range(HEADS):
        st0_scr[h] = states[h]
    worst = jnp.zeros((1, 512), F32)
    for c in range(PROMPT_TILE // CHUNK):
        rows = slice(c * CHUNK, (c + 1) * CHUNK)
        qs_c = qs_scr[rows, :]
        b_c = b_scr[rows, :]
        k_c = k_scr[rows, :]
        v_c = proj_scr[rows, OFF_I:OFF_I + 512].astype(BF16)
        b_mid = b_c[CHUNK // 2 - 1:CHUNK // 2, :]
        b_last = b_c[CHUNK - 1:CHUNK, :]
        worst = jnp.maximum(worst, jnp.maximum(-b_mid, b_mid - b_last))
        qd = (qs_c * jnp.exp(b_c - b_mid)).astype(BF16)
        kd = (k_c * jnp.exp(b_mid - b_c)).astype(BF16)
        qb = (qs_c * jnp.exp(b_c)).astype(BF16)
        kend = (k_c * jnp.exp(b_last - b_c)).astype(BF16)
        dec = jnp.exp(b_last)
        for h in range(HEADS):
            sl = slice(h * HEAD_DIM, (h + 1) * HEAD_DIM)
            att = lax.dot_general(qd[:, sl], kd[:, sl], NT_DIMS, preferred_element_type=F32)
            att = jnp.where(tri, att, 0.0).astype(BF16)
            st = states[h]
            o_h = jnp.dot(att, v_c[:, sl], preferred_element_type=F32)
            o_h = o_h + lax.dot_general(qb[:, sl], st.astype(BF16), NT_DIMS,
                                        preferred_element_type=F32)
            o_scr[rows, sl] = o_h
            d_st = lax.dot_general(v_c[:, sl], kend[:, sl], TN_DIMS, preferred_element_type=F32)
            states[h] = st * dec[:, sl] + d_st
    for h in range(HEADS):
        st_scr[h] = states[h]

    u = proj_scr[:, OFF_GC:OFF_GC + 512] * proj_scr[:, OFF_CX:OFF_CX + 512]
    row = lax.broadcasted_iota(I32, u.shape, 0)
    prev2 = ubuf_scr[SUBLANES - 2:SUBLANES - 1, :]
    prev1 = ubuf_scr[SUBLANES - 1:SUBLANES, :]
    u1 = jnp.where(row == 0, prev1, pltpu.roll(u, 1, axis=0))
    u2 = jnp.where(row == 0, prev2, jnp.where(row == 1, prev1, pltpu.roll(u, 2, axis=0)))
    y = (p_ref[0, P_CW0:P_CW0 + 1, :] * u2 + p_ref[0, P_CW1:P_CW1 + 1, :] * u1
         + p_ref[0, P_CW2:P_CW2 + 1, :] * u)
    ubuf_scr[...] = u[PROMPT_TILE - SUBLANES:, :]
    yc_scr[...] = proj_scr[:, OFF_GB:OFF_GB + 512] * y

    def finish(x_tile):
        _post_mix(x_tile, o_scr[...], proj_scr[:, OFF_G:OFF_G + 512], yc_scr[...], p_ref, wout_scr,
                  ln_ref, wr_ref, rb_ref, x1_ref, route_ref)

    finish(xt)

    @pl.when(jnp.logical_not(jnp.max(worst) <= SAFE_EXPONENT))
    def _():
        for h in range(HEADS):
            st_scr[h] = st0_scr[h]

        def block_body(blk, carry):
            r0 = pl.multiple_of(blk * SAMPLE_ROWS, SAMPLE_ROWS)
            rows = pl.ds(r0, SAMPLE_ROWS)
            b_blk = b_scr[rows, :]
            b_prev = b_scr[pl.ds(jnp.maximum(r0 - 1, 0), 1), :]
            b_prev = jnp.where((r0 & (CHUNK - 1)) == 0, 0.0, b_prev)
            outs, new_states = _exact_block(
                qs_scr[rows, :], b_blk - b_prev, k_scr[rows, :], proj_scr[rows, OFF_I:OFF_I + 512],
                [st_scr[h] for h in range(HEADS)], SAMPLE_ROWS)
            for h in range(HEADS):
                o_scr[rows, h * HEAD_DIM:(h + 1) * HEAD_DIM] = outs[h]
                st_scr[h] = new_states[h]
            return carry

        lax.fori_loop(0, PROMPT_TILE // SAMPLE_ROWS, block_body, 0)
        finish(x_ref[...])

    @pl.when(j == n_t - 1)
    def _():
        for h in range(HEADS):
            s_ref[0, h] = jnp.transpose(st_scr[h])
        buf_ref[0] = ubuf_scr[SUBLANES - (CONV_K - 1):, :]


def _sample_mixer_kernel(n_valid, x_ref, s0_ref, cbuf_ref, win_ref, wout_ref, p_ref, ln_ref,
                         wr_ref, rb_ref, x1_alias_ref,
                         x1_ref, route_ref, s_ref, buf_ref,
                         win_scr, wout_scr, proj_scr, qs_scr, b_scr, k_scr, o_scr, y_scr):
    tile = SAMPLE_SEQS * SAMPLE_ROWS

    @pl.when(pl.program_id(0) == 0)
    def _():
        win_scr[...] = win_ref[0].astype(BF16)
        wout_scr[...] = wout_ref[0].astype(BF16)

    xt = x_ref[...]
    proj_scr[...] = jnp.dot(xt.astype(BF16), win_scr[...], preferred_element_type=F32)

    q = proj_scr[:, OFF_Q:OFF_Q + 512]
    qs_scr[...] = q * _sigmoid(q)
    logf, k = _gate_terms(proj_scr[:, OFF_F:OFF_F + 512], p_ref)
    valid = (lax.broadcasted_iota(I32, (tile, 512), 0) & (SAMPLE_ROWS - 1)) < n_valid
    k_scr[...] = jnp.where(valid, k, 0.0)
    b_scr[...] = _seg_cumsum(jnp.where(valid, logf, 0.0), SAMPLE_ROWS)

    urow = lax.broadcasted_iota(I32, (SAMPLE_ROWS, 512), 0)

    def seq_body(s, carry):
        r0 = pl.multiple_of(s * SAMPLE_ROWS, SAMPLE_ROWS)
        rows = pl.ds(r0, SAMPLE_ROWS)
        qs_c = qs_scr[rows, :]
        b_c = b_scr[rows, :]
        k_c = k_scr[rows, :]
        v_c = proj_scr[rows, OFF_I:OFF_I + 512]
        states = [jnp.transpose(s0_ref[0, s, h]) for h in range(HEADS)]
        outs, states = _exact_block(qs_c, b_c, k_c, v_c, states, n_valid)
        for h in range(HEADS):
            o_scr[rows, h * HEAD_DIM:(h + 1) * HEAD_DIM] = outs[h]
            s_ref[s, h] = jnp.transpose(states[h])
        u = proj_scr[rows, OFF_GC:OFF_GC + 512] * proj_scr[rows, OFF_CX:OFF_CX + 512]
        prev2 = cbuf_ref[0, s, 0:1, :]
        prev1 = cbuf_ref[0, s, 1:2, :]
        u1 = jnp.where(urow == 0, prev1, pltpu.roll(u, 1, axis=0))
        u2 = jnp.where(urow == 0, prev2, jnp.where(urow == 1, prev1, pltpu.roll(u, 2, axis=0)))
        y_scr[rows, :] = (p_ref[0, P_CW0:P_CW0 + 1, :] * u2 + p_ref[0, P_CW1:P_CW1 + 1, :] * u1
                          + p_ref[0, P_CW2:P_CW2 + 1, :] * u)
        buf_ref[s] = u[n_valid - (CONV_K - 1):n_valid, :]
        return carry

    lax.fori_loop(0, SAMPLE_SEQS, seq_body, 0)

    yc_in = proj_scr[:, OFF_GB:OFF_GB + 512] * y_scr[...]
    _post_mix(xt, o_scr[...], proj_scr[:, OFF_G:OFF_G + 512], yc_in, p_ref, wout_scr, ln_ref,
              wr_ref, rb_ref, x1_ref, route_ref)


def _plan_kernel(cls_ref, pos_ref, stats_ref):
    n_rows = cls_ref.shape[0]
    cls = cls_ref[...]
    upper = (lax.broadcasted_iota(I32, (LANES, LANES), 0)
             < lax.broadcasted_iota(I32, (LANES, LANES), 1)).astype(BF16)
    lower = (lax.broadcasted_iota(I32, (n_rows, n_rows), 1)
             < lax.broadcasted_iota(I32, (n_rows, n_rows), 0)).astype(BF16)
    lane = lax.broadcasted_iota(I32, (n_rows, LANES), 1)
    row_tot = jnp.zeros((n_rows, LANES), F32)
    for c in range(N_CLASSES):
        oh = jnp.where(cls == c, 1.0, 0.0)
        row_tot = jnp.where(lane == c, jnp.sum(oh, axis=1, keepdims=True), row_tot)
    before = jnp.dot(lower, row_tot.astype(BF16), preferred_element_type=F32)
    cnt = jnp.sum(row_tot, axis=0, keepdims=True)
    tiles = jnp.floor((cnt + (MOE_TILE - 1)) * (1.0 / MOE_TILE))
    first_tile = jnp.dot(jnp.broadcast_to(tiles, (SUBLANES, LANES)).astype(BF16), upper,
                         preferred_element_type=F32)[0:1]
    base = before + first_tile * MOE_TILE
    pos = jnp.zeros((n_rows, LANES), F32)
    for c in range(N_CLASSES):
        oh = jnp.where(cls == c, 1.0, 0.0)
        local = jnp.dot(oh.astype(BF16), upper, preferred_element_type=F32)
        pos = pos + oh * (base[:, c:c + 1] + local)
    pos_ref[...] = pos.astype(I32)
    sub = lax.broadcasted_iota(I32, (SUBLANES, LANES), 0)
    stats_ref[...] = jnp.where(sub == 0, cnt, jnp.where(sub == 1, tiles,
                                                        jnp.where(sub == 2, first_tile, 0.0)))


def _row_copy(src_ref, src_row, dst_ref, dst_row, sem):
    return pltpu.make_async_copy(src_ref.at[pl.ds(src_row, 1), :],
                                 dst_ref.at[pl.ds(dst_row, 1), :], sem)


def _perm_kernel(n, pos_ref, cstart_ref, ccnt_ref, cpad_ref, nused_ref, perm_ref):
    def stand_in_body(r, carry):
        perm_ref[r] = n + PAD_ROWS + r
        return carry

    lax.fori_loop(0, MOE_TILE, stand_in_body, 0, unroll=8)

    def unused_body(p, carry):
        perm_ref[p] = n
        return carry

    lax.fori_loop((nused_ref[0] + 1) * MOE_TILE, perm_ref.shape[0], unused_body, 0)

    def cls_body(c, k):
        base = MOE_TILE + cstart_ref[c] + ccnt_ref[c]

        def pad_body(q, carry):
            perm_ref[base + q] = n + k + q
            return carry

        lax.fori_loop(0, cpad_ref[c], pad_body, 0)
        return k + cpad_ref[c]

    lax.fori_loop(0, N_CLASSES, cls_body, 0)

    def tok_body(t, carry):
        perm_ref[MOE_TILE + pos_ref[t]] = t
        return carry

    lax.fori_loop(0, n, tok_body, 0, unroll=8)


def _expert_kernel(n, perm_ref, elo_ref, ehi_ref, flag_ref,
                   x_ref, wg_lo, wu_lo, wd_lo, wg_hi, wu_hi, wd_hi, ln_ref, out_ref,
                   xbuf, ybuf, w1_scr, wd_scr, gsem, ssem, zsem):
    i = pl.program_id(0)
    flag = flag_ref[i]
    slot = i & 1
    other = 1 - slot

    def gather_wait(s):
        pltpu.make_async_copy(x_ref.at[pl.ds(0, MOE_TILE), :], xbuf.at[s], gsem.at[s]).wait()

    def scatter_wait(s):
        pltpu.make_async_copy(ybuf.at[s], out_ref.at[pl.ds(0, MOE_TILE), :], ssem.at[s]).wait()

    def gather_start(tile, s):
        for r in range(MOE_TILE):
            _row_copy(x_ref, perm_ref[(tile + 1) * MOE_TILE + r], xbuf.at[s], r,
                      gsem.at[s]).start(priority=r % 2)

    def scatter_start(tile, s):
        for r in range(MOE_TILE):
            _row_copy(ybuf.at[s], r, out_ref, perm_ref[(tile + 1) * MOE_TILE + r],
                      ssem.at[s]).start(priority=r % 2)

    @pl.when(i == 0)
    def _():
        ybuf[...] = jnp.zeros_like(ybuf)
        for c in range(PAD_ROWS // MOE_TILE):
            pltpu.make_async_copy(ybuf.at[0], out_ref.at[pl.ds(n + c * MOE_TILE, MOE_TILE), :],
                                  zsem).start()
        for c in range(PAD_ROWS // MOE_TILE):
            pltpu.make_async_copy(ybuf.at[0], out_ref.at[pl.ds(n + c * MOE_TILE, MOE_TILE), :],
                                  zsem).wait()
        pltpu.make_async_copy(ybuf.at[0],
                              out_ref.at[pl.ds(n + PAD_ROWS + MOE_TILE, MOE_TILE), :],
                              ssem.at[0]).start()
        gather_start(0, 0)

    @pl.when((flag & 2) != 0)
    def _():
        w1_scr[:, 0 * EXPERT_FF:1 * EXPERT_FF] = wg_lo[0, 0].astype(BF16)
        w1_scr[:, 1 * EXPERT_FF:2 * EXPERT_FF] = wu_lo[0, 0].astype(BF16)
        w1_scr[:, 2 * EXPERT_FF:3 * EXPERT_FF] = wg_hi[0, 0].astype(BF16)
        w1_scr[:, 3 * EXPERT_FF:4 * EXPERT_FF] = wu_hi[0, 0].astype(BF16)
        wd_scr[0:EXPERT_FF, :] = wd_lo[0, 0].astype(BF16)
        wd_scr[EXPERT_FF:, :] = wd_hi[0, 0].astype(BF16)

    @pl.when((flag & 1) != 0)
    def _():
        gather_wait(slot)
        scatter_wait(slot)
        gather_start(i + 1, other)
        scatter_start(i - 1, other)
        xs = xbuf[slot]
        x = xs[:, 0:D_MODEL]
        g_lo = xs[:, D_MODEL:D_MODEL + 1]
        g_hi = xs[:, D_MODEL + 1:D_MODEL + 2]
        hc = jnp.dot(x.astype(BF16), w1_scr[...], preferred_element_type=F32)
        hg_lo, hu_lo = hc[:, 0:EXPERT_FF], hc[:, EXPERT_FF:2 * EXPERT_FF]
        hg_hi, hu_hi = hc[:, 2 * EXPERT_FF:3 * EXPERT_FF], hc[:, 3 * EXPERT_FF:]
        h_lo = hg_lo * _sigmoid(hg_lo) * hu_lo * g_lo
        h_hi = hg_hi * _sigmoid(hg_hi) * hu_hi * g_hi
        h = jnp.concatenate([h_lo, h_hi], axis=-1).astype(BF16)
        y = jnp.dot(h, wd_scr[...], preferred_element_type=F32)
        ybuf[slot] = _layer_norm(ALPHA * x + y, ln_ref[0, 0:1, :], ln_ref[0, 1:2, :])

    @pl.when((flag & 4) != 0)
    def _():
        scatter_start(i - 1, other)
        scatter_wait(slot)
        scatter_wait(other)
        gather_wait(slot)


def _layer_spec(shape, layer, n_grid, single_buffer=False):
    idx = (layer,) + (0,) * len(shape)
    if n_grid == 1:
        index_map = lambda i: idx
    else:
        index_map = lambda i, j: idx
    if single_buffer:
        return pl.BlockSpec((1,) + shape, index_map, pipeline_mode=pl.Buffered(1))
    return pl.BlockSpec((1,) + shape, index_map)


def _prompt_mixer(x, w_in, w_out, params, ln, wr, rb, layer, batch, seq, n_rows_out):
    n_t = seq // PROMPT_TILE
    n_real = batch * n_t
    last = n_real - 1
    row_in = pl.BlockSpec((PROMPT_TILE, D_MODEL), lambda i: (jnp.minimum(i, last), 0))
    row_out = pl.BlockSpec((PROMPT_TILE, ROW_W), lambda i: (i, 0))
    route_out = pl.BlockSpec((SUBLANES, PROMPT_TILE), lambda i: (0, jnp.minimum(i, last)))
    full1 = lambda shape: pl.BlockSpec(shape, lambda i: (0, 0))
    seq_of = lambda i: jnp.minimum(i, last) // n_t
    return pl.pallas_call(
        functools.partial(_prompt_mixer_kernel, n_real, n_t),
        grid=(n_rows_out // PROMPT_TILE,),
        in_specs=[row_in,
                  _layer_spec((D_MODEL, N_IN), layer, 1, True),
                  _layer_spec((D_MODEL, D_MODEL), layer, 1, True),
                  _layer_spec((8, 512), layer, 1), _layer_spec((2, D_MODEL), layer, 1),
                  full1((2 * N_EXPERTS, D_MODEL)), full1((N_EXPERTS, 1))],
        out_specs=[row_out, route_out,
                   pl.BlockSpec((1, HEADS, HEAD_DIM, HEAD_DIM), lambda i: (seq_of(i), 0, 0, 0)),
                   pl.BlockSpec((1, CONV_K - 1, CONV_DIM), lambda i: (seq_of(i), 0, 0))],
        out_shape=[jax.ShapeDtypeStruct((n_rows_out, ROW_W), F32),
                   jax.ShapeDtypeStruct((SUBLANES, batch * seq), F32),
                   jax.ShapeDtypeStruct((batch, HEADS, HEAD_DIM, HEAD_DIM), F32),
                   jax.ShapeDtypeStruct((batch, CONV_K - 1, CONV_DIM), F32)],
        scratch_shapes=[pltpu.VMEM((D_MODEL, N_IN), BF16), pltpu.VMEM((D_MODEL, D_MODEL), BF16),
                        pltpu.VMEM((PROMPT_TILE, N_IN), F32)]
        + [pltpu.VMEM((PROMPT_TILE, 512), F32)] * 5
        + [pltpu.VMEM((HEADS, HEAD_DIM, HEAD_DIM), F32)] * 2
        + [pltpu.VMEM((SUBLANES, CONV_DIM), F32)],
        compiler_params=pltpu.CompilerParams(
            dimension_semantics=("arbitrary",), vmem_limit_bytes=VMEM_LIMIT),
        name="prompt_mixer",
    )(x, w_in, w_out, params, ln, wr, rb)


def _sample_mixer(x, x_row0, x1_all, s0, cbuf, w_in, w_out, params, ln, wr, rb, layer, batch,
                  n_valid):
    tile = SAMPLE_SEQS * SAMPLE_ROWS
    n = batch * SAMPLE_ROWS
    x1_row0 = x1_all.shape[0] - PAD_ROWS - n
    in_blk0, out_blk0 = x_row0 // tile, x1_row0 // tile
    row_in = pl.BlockSpec((tile, D_MODEL), lambda i: (in_blk0 + i, 0))
    row_out = pl.BlockSpec((tile, ROW_W), lambda i: (out_blk0 + i, 0))
    route_out = pl.BlockSpec((SUBLANES, tile), lambda i: (0, i))
    st_spec = pl.BlockSpec((1, SAMPLE_SEQS, HEADS, HEAD_DIM, HEAD_DIM),
                           lambda i: (layer, i, 0, 0, 0))
    cb_spec = pl.BlockSpec((1, SAMPLE_SEQS, CONV_K - 1, CONV_DIM), lambda i: (layer, i, 0, 0))
    st_out = pl.BlockSpec((SAMPLE_SEQS, HEADS, HEAD_DIM, HEAD_DIM), lambda i: (i, 0, 0, 0))
    cb_out = pl.BlockSpec((SAMPLE_SEQS, CONV_K - 1, CONV_DIM), lambda i: (i, 0, 0))
    full1 = lambda shape: pl.BlockSpec(shape, lambda i: (0, 0))
    return pl.pallas_call(
        functools.partial(_sample_mixer_kernel, n_valid),
        grid=(batch // SAMPLE_SEQS,),
        in_specs=[row_in, st_spec, cb_spec,
                  _layer_spec((D_MODEL, N_IN), layer, 1, True),
                  _layer_spec((D_MODEL, D_MODEL), layer, 1, True),
                  _layer_spec((8, 512), layer, 1), _layer_spec((2, D_MODEL), layer, 1),
                  full1((2 * N_EXPERTS, D_MODEL)), full1((N_EXPERTS, 1)),
                  pl.BlockSpec(memory_space=pl.ANY)],
        out_specs=[row_out, route_out, st_out, cb_out],
        input_output_aliases={9: 0},
        out_shape=[jax.ShapeDtypeStruct(x1_all.shape, F32),
                   jax.ShapeDtypeStruct((SUBLANES, n), F32),
                   jax.ShapeDtypeStruct((batch, HEADS, HEAD_DIM, HEAD_DIM), F32),
                   jax.ShapeDtypeStruct((batch, CONV_K - 1, CONV_DIM), F32)],
        scratch_shapes=[pltpu.VMEM((D_MODEL, N_IN), BF16), pltpu.VMEM((D_MODEL, D_MODEL), BF16),
                        pltpu.VMEM((tile, N_IN), F32)] + [pltpu.VMEM((tile, 512), F32)] * 5,
        compiler_params=pltpu.CompilerParams(
            dimension_semantics=("arbitrary",), vmem_limit_bytes=VMEM_LIMIT),
        name="sample_mixer",
    )(x, s0, cbuf, w_in, w_out, params, ln, wr, rb, x1_all)


def _plan(cls2d):
    n_rows = cls2d.shape[0]
    return pl.pallas_call(
        _plan_kernel,
        out_shape=[jax.ShapeDtypeStruct((n_rows, LANES), I32),
                   jax.ShapeDtypeStruct((SUBLANES, LANES), F32)],
        name="moe_plan",
    )(cls2d)


def _perm(n, pos, cstart, ccnt, cpad, n_used, n_tiles):
    smem = pl.BlockSpec(memory_space=pltpu.SMEM)
    return pl.pallas_call(
        functools.partial(_perm_kernel, n),
        in_specs=[smem] * 5, out_specs=smem,
        out_shape=jax.ShapeDtypeStruct(((n_tiles + 2) * MOE_TILE,), I32),
        name="moe_perm",
    )(pos, cstart, ccnt, cpad, n_used.reshape(1))


def _experts(n, perm, elo, ehi, flag, x1_all, w_gate, w_up, w_down, ln, layer, n_steps):
    def w_spec(shape, table_idx):
        def index_map(i, perm_r, elo_r, ehi_r, flag_r):
            return (layer, (elo_r, ehi_r)[table_idx][i], 0, 0)
        return pl.BlockSpec((1, 1) + shape, index_map)

    up = (D_MODEL, EXPERT_FF)
    down = (EXPERT_FF, D_MODEL)
    any_spec = pl.BlockSpec(memory_space=pl.ANY)
    return pl.pallas_call(
        functools.partial(_expert_kernel, n),
        grid_spec=pltpu.PrefetchScalarGridSpec(
            num_scalar_prefetch=4, grid=(n_steps,),
            in_specs=[any_spec,
                      w_spec(up, 0), w_spec(up, 0), w_spec(down, 0),
                      w_spec(up, 1), w_spec(up, 1), w_spec(down, 1),
                      pl.BlockSpec((1, 2, D_MODEL), lambda i, p, e0, e1, f: (layer, 0, 0))],
            out_specs=any_spec,
            scratch_shapes=[pltpu.VMEM((2, MOE_TILE, ROW_W), F32),
                            pltpu.VMEM((2, MOE_TILE, D_MODEL), F32),
                            pltpu.VMEM((D_MODEL, 4 * EXPERT_FF), BF16),
                            pltpu.VMEM((2 * EXPERT_FF, D_MODEL), BF16),
                            pltpu.SemaphoreType.DMA((2,)), pltpu.SemaphoreType.DMA((2,)),
                            pltpu.SemaphoreType.DMA(())]),
        out_shape=jax.ShapeDtypeStruct((n + PAD_ROWS + 2 * MOE_TILE, D_MODEL), F32),
        compiler_params=pltpu.CompilerParams(
            dimension_semantics=("arbitrary",), vmem_limit_bytes=VMEM_LIMIT),
        name="moe_experts",
    )(perm, elo, ehi, flag, x1_all, w_gate, w_up, w_down, w_gate, w_up, w_down, ln)


def _moe(x1_all, route_p, route_s, w_gate, w_up, w_down, ln2, layer):
    n = x1_all.shape[0] - PAD_ROWS
    n_tiles = n // MOE_TILE + N_CLASSES
    n_steps = n_tiles + 1
    cls2d = jnp.concatenate([route_p[0], route_s[0]]).reshape(n // LANES, LANES)
    pos2d, stats = _plan(cls2d)

    cnt = stats[0, :N_CLASSES].astype(I32)
    tiles = stats[1, :N_CLASSES].astype(I32)
    first_tile = stats[2, :N_CLASSES].astype(I32)
    end_tile = first_tile + tiles
    n_used = end_tile[N_CLASSES - 1]
    step = jnp.arange(n_steps, dtype=I32)
    tcls = jnp.sum((end_tile[None, :] <= jnp.minimum(step, n_used - 1)[:, None]).astype(I32), axis=1)
    valid = step < n_used
    changed = jnp.concatenate([jnp.ones((1,), jnp.bool_), tcls[1:] != tcls[:-1]]) & valid
    flag = valid.astype(I32) + 2 * changed.astype(I32) + 4 * (step == n_used).astype(I32)
    group = tcls // N_PAIRS
    pair = tcls % N_PAIRS
    elo = group * PER_GROUP + jnp.asarray(PAIR_LO, I32)[pair]
    ehi = group * PER_GROUP + jnp.asarray(PAIR_HI, I32)[pair]

    perm = _perm(n, pos2d.reshape(n), first_tile * MOE_TILE, cnt, tiles * MOE_TILE - cnt, n_used,
                 n_tiles)
    return _experts(n, perm, elo, ehi, flag, x1_all, w_gate, w_up, w_down, ln2, layer, n_steps)


def kernel(x_prompt, x_sample, state_hgrn, state_conv, w_in, w_out, lower_bounds, hgrn_norm_g,
           conv_w, conv_norm_g, ln1_g, ln1_b, ln2_g, ln2_b, w_router, router_bias,
           w_gate, w_up, w_down):
    batch, seq, _ = x_prompt.shape
    dec_batch, dec_seq, _ = x_sample.shape
    assert seq % PROMPT_TILE == 0 and dec_batch % SAMPLE_SEQS == 0
    assert CONV_K - 1 <= dec_seq <= SAMPLE_ROWS
    assert (batch * seq + dec_batch * SAMPLE_ROWS) % MOE_TILE == 0

    lb = jnp.cumsum(jax.nn.softmax(lower_bounds.astype(F32), axis=0), axis=0)
    lb = lb - lb[0:1]
    params = jnp.stack([jnp.log(lb), jnp.log1p(-lb), 1.0 - lb, hgrn_norm_g, conv_norm_g,
                        conv_w[:, 0], conv_w[:, 1], conv_w[:, 2]], axis=1)
    ln1 = jnp.stack([ln1_g, ln1_b], axis=1)
    ln2 = jnp.stack([ln2_g, ln2_b], axis=1)
    wr_hi = w_router.astype(BF16)
    wr_lo = (w_router - wr_hi.astype(F32)).astype(BF16)
    wr = jnp.concatenate([wr_hi.T, wr_lo.T], axis=0)
    rb = router_bias.astype(F32).reshape(N_EXPERTS, 1)

    n_p, n_s = batch * seq, dec_batch * SAMPLE_ROWS
    n = n_p + n_s
    xp = x_prompt.reshape(n_p, D_MODEL)
    xs = jnp.pad(x_sample, ((0, 0), (0, SAMPLE_ROWS - dec_seq), (0, 0))).reshape(n_s, D_MODEL)
    xs_row0 = 0

    s_p, b_p, s_s, b_s = [], [], [], []
    for l in range(DEPTH):
        x1_all, route_p, s_l, b_l = _prompt_mixer(xp, w_in, w_out, params, ln1, wr, rb, l, batch,
                                                  seq, n + PAD_ROWS)
        s_p.append(s_l)
        b_p.append(b_l)
        x1_all, route_s, s_l, b_l = _sample_mixer(xs, xs_row0, x1_all, state_hgrn, state_conv, w_in,
                                                  w_out, params, ln1, wr, rb, l, dec_batch, dec_seq)
        s_s.append(s_l)
        b_s.append(b_l)
        xp = xs = _moe(x1_all, route_p, route_s, w_gate, w_up, w_down, ln2, l)
        xs_row0 = n_p

    y_prompt = xp[:n_p].reshape(batch, seq, D_MODEL)
    y_sample = xs[n_p:n].reshape(dec_batch, SAMPLE_ROWS, D_MODEL)[:, :dec_seq]
    return (y_prompt, y_sample, jnp.stack(s_p), jnp.stack(b_p), jnp.stack(s_s), jnp.stack(b_s))
```

```python
import functools

import jax
import jax.numpy as jnp
from jax import lax
from jax.experimental import pallas as pl
from jax.experimental.pallas import tpu as pltpu

F32 = jnp.float32
BF16 = jnp.bfloat16
I32 = jnp.int32

D_MODEL = 1024
DEPTH = 2
HEADS = 4
HEAD_DIM = 128
HGRN_W = HEADS * HEAD_DIM
CONV_DIM = 512
CONV_GROUPS = 8
CONV_K = 3
N_IN = 7 * 512
N_EXPERTS = 16
N_GROUPS = 4
PER_GROUP = 4
N_PAIRS = 6
N_CLASSES = N_GROUPS * N_PAIRS
PAIR_LO = (0, 0, 0, 1, 1, 2)
PAIR_HI = (1, 2, 3, 2, 3, 3)
EXPERT_FF = 512
ALPHA = (2 * DEPTH) ** 0.25
LN_EPS = 1e-5
RMS_EPS = 1e-6
SAFE_EXPONENT = 80.0

LANES = 128
SUBLANES = 8
CHUNK = 64
PROMPT_TILE = 256
SAMPLE_ROWS = 8
SAMPLE_SEQS = 16
MOE_TILE = 256
ROW_W = D_MODEL + LANES
PAD_ROWS = N_CLASSES * MOE_TILE
PIPE = 3
VMEM_LIMIT = 56 * 1024 * 1024

OFF_Q, OFF_F, OFF_I, OFF_G, OFF_GB, OFF_GC, OFF_CX = (i * 512 for i in range(7))

P_LOGLB, P_LOG1MLB, P_OMLB, P_HNORM, P_CNORM, P_CW0, P_CW1, P_CW2 = range(8)

NT_DIMS = (((1,), (1,)), ((), ()))
TN_DIMS = (((0,), (0,)), ((), ()))


def _sigmoid(x):
    return 1.0 / (1.0 + jnp.exp(-x))


def _seg_cumsum(x, seg):
    row = lax.broadcasted_iota(I32, x.shape, 0)
    pos = row & (seg - 1)
    sh = 1
    while sh < seg:
        x = x + jnp.where(pos >= sh, pltpu.roll(x, sh, axis=0), 0.0)
        sh *= 2
    return x


def _gate_terms(z, p_ref):
    e = jnp.exp(-jnp.abs(z))
    inv = 1.0 / (1.0 + e)
    logsig = jnp.minimum(z, 0.0) - jnp.log(1.0 + e)
    a = p_ref[0, P_LOGLB:P_LOGLB + 1, :]
    b = p_ref[0, P_LOG1MLB:P_LOG1MLB + 1, :] + logsig
    logf = jnp.maximum(a, b) + jnp.log(1.0 + jnp.exp(-jnp.abs(a - b)))
    k = p_ref[0, P_OMLB:P_OMLB + 1, :] * (jnp.where(z >= 0.0, e, 1.0) * inv)
    return logf, k


def _exact_block(qs_c, b_c, k_c, v_c, states, n_valid):
    trow = lax.broadcasted_iota(I32, (SAMPLE_ROWS, 1), 0)
    b_last = b_c[n_valid - 1:n_valid, :]
    qb = (qs_c * jnp.exp(b_c)).astype(BF16)
    kend = (k_c * jnp.exp(b_last - b_c)).astype(BF16)
    dec = jnp.exp(b_last)
    outs, new_states = [], []
    for h in range(HEADS):
        sl = slice(h * HEAD_DIM, (h + 1) * HEAD_DIM)
        st = states[h]
        o_h = lax.dot_general(qb[:, sl], st.astype(BF16), NT_DIMS, preferred_element_type=F32)
        for t in range(n_valid):
            dlt = jnp.minimum(b_c[:, sl] - b_c[t:t + 1, sl], 0.0)
            a_col = jnp.sum(qs_c[:, sl] * k_c[t:t + 1, sl] * jnp.exp(dlt), axis=-1, keepdims=True)
            a_col = jnp.where(trow >= t, a_col, 0.0)
            o_h = o_h + a_col * v_c[t:t + 1, sl]
        d_st = lax.dot_general(v_c[:, sl].astype(BF16), kend[:, sl], TN_DIMS,
                               preferred_element_type=F32)
        outs.append(o_h)
        new_states.append(st * dec[:, sl] + d_st)
    return outs, new_states


def _group_rms(x, n_groups):
    width = x.shape[-1] // n_groups
    x2 = x * x
    outs = []
    for s in range(x.shape[-1] // LANES):
        xs = x[:, s * LANES:(s + 1) * LANES]
        x2s = x2[:, s * LANES:(s + 1) * LANES]
        if width == LANES:
            ms = jnp.sum(x2s, axis=-1, keepdims=True) * (1.0 / width)
            scale = lax.rsqrt(ms + RMS_EPS)
        else:
            lane = lax.broadcasted_iota(I32, xs.shape, 1)
            lo = lane < width
            ms_lo = jnp.sum(jnp.where(lo, x2s, 0.0), axis=-1, keepdims=True) * (1.0 / width)
            ms_hi = jnp.sum(jnp.where(lo, 0.0, x2s), axis=-1, keepdims=True) * (1.0 / width)
            scale = jnp.where(lo, lax.rsqrt(ms_lo + RMS_EPS), lax.rsqrt(ms_hi + RMS_EPS))
        outs.append(xs * scale)
    return jnp.concatenate(outs, axis=-1)


def _layer_norm(r, g, b):
    mu = jnp.mean(r, axis=-1, keepdims=True)
    rc = r - mu
    var = jnp.mean(rc * rc, axis=-1, keepdims=True)
    return rc * lax.rsqrt(var + LN_EPS) * g + b


def _route(x1, wr_ref, rb_ref):
    x_hi = x1.astype(BF16)
    x_lo = (x1 - x_hi.astype(F32)).astype(BF16)
    wr = wr_ref[...]
    r1 = lax.dot_general(wr, x_hi, NT_DIMS, preferred_element_type=F32)
    r2 = lax.dot_general(wr, x_lo, NT_DIMS, preferred_element_type=F32)
    lt = r1[0:N_EXPERTS] + r1[N_EXPERTS:] + r2[0:N_EXPERTS] + r2[N_EXPERTS:] + rb_ref[...]
    lg = [lt[e:e + 1, :] for e in range(N_EXPERTS)]
    mx = lg[0]
    for e in range(1, N_EXPERTS):
        mx = jnp.maximum(mx, lg[e])
    ex = [jnp.exp(l - mx) for l in lg]
    best = None
    gi = None
    for g in range(N_GROUPS):
        a, b, c, d = ex[PER_GROUP * g:PER_GROUP * (g + 1)]
        s = jnp.maximum(jnp.maximum(jnp.maximum(a + b, a + c), jnp.maximum(a + d, b + c)),
                        jnp.maximum(b + d, c + d))
        if g == 0:
            best, gi = s, jnp.zeros(s.shape, I32)
        else:
            upd = s > best
            best = jnp.where(upd, s, best)
            gi = jnp.where(upd, g, gi)
    v = []
    for i in range(PER_GROUP):
        vi = ex[3 * PER_GROUP + i]
        for g in (2, 1, 0):
            vi = jnp.where(gi == g, ex[PER_GROUP * g + i], vi)
        v.append(vi)
    w1, i1 = v[0], jnp.zeros(v[0].shape, I32)
    for i in range(1, PER_GROUP):
        upd = v[i] > w1
        w1 = jnp.where(upd, v[i], w1)
        i1 = jnp.where(upd, i, i1)
    w2, i2 = None, None
    for i in range(PER_GROUP):
        vi = jnp.where(i1 == i, -1.0, v[i])
        if i == 0:
            w2, i2 = vi, jnp.zeros(vi.shape, I32)
        else:
            upd = vi > w2
            w2 = jnp.where(upd, vi, w2)
            i2 = jnp.where(upd, i, i2)
    inv = 1.0 / (w1 + w2)
    first_lo = i1 < i2
    lo = jnp.where(first_lo, i1, i2)
    hi = jnp.where(first_lo, i2, i1)
    g_lo = jnp.where(first_lo, w1, w2) * inv
    g_hi = jnp.where(first_lo, w2, w1) * inv
    pair = jnp.where(lo == 0, 0, jnp.where(lo == 1, 3, 5)) + hi - lo - 1
    return gi * N_PAIRS + pair, g_lo, g_hi


def _post_mix(xt, o, g, yc_in, p_ref, wout_scr, ln_ref, wr_ref, rb_ref, x1_ref, route_ref):
    rows = xt.shape[0]
    o = _group_rms(o, HEADS) * p_ref[0, P_HNORM:P_HNORM + 1, :]
    o = o * (g * _sigmoid(g))
    yc = _group_rms(yc_in, CONV_GROUPS) * p_ref[0, P_CNORM:P_CNORM + 1, :]
    mix = jnp.concatenate([o, yc], axis=-1).astype(BF16)
    h = jnp.dot(mix, wout_scr[...], preferred_element_type=F32)
    x1 = _layer_norm(ALPHA * xt + h, ln_ref[0, 0:1, :], ln_ref[0, 1:2, :])
    cls, g_lo, g_hi = _route(x1, wr_ref, rb_ref)
    x1_ref[:, 0:D_MODEL] = x1
    sub = lax.broadcasted_iota(I32, (LANES, rows), 0)
    gates_t = jnp.where(sub == 0, g_lo, jnp.where(sub == 1, g_hi, 0.0))
    x1_ref[:, D_MODEL:ROW_W] = jnp.transpose(gates_t)
    sub8 = lax.broadcasted_iota(I32, (SUBLANES, rows), 0)
    route_ref[...] = jnp.where(sub8 == 0, cls.astype(F32), 0.0)


def _prompt_mixer_kernel(n_real, n_t, x_ref, *refs):
    i = pl.program_id(0)
    x1_ref = refs[6]

    @pl.when(i < n_real)
    def _():
        _prompt_tile(i, lax.rem(i, n_t), n_t, x_ref, *refs)

    @pl.when(i >= n_real)
    def _():
        x1_ref[...] = jnp.zeros_like(x1_ref)


def _prompt_tile(i, j, n_t, x_ref, win_ref, wout_ref, p_ref, ln_ref, wr_ref, rb_ref,
                 x1_ref, route_ref, s_ref, buf_ref,
                 win_scr, wout_scr, proj_scr, qs_scr, b_scr, k_scr, o_scr, yc_scr, st_scr, st0_scr,
                 ubuf_scr):
    @pl.when(i == 0)
    def _():
        win_scr[...] = win_ref[0].astype(BF16)
        wout_scr[...] = wout_ref[0].astype(BF16)

    @pl.when(j == 0)
    def _():
        st_scr[...] = jnp.zeros_like(st_scr)
        ubuf_scr[...] = jnp.zeros_like(ubuf_scr)

    xt = x_ref[...]
    proj_scr[...] = jnp.dot(xt.astype(BF16), win_scr[...], preferred_element_type=F32)

    q = proj_scr[:, OFF_Q:OFF_Q + 512]
    qs_scr[...] = q * _sigmoid(q)
    logf, k = _gate_terms(proj_scr[:, OFF_F:OFF_F + 512], p_ref)
    k_scr[...] = k
    b_scr[...] = _seg_cumsum(logf, CHUNK)

    tri = (lax.broadcasted_iota(I32, (CHUNK, CHUNK), 0)
           >= lax.broadcasted_iota(I32, (CHUNK, CHUNK), 1))

    states = [st_scr[h] for h in range(HEADS)]
    for h in range(HEADS):
        st0_scr[h] = states[h]
    worst = jnp.zeros((1, 512), F32)
    for c in range(PROMPT_TILE // CHUNK):
        rows = slice(c * CHUNK, (c + 1) * CHUNK)
        qs_c = qs_scr[rows, :]
        b_c = b_scr[rows, :]
        k_c = k_scr[rows, :]
        v_c = proj_scr[rows, OFF_I:OFF_I + 512].astype(BF16)
        b_mid = b_c[CHUNK // 2 - 1:CHUNK // 2, :]
        b_last = b_c[CHUNK - 1:CHUNK, :]
        worst = jnp.maximum(worst, jnp.maximum(-b_mid, b_mid - b_last))
        qd = (qs_c * jnp.exp(b_c - b_mid)).astype(BF16)
        kd = (k_c * jnp.exp(b_mid - b_c)).astype(BF16)
        qb = (qs_c * jnp.exp(b_c)).astype(BF16)
        kend = (k_c * jnp.exp(b_last - b_c)).astype(BF16)
        dec = jnp.exp(b_last)
        for h in range(HEADS):
            sl = slice(h * HEAD_DIM, (h + 1) * HEAD_DIM)
            att = lax.dot_general(qd[:, sl], kd[:, sl], NT_DIMS, preferred_element_type=F32)
            att = jnp.where(tri, att, 0.0).astype(BF16)
            st = states[h]
            o_h = jnp.dot(att, v_c[:, sl], preferred_element_type=F32)
            o_h = o_h + lax.dot_general(qb[:, sl], st.astype(BF16), NT_DIMS,
                                        preferred_element_type=F32)
            o_scr[rows, sl] = o_h
            d_st = lax.dot_general(v_c[:, sl], kend[:, sl], TN_DIMS, preferred_element_type=F32)
            states[h] = st * dec[:, sl] + d_st
    for h in range(HEADS):
        st_scr[h] = states[h]

    u = proj_scr[:, OFF_GC:OFF_GC + 512] * proj_scr[:, OFF_CX:OFF_CX + 512]
    row = lax.broadcasted_iota(I32, u.shape, 0)
    prev2 = ubuf_scr[SUBLANES - 2:SUBLANES - 1, :]
    prev1 = ubuf_scr[SUBLANES - 1:SUBLANES, :]
    u1 = jnp.where(row == 0, prev1, pltpu.roll(u, 1, axis=0))
    u2 = jnp.where(row == 0, prev2, jnp.where(row == 1, prev1, pltpu.roll(u, 2, axis=0)))
    y = (p_ref[0, P_CW0:P_CW0 + 1, :] * u2 + p_ref[0, P_CW1:P_CW1 + 1, :] * u1
         + p_ref[0, P_CW2:P_CW2 + 1, :] * u)
    ubuf_scr[...] = u[PROMPT_TILE - SUBLANES:, :]
    yc_scr[...] = proj_scr[:, OFF_GB:OFF_GB + 512] * y

    def finish(x_tile):
        _post_mix(x_tile, o_scr[...], proj_scr[:, OFF_G:OFF_G + 512], yc_scr[...], p_ref, wout_scr,
                  ln_ref, wr_ref, rb_ref, x1_ref, route_ref)

    finish(xt)

    @pl.when(jnp.logical_not(jnp.max(worst) <= SAFE_EXPONENT))
    def _():
        for h in range(HEADS):
            st_scr[h] = st0_scr[h]

        def block_body(blk, carry):
            r0 = pl.multiple_of(blk * SAMPLE_ROWS, SAMPLE_ROWS)
            rows = pl.ds(r0, SAMPLE_ROWS)
            b_blk = b_scr[rows, :]
            b_prev = b_scr[pl.ds(jnp.maximum(r0 - 1, 0), 1), :]
            b_prev = jnp.where((r0 & (CHUNK - 1)) == 0, 0.0, b_prev)
            outs, new_states = _exact_block(
                qs_scr[rows, :], b_blk - b_prev, k_scr[rows, :], proj_scr[rows, OFF_I:OFF_I + 512],
                [st_scr[h] for h in range(HEADS)], SAMPLE_ROWS)
            for h in range(HEADS):
                o_scr[rows, h * HEAD_DIM:(h + 1) * HEAD_DIM] = outs[h]
                st_scr[h] = new_states[h]
            return carry

        lax.fori_loop(0, PROMPT_TILE // SAMPLE_ROWS, block_body, 0)
        finish(x_ref[...])

    @pl.when(j == n_t - 1)
    def _():
        for h in range(HEADS):
            s_ref[0, h] = jnp.transpose(st_scr[h])
        buf_ref[0] = ubuf_scr[SUBLANES - (CONV_K - 1):, :]


def _sample_mixer_kernel(n_valid, x_ref, s0_ref, cbuf_ref, win_ref, wout_ref, p_ref, ln_ref,
                         wr_ref, rb_ref, x1_alias_ref,
                         x1_ref, route_ref, s_ref, buf_ref,
                         win_scr, wout_scr, proj_scr, qs_scr, b_scr, k_scr, o_scr, y_scr):
    tile = SAMPLE_SEQS * SAMPLE_ROWS

    @pl.when(pl.program_id(0) == 0)
    def _():
        win_scr[...] = win_ref[0].astype(BF16)
        wout_scr[...] = wout_ref[0].astype(BF16)

    xt = x_ref[...]
    proj_scr[...] = jnp.dot(xt.astype(BF16), win_scr[...], preferred_element_type=F32)

    q = proj_scr[:, OFF_Q:OFF_Q + 512]
    qs_scr[...] = q * _sigmoid(q)
    logf, k = _gate_terms(proj_scr[:, OFF_F:OFF_F + 512], p_ref)
    valid = (lax.broadcasted_iota(I32, (tile, 512), 0) & (SAMPLE_ROWS - 1)) < n_valid
    k_scr[...] = jnp.where(valid, k, 0.0)
    b_scr[...] = _seg_cumsum(jnp.where(valid, logf, 0.0), SAMPLE_ROWS)

    urow = lax.broadcasted_iota(I32, (SAMPLE_ROWS, 512), 0)

    def seq_body(s, carry):
        r0 = pl.multiple_of(s * SAMPLE_ROWS, SAMPLE_ROWS)
        rows = pl.ds(r0, SAMPLE_ROWS)
        qs_c = qs_scr[rows, :]
        b_c = b_scr[rows, :]
        k_c = k_scr[rows, :]
        v_c = proj_scr[rows, OFF_I:OFF_I + 512]
        states = [jnp.transpose(s0_ref[0, s, h]) for h in range(HEADS)]
        outs, states = _exact_block(qs_c, b_c, k_c, v_c, states, n_valid)
        for h in range(HEADS):
            o_scr[rows, h * HEAD_DIM:(h + 1) * HEAD_DIM] = outs[h]
            s_ref[s, h] = jnp.transpose(states[h])
        u = proj_scr[rows, OFF_GC:OFF_GC + 512] * proj_scr[rows, OFF_CX:OFF_CX + 512]
        prev2 = cbuf_ref[0, s, 0:1, :]
        prev1 = cbuf_ref[0, s, 1:2, :]
        u1 = jnp.where(urow == 0, prev1, pltpu.roll(u, 1, axis=0))
        u2 = jnp.where(urow == 0, prev2, jnp.where(urow == 1, prev1, pltpu.roll(u, 2, axis=0)))
        y_scr[rows, :] = (p_ref[0, P_CW0:P_CW0 + 1, :] * u2 + p_ref[0, P_CW1:P_CW1 + 1, :] * u1
                          + p_ref[0, P_CW2:P_CW2 + 1, :] * u)
        buf_ref[s] = u[n_valid - (CONV_K - 1):n_valid, :]
        return carry

    lax.fori_loop(0, SAMPLE_SEQS, seq_body, 0)

    yc_in = proj_scr[:, OFF_GB:OFF_GB + 512] * y_scr[...]
    _post_mix(xt, o_scr[...], proj_scr[:, OFF_G:OFF_G + 512], yc_in, p_ref, wout_scr, ln_ref,
              wr_ref, rb_ref, x1_ref, route_ref)


def _plan_kernel(cls_ref, pos_ref, stats_ref):
    n_rows = cls_ref.shape[0]
    cls = cls_ref[...]
    upper = (lax.broadcasted_iota(I32, (LANES, LANES), 0)
             < lax.broadcasted_iota(I32, (LANES, LANES), 1)).astype(BF16)
    lower = (lax.broadcasted_iota(I32, (n_rows, n_rows), 1)
             < lax.broadcasted_iota(I32, (n_rows, n_rows), 0)).astype(BF16)
    lane = lax.broadcasted_iota(I32, (n_rows, LANES), 1)
    row_tot = jnp.zeros((n_rows, LANES), F32)
    for c in range(N_CLASSES):
        oh = jnp.where(cls == c, 1.0, 0.0)
        row_tot = jnp.where(lane == c, jnp.sum(oh, axis=1, keepdims=True), row_tot)
    before = jnp.dot(lower, row_tot.astype(BF16), preferred_element_type=F32)
    cnt = jnp.sum(row_tot, axis=0, keepdims=True)
    tiles = jnp.floor((cnt + (MOE_TILE - 1)) * (1.0 / MOE_TILE))
    first_tile = jnp.dot(jnp.broadcast_to(tiles, (SUBLANES, LANES)).astype(BF16), upper,
                         preferred_element_type=F32)[0:1]
    base = before + first_tile * MOE_TILE
    pos = jnp.zeros((n_rows, LANES), F32)
    for c in range(N_CLASSES):
        oh = jnp.where(cls == c, 1.0, 0.0)
        local = jnp.dot(oh.astype(BF16), upper, preferred_element_type=F32)
        pos = pos + oh * (base[:, c:c + 1] + local)
    pos_ref[...] = pos.astype(I32)
    sub = lax.broadcasted_iota(I32, (SUBLANES, LANES), 0)
    stats_ref[...] = jnp.where(sub == 0, cnt, jnp.where(sub == 1, tiles,
                                                        jnp.where(sub == 2, first_tile, 0.0)))


def _row_copy(src_ref, src_row, dst_ref, dst_row, sem):
    return pltpu.make_async_copy(src_ref.at[pl.ds(src_row, 1), :],
                                 dst_ref.at[pl.ds(dst_row, 1), :], sem)


def _perm_kernel(n, pos_ref, cstart_ref, ccnt_ref, cpad_ref, nused_ref, perm_ref):
    def stand_in_body(r, carry):
        perm_ref[r] = n + PAD_ROWS + r
        return carry

    lax.fori_loop(0, MOE_TILE, stand_in_body, 0, unroll=8)

    def unused_body(p, carry):
        perm_ref[p] = n
        return carry

    lax.fori_loop((nused_ref[0] + 1) * MOE_TILE, perm_ref.shape[0], unused_body, 0)

    def cls_body(c, k):
        base = MOE_TILE + cstart_ref[c] + ccnt_ref[c]

        def pad_body(q, carry):
            perm_ref[base + q] = n + k + q
            return carry

        lax.fori_loop(0, cpad_ref[c], pad_body, 0)
        return k + cpad_ref[c]

    lax.fori_loop(0, N_CLASSES, cls_body, 0)

    def tok_body(t, carry):
        perm_ref[MOE_TILE + pos_ref[t]] = t
        return carry

    lax.fori_loop(0, n, tok_body, 0, unroll=8)


def _expert_kernel(n, perm_ref, elo_ref, ehi_ref, flag_ref,
                   x_ref, wg_lo, wu_lo, wd_lo, wg_hi, wu_hi, wd_hi, ln_ref, out_ref,
                   xbuf, ybuf, w1_scr, wd_scr, gsem, ssem, zsem):
    i = pl.program_id(0)
    flag = flag_ref[i]
    slot = lax.rem(i, PIPE)
    nxt = lax.rem(i + 1, PIPE)
    other = lax.rem(i + 2, PIPE)

    def gather_wait(s):
        pltpu.make_async_copy(x_ref.at[pl.ds(0, MOE_TILE), :], xbuf.at[s], gsem.at[s]).wait()

    def scatter_wait(s):
        pltpu.make_async_copy(ybuf.at[s], out_ref.at[pl.ds(0, MOE_TILE), :], ssem.at[s]).wait()

    def gather_start(tile, s):
        for r in range(MOE_TILE):
            _row_copy(x_ref, perm_ref[(tile + 1) * MOE_TILE + r], xbuf.at[s], r,
                      gsem.at[s]).start(priority=r % 2)

    def scatter_start(tile, s):
        for r in range(MOE_TILE):
            _row_copy(ybuf.at[s], r, out_ref, perm_ref[(tile + 1) * MOE_TILE + r],
                      ssem.at[s]).start(priority=r % 2)

    @pl.when(i == 0)
    def _():
        ybuf[...] = jnp.zeros_like(ybuf)
        for c in range(PAD_ROWS // MOE_TILE):
            pltpu.make_async_copy(ybuf.at[0], out_ref.at[pl.ds(n + c * MOE_TILE, MOE_TILE), :],
                                  zsem).start()
        for c in range(PAD_ROWS // MOE_TILE):
            pltpu.make_async_copy(ybuf.at[0], out_ref.at[pl.ds(n + c * MOE_TILE, MOE_TILE), :],
                                  zsem).wait()
        for s in range(PIPE - 1):
            pltpu.make_async_copy(
                ybuf.at[s], out_ref.at[pl.ds(n + PAD_ROWS + (s + 1) * MOE_TILE, MOE_TILE), :],
                ssem.at[s]).start()
        gather_start(0, 0)
        gather_start(1, 1)

    @pl.when((flag & 2) != 0)
    def _():
        w1_scr[:, 0 * EXPERT_FF:1 * EXPERT_FF] = wg_lo[0, 0].astype(BF16)
        w1_scr[:, 1 * EXPERT_FF:2 * EXPERT_FF] = wu_lo[0, 0].astype(BF16)
        w1_scr[:, 2 * EXPERT_FF:3 * EXPERT_FF] = wg_hi[0, 0].astype(BF16)
        w1_scr[:, 3 * EXPERT_FF:4 * EXPERT_FF] = wu_hi[0, 0].astype(BF16)
        wd_scr[0:EXPERT_FF, :] = wd_lo[0, 0].astype(BF16)
        wd_scr[EXPERT_FF:, :] = wd_hi[0, 0].astype(BF16)

    @pl.when((flag & 1) != 0)
    def _():
        gather_wait(slot)
        scatter_wait(slot)
        gather_start(i + 2, other)
        scatter_start(i - 1, other)
        xs = xbuf[slot]
        x = xs[:, 0:D_MODEL]
        g_lo = xs[:, D_MODEL:D_MODEL + 1]
        g_hi = xs[:, D_MODEL + 1:D_MODEL + 2]
        hc = jnp.dot(x.astype(BF16), w1_scr[...], preferred_element_type=F32)
        hg_lo, hu_lo = hc[:, 0:EXPERT_FF], hc[:, EXPERT_FF:2 * EXPERT_FF]
        hg_hi, hu_hi = hc[:, 2 * EXPERT_FF:3 * EXPERT_FF], hc[:, 3 * EXPERT_FF:]
        h_lo = hg_lo * _sigmoid(hg_lo) * hu_lo * g_lo
        h_hi = hg_hi * _sigmoid(hg_hi) * hu_hi * g_hi
        h = jnp.concatenate([h_lo, h_hi], axis=-1).astype(BF16)
        y = jnp.dot(h, wd_scr[...], preferred_element_type=F32)
        ybuf[slot] = _layer_norm(ALPHA * x + y, ln_ref[0, 0:1, :], ln_ref[0, 1:2, :])

    @pl.when((flag & 4) != 0)
    def _():
        scatter_start(i - 1, other)
        for s in (slot, nxt, other):
            scatter_wait(s)
        for s in (slot, nxt):
            gather_wait(s)


def _layer_spec(shape, layer, n_grid, single_buffer=False):
    idx = (layer,) + (0,) * len(shape)
    if n_grid == 1:
        index_map = lambda i: idx
    else:
        index_map = lambda i, j: idx
    if single_buffer:
        return pl.BlockSpec((1,) + shape, index_map, pipeline_mode=pl.Buffered(1))
    return pl.BlockSpec((1,) + shape, index_map)


def _prompt_mixer(x, w_in, w_out, params, ln, wr, rb, layer, batch, seq, n_rows_out):
    n_t = seq // PROMPT_TILE
    n_real = batch * n_t
    last = n_real - 1
    row_in = pl.BlockSpec((PROMPT_TILE, D_MODEL), lambda i: (jnp.minimum(i, last), 0))
    row_out = pl.BlockSpec((PROMPT_TILE, ROW_W), lambda i: (i, 0))
    route_out = pl.BlockSpec((SUBLANES, PROMPT_TILE), lambda i: (0, jnp.minimum(i, last)))
    full1 = lambda shape: pl.BlockSpec(shape, lambda i: (0, 0))
    seq_of = lambda i: jnp.minimum(i, last) // n_t
    return pl.pallas_call(
        functools.partial(_prompt_mixer_kernel, n_real, n_t),
        grid=(n_rows_out // PROMPT_TILE,),
        in_specs=[row_in,
                  _layer_spec((D_MODEL, N_IN), layer, 1, True),
                  _layer_spec((D_MODEL, D_MODEL), layer, 1, True),
                  _layer_spec((8, 512), layer, 1), _layer_spec((2, D_MODEL), layer, 1),
                  full1((2 * N_EXPERTS, D_MODEL)), full1((N_EXPERTS, 1))],
        out_specs=[row_out, route_out,
                   pl.BlockSpec((1, HEADS, HEAD_DIM, HEAD_DIM), lambda i: (seq_of(i), 0, 0, 0)),
                   pl.BlockSpec((1, CONV_K - 1, CONV_DIM), lambda i: (seq_of(i), 0, 0))],
        out_shape=[jax.ShapeDtypeStruct((n_rows_out, ROW_W), F32),
                   jax.ShapeDtypeStruct((SUBLANES, batch * seq), F32),
                   jax.ShapeDtypeStruct((batch, HEADS, HEAD_DIM, HEAD_DIM), F32),
                   jax.ShapeDtypeStruct((batch, CONV_K - 1, CONV_DIM), F32)],
        scratch_shapes=[pltpu.VMEM((D_MODEL, N_IN), BF16), pltpu.VMEM((D_MODEL, D_MODEL), BF16),
                        pltpu.VMEM((PROMPT_TILE, N_IN), F32)]
        + [pltpu.VMEM((PROMPT_TILE, 512), F32)] * 5
        + [pltpu.VMEM((HEADS, HEAD_DIM, HEAD_DIM), F32)] * 2
        + [pltpu.VMEM((SUBLANES, CONV_DIM), F32)],
        compiler_params=pltpu.CompilerParams(
            dimension_semantics=("arbitrary",), vmem_limit_bytes=VMEM_LIMIT),
        name="prompt_mixer",
    )(x, w_in, w_out, params, ln, wr, rb)


def _sample_mixer(x, x_row0, x1_all, s0, cbuf, w_in, w_out, params, ln, wr, rb, layer, batch,
                  n_valid):
    tile = SAMPLE_SEQS * SAMPLE_ROWS
    n = batch * SAMPLE_ROWS
    x1_row0 = x1_all.shape[0] - PAD_ROWS - n
    in_blk0, out_blk0 = x_row0 // tile, x1_row0 // tile
    row_in = pl.BlockSpec((tile, D_MODEL), lambda i: (in_blk0 + i, 0))
    row_out = pl.BlockSpec((tile, ROW_W), lambda i: (out_blk0 + i, 0))
    route_out = pl.BlockSpec((SUBLANES, tile), lambda i: (0, i))
    st_spec = pl.BlockSpec((1, SAMPLE_SEQS, HEADS, HEAD_DIM, HEAD_DIM),
                           lambda i: (layer, i, 0, 0, 0))
    cb_spec = pl.BlockSpec((1, SAMPLE_SEQS, CONV_K - 1, CONV_DIM), lambda i: (layer, i, 0, 0))
    st_out = pl.BlockSpec((SAMPLE_SEQS, HEADS, HEAD_DIM, HEAD_DIM), lambda i: (i, 0, 0, 0))
    cb_out = pl.BlockSpec((SAMPLE_SEQS, CONV_K - 1, CONV_DIM), lambda i: (i, 0, 0))
    full1 = lambda shape: pl.BlockSpec(shape, lambda i: (0, 0))
    return pl.pallas_call(
        functools.partial(_sample_mixer_kernel, n_valid),
        grid=(batch // SAMPLE_SEQS,),
        in_specs=[row_in, st_spec, cb_spec,
                  _layer_spec((D_MODEL, N_IN), layer, 1, True),
                  _layer_spec((D_MODEL, D_MODEL), layer, 1, True),
                  _layer_spec((8, 512), layer, 1), _layer_spec((2, D_MODEL), layer, 1),
                  full1((2 * N_EXPERTS, D_MODEL)), full1((N_EXPERTS, 1)),
                  pl.BlockSpec(memory_space=pl.ANY)],
        out_specs=[row_out, route_out, st_out, cb_out],
        input_output_aliases={9: 0},
        out_shape=[jax.ShapeDtypeStruct(x1_all.shape, F32),
                   jax.ShapeDtypeStruct((SUBLANES, n), F32),
                   jax.ShapeDtypeStruct((batch, HEADS, HEAD_DIM, HEAD_DIM), F32),
                   jax.ShapeDtypeStruct((batch, CONV_K - 1, CONV_DIM), F32)],
        scratch_shapes=[pltpu.VMEM((D_MODEL, N_IN), BF16), pltpu.VMEM((D_MODEL, D_MODEL), BF16),
                        pltpu.VMEM((tile, N_IN), F32)] + [pltpu.VMEM((tile, 512), F32)] * 5,
        compiler_params=pltpu.CompilerParams(
            dimension_semantics=("arbitrary",), vmem_limit_bytes=VMEM_LIMIT),
        name="sample_mixer",
    )(x, s0, cbuf, w_in, w_out, params, ln, wr, rb, x1_all)


def _plan(cls2d):
    n_rows = cls2d.shape[0]
    return pl.pallas_call(
        _plan_kernel,
        out_shape=[jax.ShapeDtypeStruct((n_rows, LANES), I32),
                   jax.ShapeDtypeStruct((SUBLANES, LANES), F32)],
        name="moe_plan",
    )(cls2d)


def _perm(n, pos, cstart, ccnt, cpad, n_used, n_tiles):
    smem = pl.BlockSpec(memory_space=pltpu.SMEM)
    return pl.pallas_call(
        functools.partial(_perm_kernel, n),
        in_specs=[smem] * 5, out_specs=smem,
        out_shape=jax.ShapeDtypeStruct(((n_tiles + PIPE) * MOE_TILE,), I32),
        name="moe_perm",
    )(pos, cstart, ccnt, cpad, n_used.reshape(1))


def _experts(n, perm, elo, ehi, flag, x1_all, w_gate, w_up, w_down, ln, layer, n_steps):
    def w_spec(shape, table_idx):
        def index_map(i, perm_r, elo_r, ehi_r, flag_r):
            return (layer, (elo_r, ehi_r)[table_idx][i], 0, 0)
        return pl.BlockSpec((1, 1) + shape, index_map)

    up = (D_MODEL, EXPERT_FF)
    down = (EXPERT_FF, D_MODEL)
    any_spec = pl.BlockSpec(memory_space=pl.ANY)
    return pl.pallas_call(
        functools.partial(_expert_kernel, n),
        grid_spec=pltpu.PrefetchScalarGridSpec(
            num_scalar_prefetch=4, grid=(n_steps,),
            in_specs=[any_spec,
                      w_spec(up, 0), w_spec(up, 0), w_spec(down, 0),
                      w_spec(up, 1), w_spec(up, 1), w_spec(down, 1),
                      pl.BlockSpec((1, 2, D_MODEL), lambda i, p, e0, e1, f: (layer, 0, 0))],
            out_specs=any_spec,
            scratch_shapes=[pltpu.VMEM((PIPE, MOE_TILE, ROW_W), F32),
                            pltpu.VMEM((PIPE, MOE_TILE, D_MODEL), F32),
                            pltpu.VMEM((D_MODEL, 4 * EXPERT_FF), BF16),
                            pltpu.VMEM((2 * EXPERT_FF, D_MODEL), BF16),
                            pltpu.SemaphoreType.DMA((PIPE,)), pltpu.SemaphoreType.DMA((PIPE,)),
                            pltpu.SemaphoreType.DMA(())]),
        out_shape=jax.ShapeDtypeStruct((n + PAD_ROWS + PIPE * MOE_TILE, D_MODEL), F32),
        compiler_params=pltpu.CompilerParams(
            dimension_semantics=("arbitrary",), vmem_limit_bytes=VMEM_LIMIT),
        name="moe_experts",
    )(perm, elo, ehi, flag, x1_all, w_gate, w_up, w_down, w_gate, w_up, w_down, ln)


def _moe(x1_all, route_p, route_s, w_gate, w_up, w_down, ln2, layer):
    n = x1_all.shape[0] - PAD_ROWS
    n_tiles = n // MOE_TILE + N_CLASSES
    n_steps = n_tiles + 1
    cls2d = jnp.concatenate([route_p[0], route_s[0]]).reshape(n // LANES, LANES)
    pos2d, stats = _plan(cls2d)

    cnt = stats[0, :N_CLASSES].astype(I32)
    tiles = stats[1, :N_CLASSES].astype(I32)
    first_tile = stats[2, :N_CLASSES].astype(I32)
    end_tile = first_tile + tiles
    n_used = end_tile[N_CLASSES - 1]
    step = jnp.arange(n_steps, dtype=I32)
    tcls = jnp.sum((end_tile[None, :] <= jnp.minimum(step, n_used - 1)[:, None]).astype(I32), axis=1)
    valid = step < n_used
    changed = jnp.concatenate([jnp.ones((1,), jnp.bool_), tcls[1:] != tcls[:-1]]) & valid
    flag = valid.astype(I32) + 2 * changed.astype(I32) + 4 * (step == n_used).astype(I32)
    group = tcls // N_PAIRS
    pair = tcls % N_PAIRS
    elo = group * PER_GROUP + jnp.asarray(PAIR_LO, I32)[pair]
    ehi = group * PER_GROUP + jnp.asarray(PAIR_HI, I32)[pair]

    perm = _perm(n, pos2d.reshape(n), first_tile * MOE_TILE, cnt, tiles * MOE_TILE - cnt, n_used,
                 n_tiles)
    return _experts(n, perm, elo, ehi, flag, x1_all, w_gate, w_up, w_down, ln2, layer, n_steps)


def kernel(x_prompt, x_sample, state_hgrn, state_conv, w_in, w_out, lower_bounds, hgrn_norm_g,
           conv_w, conv_norm_g, ln1_g, ln1_b, ln2_g, ln2_b, w_router, router_bias,
           w_gate, w_up, w_down):
    batch, seq, _ = x_prompt.shape
    dec_batch, dec_seq, _ = x_sample.shape
    assert seq % PROMPT_TILE == 0 and dec_batch % SAMPLE_SEQS == 0
    assert CONV_K - 1 <= dec_seq <= SAMPLE_ROWS
    assert (batch * seq + dec_batch * SAMPLE_ROWS) % MOE_TILE == 0

    lb = jnp.cumsum(jax.nn.softmax(lower_bounds.astype(F32), axis=0), axis=0)
    lb = lb - lb[0:1]
    params = jnp.stack([jnp.log(lb), jnp.log1p(-lb), 1.0 - lb, hgrn_norm_g, conv_norm_g,
                        conv_w[:, 0], conv_w[:, 1], conv_w[:, 2]], axis=1)
    ln1 = jnp.stack([ln1_g, ln1_b], axis=1)
    ln2 = jnp.stack([ln2_g, ln2_b], axis=1)
    wr_hi = w_router.astype(BF16)
    wr_lo = (w_router - wr_hi.astype(F32)).astype(BF16)
    wr = jnp.concatenate([wr_hi.T, wr_lo.T], axis=0)
    rb = router_bias.astype(F32).reshape(N_EXPERTS, 1)

    n_p, n_s = batch * seq, dec_batch * SAMPLE_ROWS
    n = n_p + n_s
    xp = x_prompt.reshape(n_p, D_MODEL)
    xs = jnp.pad(x_sample, ((0, 0), (0, SAMPLE_ROWS - dec_seq), (0, 0))).reshape(n_s, D_MODEL)
    xs_row0 = 0

    s_p, b_p, s_s, b_s = [], [], [], []
    for l in range(DEPTH):
        x1_all, route_p, s_l, b_l = _prompt_mixer(xp, w_in, w_out, params, ln1, wr, rb, l, batch,
                                                  seq, n + PAD_ROWS)
        s_p.append(s_l)
        b_p.append(b_l)
        x1_all, route_s, s_l, b_l = _sample_mixer(xs, xs_row0, x1_all, state_hgrn, state_conv, w_in,
                                                  w_out, params, ln1, wr, rb, l, dec_batch, dec_seq)
        s_s.append(s_l)
        b_s.append(b_l)
        xp = xs = _moe(x1_all, route_p, route_s, w_gate, w_up, w_down, ln2, l)
        xs_row0 = n_p

    y_prompt = xp[:n_p].reshape(batch, seq, D_MODEL)
    y_sample = xs[n_p:n].reshape(dec_batch, SAMPLE_ROWS, D_MODEL)[:, :dec_seq]
    return (y_prompt, y_sample, jnp.stack(s_p), jnp.stack(b_p), jnp.stack(s_s), jnp.stack(b_s))
```

```python
import functools

import jax
import jax.numpy as jnp
from jax import lax
from jax.experimental import pallas as pl
from jax.experimental.pallas import tpu as pltpu

F32 = jnp.float32
BF16 = jnp.bfloat16
I32 = jnp.int32

D_MODEL = 1024
DEPTH = 2
HEADS = 4
HEAD_DIM = 128
HGRN_W = HEADS * HEAD_DIM
CONV_DIM = 512
CONV_GROUPS = 8
CONV_K = 3
N_IN = 7 * 512
N_EXPERTS = 16
N_GROUPS = 4
PER_GROUP = 4
N_PAIRS = 6
N_CLASSES = N_GROUPS * N_PAIRS
PAIR_LO = (0, 0, 0, 1, 1, 2)
PAIR_HI = (1, 2, 3, 2, 3, 3)
EXPERT_FF = 512
ALPHA = (2 * DEPTH) ** 0.25
LN_EPS = 1e-5
RMS_EPS = 1e-6
SAFE_EXPONENT = 80.0

LANES = 128
SUBLANES = 8
CHUNK = 64
PROMPT_TILE = 256
SAMPLE_ROWS = 8
SAMPLE_SEQS = 16
MOE_TILE = 256
ROW_W = D_MODEL + LANES
PAD_ROWS = N_CLASSES * MOE_TILE
PIPE = 3
VMEM_LIMIT = 56 * 1024 * 1024

OFF_Q, OFF_F, OFF_I, OFF_G, OFF_GB, OFF_GC, OFF_CX = (i * 512 for i in range(7))

P_LOGLB, P_LOG1MLB, P_OMLB, P_HNORM, P_CNORM, P_CW0, P_CW1, P_CW2 = range(8)

NT_DIMS = (((1,), (1,)), ((), ()))
TN_DIMS = (((0,), (0,)), ((), ()))


def _sigmoid(x):
    return 1.0 / (1.0 + jnp.exp(-x))


def _seg_cumsum(x, seg):
    row = lax.broadcasted_iota(I32, x.shape, 0)
    pos = row & (seg - 1)
    sh = 1
    while sh < seg:
        x = x + jnp.where(pos >= sh, pltpu.roll(x, sh, axis=0), 0.0)
        sh *= 2
    return x


def _gate_terms(z, p_ref):
    e = jnp.exp(-jnp.abs(z))
    inv = 1.0 / (1.0 + e)
    logsig = jnp.minimum(z, 0.0) - jnp.log(1.0 + e)
    a = p_ref[0, P_LOGLB:P_LOGLB + 1, :]
    b = p_ref[0, P_LOG1MLB:P_LOG1MLB + 1, :] + logsig
    logf = jnp.maximum(a, b) + jnp.log(1.0 + jnp.exp(-jnp.abs(a - b)))
    k = p_ref[0, P_OMLB:P_OMLB + 1, :] * (jnp.where(z >= 0.0, e, 1.0) * inv)
    return logf, k


def _exact_block(qs_c, b_c, k_c, v_c, states, n_valid):
    trow = lax.broadcasted_iota(I32, (SAMPLE_ROWS, 1), 0)
    b_last = b_c[n_valid - 1:n_valid, :]
    qb = (qs_c * jnp.exp(b_c)).astype(BF16)
    kend = (k_c * jnp.exp(b_last - b_c)).astype(BF16)
    dec = jnp.exp(b_last)
    outs, new_states = [], []
    for h in range(HEADS):
        sl = slice(h * HEAD_DIM, (h + 1) * HEAD_DIM)
        st = states[h]
        o_h = lax.dot_general(qb[:, sl], st.astype(BF16), NT_DIMS, preferred_element_type=F32)
        for t in range(n_valid):
            dlt = jnp.minimum(b_c[:, sl] - b_c[t:t + 1, sl], 0.0)
            a_col = jnp.sum(qs_c[:, sl] * k_c[t:t + 1, sl] * jnp.exp(dlt), axis=-1, keepdims=True)
            a_col = jnp.where(trow >= t, a_col, 0.0)
            o_h = o_h + a_col * v_c[t:t + 1, sl]
        d_st = lax.dot_general(v_c[:, sl].astype(BF16), kend[:, sl], TN_DIMS,
                               preferred_element_type=F32)
        outs.append(o_h)
        new_states.append(st * dec[:, sl] + d_st)
    return outs, new_states


def _group_rms(x, n_groups):
    width = x.shape[-1] // n_groups
    x2 = x * x
    outs = []
    for s in range(x.shape[-1] // LANES):
        xs = x[:, s * LANES:(s + 1) * LANES]
        x2s = x2[:, s * LANES:(s + 1) * LANES]
        if width == LANES:
            ms = jnp.sum(x2s, axis=-1, keepdims=True) * (1.0 / width)
            scale = lax.rsqrt(ms + RMS_EPS)
        else:
            lane = lax.broadcasted_iota(I32, xs.shape, 1)
            lo = lane < width
            ms_lo = jnp.sum(jnp.where(lo, x2s, 0.0), axis=-1, keepdims=True) * (1.0 / width)
            ms_hi = jnp.sum(jnp.where(lo, 0.0, x2s), axis=-1, keepdims=True) * (1.0 / width)
            scale = jnp.where(lo, lax.rsqrt(ms_lo + RMS_EPS), lax.rsqrt(ms_hi + RMS_EPS))
        outs.append(xs * scale)
    return jnp.concatenate(outs, axis=-1)


def _layer_norm(r, g, b):
    mu = jnp.mean(r, axis=-1, keepdims=True)
    rc = r - mu
    var = jnp.mean(rc * rc, axis=-1, keepdims=True)
    return rc * lax.rsqrt(var + LN_EPS) * g + b


def _route(x1, wr_ref, rb_ref):
    x_hi = x1.astype(BF16)
    x_lo = (x1 - x_hi.astype(F32)).astype(BF16)
    wr = wr_ref[...]
    r1 = lax.dot_general(wr, x_hi, NT_DIMS, preferred_element_type=F32)
    r2 = lax.dot_general(wr, x_lo, NT_DIMS, preferred_element_type=F32)
    lt = r1[0:N_EXPERTS] + r1[N_EXPERTS:] + r2[0:N_EXPERTS] + r2[N_EXPERTS:] + rb_ref[...]
    lg = [lt[e:e + 1, :] for e in range(N_EXPERTS)]
    mx = lg[0]
    for e in range(1, N_EXPERTS):
        mx = jnp.maximum(mx, lg[e])
    ex = [jnp.exp(l - mx) for l in lg]
    best = None
    gi = None
    for g in range(N_GROUPS):
        a, b, c, d = ex[PER_GROUP * g:PER_GROUP * (g + 1)]
        s = jnp.maximum(jnp.maximum(jnp.maximum(a + b, a + c), jnp.maximum(a + d, b + c)),
                        jnp.maximum(b + d, c + d))
        if g == 0:
            best, gi = s, jnp.zeros(s.shape, I32)
        else:
            upd = s > best
            best = jnp.where(upd, s, best)
            gi = jnp.where(upd, g, gi)
    v = []
    for i in range(PER_GROUP):
        vi = ex[3 * PER_GROUP + i]
        for g in (2, 1, 0):
            vi = jnp.where(gi == g, ex[PER_GROUP * g + i], vi)
        v.append(vi)
    w1, i1 = v[0], jnp.zeros(v[0].shape, I32)
    for i in range(1, PER_GROUP):
        upd = v[i] > w1
        w1 = jnp.where(upd, v[i], w1)
        i1 = jnp.where(upd, i, i1)
    w2, i2 = None, None
    for i in range(PER_GROUP):
        vi = jnp.where(i1 == i, -1.0, v[i])
        if i == 0:
            w2, i2 = vi, jnp.zeros(vi.shape, I32)
        else:
            upd = vi > w2
            w2 = jnp.where(upd, vi, w2)
            i2 = jnp.where(upd, i, i2)
    inv = 1.0 / (w1 + w2)
    first_lo = i1 < i2
    lo = jnp.where(first_lo, i1, i2)
    hi = jnp.where(first_lo, i2, i1)
    g_lo = jnp.where(first_lo, w1, w2) * inv
    g_hi = jnp.where(first_lo, w2, w1) * inv
    pair = jnp.where(lo == 0, 0, jnp.where(lo == 1, 3, 5)) + hi - lo - 1
    return gi * N_PAIRS + pair, g_lo, g_hi


def _post_mix(xt, o, g, yc_in, p_ref, wout_scr, ln_ref, wr_ref, rb_ref, x1_ref, route_ref):
    rows = xt.shape[0]
    o = _group_rms(o, HEADS) * p_ref[0, P_HNORM:P_HNORM + 1, :]
    o = o * (g * _sigmoid(g))
    yc = _group_rms(yc_in, CONV_GROUPS) * p_ref[0, P_CNORM:P_CNORM + 1, :]
    mix = jnp.concatenate([o, yc], axis=-1).astype(BF16)
    h = jnp.dot(mix, wout_scr[...], preferred_element_type=F32)
    x1 = _layer_norm(ALPHA * xt + h, ln_ref[0, 0:1, :], ln_ref[0, 1:2, :])
    cls, g_lo, g_hi = _route(x1, wr_ref, rb_ref)
    x1_ref[:, 0:D_MODEL] = x1
    sub = lax.broadcasted_iota(I32, (LANES, rows), 0)
    gates_t = jnp.where(sub == 0, g_lo, jnp.where(sub == 1, g_hi, 0.0))
    x1_ref[:, D_MODEL:ROW_W] = jnp.transpose(gates_t)
    sub8 = lax.broadcasted_iota(I32, (SUBLANES, rows), 0)
    route_ref[...] = jnp.where(sub8 == 0, cls.astype(F32), 0.0)


def _prompt_mixer_kernel(n_real, n_t, x_ref, *refs):
    i = pl.program_id(0)
    x1_ref = refs[6]

    @pl.when(i < n_real)
    def _():
        _prompt_tile(i, lax.rem(i, n_t), n_t, x_ref, *refs)

    @pl.when(i >= n_real)
    def _():
        x1_ref[...] = jnp.zeros_like(x1_ref)


def _prompt_tile(i, j, n_t, x_ref, win_ref, wout_ref, p_ref, ln_ref, wr_ref, rb_ref,
                 x1_ref, route_ref, s_ref, buf_ref,
                 win_scr, wout_scr, proj_scr, qs_scr, b_scr, k_scr, o_scr, yc_scr, st_scr, st0_scr,
                 ubuf_scr):
    @pl.when(i == 0)
    def _():
        win_scr[...] = win_ref[0].astype(BF16)
        wout_scr[...] = wout_ref[0].astype(BF16)

    @pl.when(j == 0)
    def _():
        st_scr[...] = jnp.zeros_like(st_scr)
        ubuf_scr[...] = jnp.zeros_like(ubuf_scr)

    xt = x_ref[...]
    proj_scr[...] = jnp.dot(xt.astype(BF16), win_scr[...], preferred_element_type=F32)

    q = proj_scr[:, OFF_Q:OFF_Q + 512]
    qs_scr[...] = q * _sigmoid(q)
    logf, k = _gate_terms(proj_scr[:, OFF_F:OFF_F + 512], p_ref)
    k_scr[...] = k
    b_scr[...] = _seg_cumsum(logf, CHUNK)

    tri = (lax.broadcasted_iota(I32, (CHUNK, CHUNK), 0)
           >= lax.broadcasted_iota(I32, (CHUNK, CHUNK), 1))

    states = [st_scr[h] for h in range(HEADS)]
    for h in range(HEADS):
        st0_scr[h] = states[h]
    worst = jnp.zeros((1, 512), F32)
    for c in range(PROMPT_TILE // CHUNK):
        rows = slice(c * CHUNK, (c + 1) * CHUNK)
        qs_c = qs_scr[rows, :]
        b_c = b_scr[rows, :]
        k_c = k_scr[rows, :]
        v_c = proj_scr[rows, OFF_I:OFF_I + 512].astype(BF16)
        b_mid = b_c[CHUNK // 2 - 1:CHUNK // 2, :]
        b_last = b_c[CHUNK - 1:CHUNK, :]
        worst = jnp.maximum(worst, jnp.maximum(-b_mid, b_mid - b_last))
        qd = (qs_c * jnp.exp(b_c - b_mid)).astype(BF16)
        kd = (k_c * jnp.exp(b_mid - b_c)).astype(BF16)
        qb = (qs_c * jnp.exp(b_c)).astype(BF16)
        kend = (k_c * jnp.exp(b_last - b_c)).astype(BF16)
        dec = jnp.exp(b_last)
        for h in range(HEADS):
            sl = slice(h * HEAD_DIM, (h + 1) * HEAD_DIM)
            att = lax.dot_general(qd[:, sl], kd[:, sl], NT_DIMS, preferred_element_type=F32)
            att = jnp.where(tri, att, 0.0).astype(BF16)
            st = states[h]
            o_h = jnp.dot(att, v_c[:, sl], preferred_element_type=F32)
            o_h = o_h + lax.dot_general(qb[:, sl], st.astype(BF16), NT_DIMS,
                                        preferred_element_type=F32)
            o_scr[rows, sl] = o_h
            d_st = lax.dot_general(v_c[:, sl], kend[:, sl], TN_DIMS, preferred_element_type=F32)
            states[h] = st * dec[:, sl] + d_st
    for h in range(HEADS):
        st_scr[h] = states[h]

    u = proj_scr[:, OFF_GC:OFF_GC + 512] * proj_scr[:, OFF_CX:OFF_CX + 512]
    row = lax.broadcasted_iota(I32, u.shape, 0)
    prev2 = ubuf_scr[SUBLANES - 2:SUBLANES - 1, :]
    prev1 = ubuf_scr[SUBLANES - 1:SUBLANES, :]
    u1 = jnp.where(row == 0, prev1, pltpu.roll(u, 1, axis=0))
    u2 = jnp.where(row == 0, prev2, jnp.where(row == 1, prev1, pltpu.roll(u, 2, axis=0)))
    y = (p_ref[0, P_CW0:P_CW0 + 1, :] * u2 + p_ref[0, P_CW1:P_CW1 + 1, :] * u1
         + p_ref[0, P_CW2:P_CW2 + 1, :] * u)
    ubuf_scr[...] = u[PROMPT_TILE - SUBLANES:, :]
    yc_scr[...] = proj_scr[:, OFF_GB:OFF_GB + 512] * y

    def finish(x_tile):
        _post_mix(x_tile, o_scr[...], proj_scr[:, OFF_G:OFF_G + 512], yc_scr[...], p_ref, wout_scr,
                  ln_ref, wr_ref, rb_ref, x1_ref, route_ref)

    finish(xt)

    @pl.when(jnp.logical_not(jnp.max(worst) <= SAFE_EXPONENT))
    def _():
        for h in range(HEADS):
            st_scr[h] = st0_scr[h]

        def block_body(blk, carry):
            r0 = pl.multiple_of(blk * SAMPLE_ROWS, SAMPLE_ROWS)
            rows = pl.ds(r0, SAMPLE_ROWS)
            b_blk = b_scr[rows, :]
            b_prev = b_scr[pl.ds(jnp.maximum(r0 - 1, 0), 1), :]
            b_prev = jnp.where((r0 & (CHUNK - 1)) == 0, 0.0, b_prev)
            outs, new_states = _exact_block(
                qs_scr[rows, :], b_blk - b_prev, k_scr[rows, :], proj_scr[rows, OFF_I:OFF_I + 512],
                [st_scr[h] for h in range(HEADS)], SAMPLE_ROWS)
            for h in range(HEADS):
                o_scr[rows, h * HEAD_DIM:(h + 1) * HEAD_DIM] = outs[h]
                st_scr[h] = new_states[h]
            return carry

        lax.fori_loop(0, PROMPT_TILE // SAMPLE_ROWS, block_body, 0)
        finish(x_ref[...])

    @pl.when(j == n_t - 1)
    def _():
        for h in range(HEADS):
            s_ref[0, h] = jnp.transpose(st_scr[h])
        buf_ref[0] = ubuf_scr[SUBLANES - (CONV_K - 1):, :]


def _sample_mixer_kernel(n_valid, x_ref, s0_ref, cbuf_ref, win_ref, wout_ref, p_ref, ln_ref,
                         wr_ref, rb_ref, x1_alias_ref,
                         x1_ref, route_ref, s_ref, buf_ref,
                         win_scr, wout_scr, proj_scr, qs_scr, b_scr, k_scr, o_scr, y_scr):
    tile = SAMPLE_SEQS * SAMPLE_ROWS

    @pl.when(pl.program_id(0) == 0)
    def _():
        win_scr[...] = win_ref[0].astype(BF16)
        wout_scr[...] = wout_ref[0].astype(BF16)

    xt = x_ref[...]
    proj_scr[...] = jnp.dot(xt.astype(BF16), win_scr[...], preferred_element_type=F32)

    q = proj_scr[:, OFF_Q:OFF_Q + 512]
    qs_scr[...] = q * _sigmoid(q)
    logf, k = _gate_terms(proj_scr[:, OFF_F:OFF_F + 512], p_ref)
    valid = (lax.broadcasted_iota(I32, (tile, 512), 0) & (SAMPLE_ROWS - 1)) < n_valid
    k_scr[...] = jnp.where(valid, k, 0.0)
    b_scr[...] = _seg_cumsum(jnp.where(valid, logf, 0.0), SAMPLE_ROWS)

    urow = lax.broadcasted_iota(I32, (SAMPLE_ROWS, 512), 0)

    def seq_body(s, carry):
        r0 = pl.multiple_of(s * SAMPLE_ROWS, SAMPLE_ROWS)
        rows = pl.ds(r0, SAMPLE_ROWS)
        qs_c = qs_scr[rows, :]
        b_c = b_scr[rows, :]
        k_c = k_scr[rows, :]
        v_c = proj_scr[rows, OFF_I:OFF_I + 512]
        states = [jnp.transpose(s0_ref[0, s, h]) for h in range(HEADS)]
        outs, states = _exact_block(qs_c, b_c, k_c, v_c, states, n_valid)
        for h in range(HEADS):
            o_scr[rows, h * HEAD_DIM:(h + 1) * HEAD_DIM] = outs[h]
            s_ref[s, h] = jnp.transpose(states[h])
        u = proj_scr[rows, OFF_GC:OFF_GC + 512] * proj_scr[rows, OFF_CX:OFF_CX + 512]
        prev2 = cbuf_ref[0, s, 0:1, :]
        prev1 = cbuf_ref[0, s, 1:2, :]
        u1 = jnp.where(urow == 0, prev1, pltpu.roll(u, 1, axis=0))
        u2 = jnp.where(urow == 0, prev2, jnp.where(urow == 1, prev1, pltpu.roll(u, 2, axis=0)))
        y_scr[rows, :] = (p_ref[0, P_CW0:P_CW0 + 1, :] * u2 + p_ref[0, P_CW1:P_CW1 + 1, :] * u1
                          + p_ref[0, P_CW2:P_CW2 + 1, :] * u)
        buf_ref[s] = u[n_valid - (CONV_K - 1):n_valid, :]
        return carry

    lax.fori_loop(0, SAMPLE_SEQS, seq_body, 0)

    yc_in = proj_scr[:, OFF_GB:OFF_GB + 512] * y_scr[...]
    _post_mix(xt, o_scr[...], proj_scr[:, OFF_G:OFF_G + 512], yc_in, p_ref, wout_scr, ln_ref,
              wr_ref, rb_ref, x1_ref, route_ref)


def _plan_kernel(cls_ref, pos_ref, stats_ref):
    n_rows = cls_ref.shape[0]
    cls = cls_ref[...]
    upper = (lax.broadcasted_iota(I32, (LANES, LANES), 0)
             < lax.broadcasted_iota(I32, (LANES, LANES), 1)).astype(BF16)
    lower = (lax.broadcasted_iota(I32, (n_rows, n_rows), 1)
             < lax.broadcasted_iota(I32, (n_rows, n_rows), 0)).astype(BF16)
    lane = lax.broadcasted_iota(I32, (n_rows, LANES), 1)
    row_tot = jnp.zeros((n_rows, LANES), F32)
    for c in range(N_CLASSES):
        oh = jnp.where(cls == c, 1.0, 0.0)
        row_tot = jnp.where(lane == c, jnp.sum(oh, axis=1, keepdims=True), row_tot)
    before = jnp.dot(lower, row_tot.astype(BF16), preferred_element_type=F32)
    cnt = jnp.sum(row_tot, axis=0, keepdims=True)
    tiles = jnp.floor((cnt + (MOE_TILE - 1)) * (1.0 / MOE_TILE))
    first_tile = jnp.dot(jnp.broadcast_to(tiles, (SUBLANES, LANES)).astype(BF16), upper,
                         preferred_element_type=F32)[0:1]
    base = before + first_tile * MOE_TILE
    pos = jnp.zeros((n_rows, LANES), F32)
    for c in range(N_CLASSES):
        oh = jnp.where(cls == c, 1.0, 0.0)
        local = jnp.dot(oh.astype(BF16), upper, preferred_element_type=F32)
        pos = pos + oh * (base[:, c:c + 1] + local)
    pos_ref[...] = pos.astype(I32)
    sub = lax.broadcasted_iota(I32, (SUBLANES, LANES), 0)
    stats_ref[...] = jnp.where(sub == 0, cnt, jnp.where(sub == 1, tiles,
                                                        jnp.where(sub == 2, first_tile, 0.0)))


def _row_copy(src_ref, src_row, dst_ref, dst_row, sem):
    return pltpu.make_async_copy(src_ref.at[pl.ds(src_row, 1), :],
                                 dst_ref.at[pl.ds(dst_row, 1), :], sem)


def _perm_kernel(n, pos_ref, cstart_ref, ccnt_ref, cpad_ref, nused_ref, perm_ref):
    def stand_in_body(r, carry):
        perm_ref[r] = n + PAD_ROWS + r
        return carry

    lax.fori_loop(0, MOE_TILE, stand_in_body, 0, unroll=8)

    def unused_body(p, carry):
        perm_ref[p] = n
        return carry

    lax.fori_loop((nused_ref[0] + 1) * MOE_TILE, perm_ref.shape[0], unused_body, 0)

    def cls_body(c, k):
        base = MOE_TILE + cstart_ref[c] + ccnt_ref[c]

        def pad_body(q, carry):
            perm_ref[base + q] = n + k + q
            return carry

        lax.fori_loop(0, cpad_ref[c], pad_body, 0)
        return k + cpad_ref[c]

    lax.fori_loop(0, N_CLASSES, cls_body, 0)

    def tok_body(t, carry):
        perm_ref[MOE_TILE + pos_ref[t]] = t
        return carry

    lax.fori_loop(0, n, tok_body, 0, unroll=8)


def _expert_kernel(n, perm_ref, elo_ref, ehi_ref, flag_ref,
                   x_ref, wg_lo, wu_lo, wd_lo, wg_hi, wu_hi, wd_hi, ln_ref, out_ref,
                   xbuf, ybuf, w1_scr, wd_scr, gsem, ssem, zsem):
    i = pl.program_id(0)
    flag = flag_ref[i]
    slot = lax.rem(i, PIPE)
    nxt = lax.rem(i + 1, PIPE)
    other = lax.rem(i + 2, PIPE)

    def gather_wait(s):
        pltpu.make_async_copy(x_ref.at[pl.ds(0, MOE_TILE), :], xbuf.at[s], gsem.at[s]).wait()

    def scatter_wait(s):
        pltpu.make_async_copy(ybuf.at[s], out_ref.at[pl.ds(0, MOE_TILE), :], ssem.at[s]).wait()

    def gather_start(tile, s):
        for r in range(MOE_TILE):
            _row_copy(x_ref, perm_ref[(tile + 1) * MOE_TILE + r], xbuf.at[s], r,
                      gsem.at[s]).start(priority=r % 2)

    def scatter_start(tile, s):
        for r in range(MOE_TILE):
            _row_copy(ybuf.at[s], r, out_ref, perm_ref[(tile + 1) * MOE_TILE + r],
                      ssem.at[s]).start(priority=r % 2)

    @pl.when(i == 0)
    def _():
        ybuf[...] = jnp.zeros_like(ybuf)
        for c in range(PAD_ROWS // MOE_TILE):
            pltpu.make_async_copy(ybuf.at[0], out_ref.at[pl.ds(n + c * MOE_TILE, MOE_TILE), :],
                                  zsem).start()
        for c in range(PAD_ROWS // MOE_TILE):
            pltpu.make_async_copy(ybuf.at[0], out_ref.at[pl.ds(n + c * MOE_TILE, MOE_TILE), :],
                                  zsem).wait()
        for s in range(PIPE - 1):
            pltpu.make_async_copy(
                ybuf.at[s], out_ref.at[pl.ds(n + PAD_ROWS + (s + 1) * MOE_TILE, MOE_TILE), :],
                ssem.at[s]).start()
        gather_start(0, 0)
        gather_start(1, 1)

    @pl.when((flag & 2) != 0)
    def _():
        w1_scr[:, 0 * EXPERT_FF:1 * EXPERT_FF] = wg_lo[0, 0].astype(BF16)
        w1_scr[:, 1 * EXPERT_FF:2 * EXPERT_FF] = wu_lo[0, 0].astype(BF16)
        w1_scr[:, 2 * EXPERT_FF:3 * EXPERT_FF] = wg_hi[0, 0].astype(BF16)
        w1_scr[:, 3 * EXPERT_FF:4 * EXPERT_FF] = wu_hi[0, 0].astype(BF16)
        wd_scr[0:EXPERT_FF, :] = wd_lo[0, 0].astype(BF16)
        wd_scr[EXPERT_FF:, :] = wd_hi[0, 0].astype(BF16)

    def tile_body(slot, other):
        gather_wait(slot)
        scatter_wait(slot)
        gather_start(i + 2, other)
        scatter_start(i - 1, other)
        xs = xbuf[slot]
        x = xs[:, 0:D_MODEL]
        g_lo = xs[:, D_MODEL:D_MODEL + 1]
        g_hi = xs[:, D_MODEL + 1:D_MODEL + 2]
        hc = jnp.dot(x.astype(BF16), w1_scr[...], preferred_element_type=F32)
        hg_lo, hu_lo = hc[:, 0:EXPERT_FF], hc[:, EXPERT_FF:2 * EXPERT_FF]
        hg_hi, hu_hi = hc[:, 2 * EXPERT_FF:3 * EXPERT_FF], hc[:, 3 * EXPERT_FF:]
        h_lo = hg_lo * _sigmoid(hg_lo) * hu_lo * g_lo
        h_hi = hg_hi * _sigmoid(hg_hi) * hu_hi * g_hi
        h = jnp.concatenate([h_lo, h_hi], axis=-1).astype(BF16)
        y = jnp.dot(h, wd_scr[...], preferred_element_type=F32)
        ybuf[slot] = _layer_norm(ALPHA * x + y, ln_ref[0, 0:1, :], ln_ref[0, 1:2, :])

    for k in range(PIPE):
        @pl.when(((flag & 1) != 0) & (slot == k))
        def _():
            tile_body(k, (k + 2) % PIPE)

    @pl.when((flag & 4) != 0)
    def _():
        scatter_start(i - 1, other)
        for s in (slot, nxt, other):
            scatter_wait(s)
        for s in (slot, nxt):
            gather_wait(s)


def _layer_spec(shape, layer, n_grid, single_buffer=False):
    idx = (layer,) + (0,) * len(shape)
    if n_grid == 1:
        index_map = lambda i: idx
    else:
        index_map = lambda i, j: idx
    if single_buffer:
        return pl.BlockSpec((1,) + shape, index_map, pipeline_mode=pl.Buffered(1))
    return pl.BlockSpec((1,) + shape, index_map)


def _prompt_mixer(x, w_in, w_out, params, ln, wr, rb, layer, batch, seq, n_rows_out):
    n_t = seq // PROMPT_TILE
    n_real = batch * n_t
    last = n_real - 1
    row_in = pl.BlockSpec((PROMPT_TILE, D_MODEL), lambda i: (jnp.minimum(i, last), 0))
    row_out = pl.BlockSpec((PROMPT_TILE, ROW_W), lambda i: (i, 0))
    route_out = pl.BlockSpec((SUBLANES, PROMPT_TILE), lambda i: (0, jnp.minimum(i, last)))
    full1 = lambda shape: pl.BlockSpec(shape, lambda i: (0, 0))
    seq_of = lambda i: jnp.minimum(i, last) // n_t
    return pl.pallas_call(
        functools.partial(_prompt_mixer_kernel, n_real, n_t),
        grid=(n_rows_out // PROMPT_TILE,),
        in_specs=[row_in,
                  _layer_spec((D_MODEL, N_IN), layer, 1, True),
                  _layer_spec((D_MODEL, D_MODEL), layer, 1, True),
                  _layer_spec((8, 512), layer, 1), _layer_spec((2, D_MODEL), layer, 1),
                  full1((2 * N_EXPERTS, D_MODEL)), full1((N_EXPERTS, 1))],
        out_specs=[row_out, route_out,
                   pl.BlockSpec((1, HEADS, HEAD_DIM, HEAD_DIM), lambda i: (seq_of(i), 0, 0, 0)),
                   pl.BlockSpec((1, CONV_K - 1, CONV_DIM), lambda i: (seq_of(i), 0, 0))],
        out_shape=[jax.ShapeDtypeStruct((n_rows_out, ROW_W), F32),
                   jax.ShapeDtypeStruct((SUBLANES, batch * seq), F32),
                   jax.ShapeDtypeStruct((batch, HEADS, HEAD_DIM, HEAD_DIM), F32),
                   jax.ShapeDtypeStruct((batch, CONV_K - 1, CONV_DIM), F32)],
        scratch_shapes=[pltpu.VMEM((D_MODEL, N_IN), BF16), pltpu.VMEM((D_MODEL, D_MODEL), BF16),
                        pltpu.VMEM((PROMPT_TILE, N_IN), F32)]
        + [pltpu.VMEM((PROMPT_TILE, 512), F32)] * 5
        + [pltpu.VMEM((HEADS, HEAD_DIM, HEAD_DIM), F32)] * 2
        + [pltpu.VMEM((SUBLANES, CONV_DIM), F32)],
        compiler_params=pltpu.CompilerParams(
            dimension_semantics=("arbitrary",), vmem_limit_bytes=VMEM_LIMIT),
        name="prompt_mixer",
    )(x, w_in, w_out, params, ln, wr, rb)


def _sample_mixer(x, x_row0, x1_all, s0, cbuf, w_in, w_out, params, ln, wr, rb, layer, batch,
                  n_valid):
    tile = SAMPLE_SEQS * SAMPLE_ROWS
    n = batch * SAMPLE_ROWS
    x1_row0 = x1_all.shape[0] - PAD_ROWS - n
    in_blk0, out_blk0 = x_row0 // tile, x1_row0 // tile
    row_in = pl.BlockSpec((tile, D_MODEL), lambda i: (in_blk0 + i, 0))
    row_out = pl.BlockSpec((tile, ROW_W), lambda i: (out_blk0 + i, 0))
    route_out = pl.BlockSpec((SUBLANES, tile), lambda i: (0, i))
    st_spec = pl.BlockSpec((1, SAMPLE_SEQS, HEADS, HEAD_DIM, HEAD_DIM),
                           lambda i: (layer, i, 0, 0, 0))
    cb_spec = pl.BlockSpec((1, SAMPLE_SEQS, CONV_K - 1, CONV_DIM), lambda i: (layer, i, 0, 0))
    st_out = pl.BlockSpec((SAMPLE_SEQS, HEADS, HEAD_DIM, HEAD_DIM), lambda i: (i, 0, 0, 0))
    cb_out = pl.BlockSpec((SAMPLE_SEQS, CONV_K - 1, CONV_DIM), lambda i: (i, 0, 0))
    full1 = lambda shape: pl.BlockSpec(shape, lambda i: (0, 0))
    return pl.pallas_call(
        functools.partial(_sample_mixer_kernel, n_valid),
        grid=(batch // SAMPLE_SEQS,),
        in_specs=[row_in, st_spec, cb_spec,
                  _layer_spec((D_MODEL, N_IN), layer, 1, True),
                  _layer_spec((D_MODEL, D_MODEL), layer, 1, True),
                  _layer_spec((8, 512), layer, 1), _layer_spec((2, D_MODEL), layer, 1),
                  full1((2 * N_EXPERTS, D_MODEL)), full1((N_EXPERTS, 1)),
                  pl.BlockSpec(memory_space=pl.ANY)],
        out_specs=[row_out, route_out, st_out, cb_out],
        input_output_aliases={9: 0},
        out_shape=[jax.ShapeDtypeStruct(x1_all.shape, F32),
                   jax.ShapeDtypeStruct((SUBLANES, n), F32),
                   jax.ShapeDtypeStruct((batch, HEADS, HEAD_DIM, HEAD_DIM), F32),
                   jax.ShapeDtypeStruct((batch, CONV_K - 1, CONV_DIM), F32)],
        scratch_shapes=[pltpu.VMEM((D_MODEL, N_IN), BF16), pltpu.VMEM((D_MODEL, D_MODEL), BF16),
                        pltpu.VMEM((tile, N_IN), F32)] + [pltpu.VMEM((tile, 512), F32)] * 5,
        compiler_params=pltpu.CompilerParams(
            dimension_semantics=("arbitrary",), vmem_limit_bytes=VMEM_LIMIT),
        name="sample_mixer",
    )(x, s0, cbuf, w_in, w_out, params, ln, wr, rb, x1_all)


def _plan(cls2d):
    n_rows = cls2d.shape[0]
    return pl.pallas_call(
        _plan_kernel,
        out_shape=[jax.ShapeDtypeStruct((n_rows, LANES), I32),
                   jax.ShapeDtypeStruct((SUBLANES, LANES), F32)],
        name="moe_plan",
    )(cls2d)


def _perm(n, pos, cstart, ccnt, cpad, n_used, n_tiles):
    smem = pl.BlockSpec(memory_space=pltpu.SMEM)
    return pl.pallas_call(
        functools.partial(_perm_kernel, n),
        in_specs=[smem] * 5, out_specs=smem,
        out_shape=jax.ShapeDtypeStruct(((n_tiles + PIPE) * MOE_TILE,), I32),
        name="moe_perm",
    )(pos, cstart, ccnt, cpad, n_used.reshape(1))


def _experts(n, perm, elo, ehi, flag, x1_all, w_gate, w_up, w_down, ln, layer, n_steps):
    def w_spec(shape, table_idx):
        def index_map(i, perm_r, elo_r, ehi_r, flag_r):
            return (layer, (elo_r, ehi_r)[table_idx][i], 0, 0)
        return pl.BlockSpec((1, 1) + shape, index_map)

    up = (D_MODEL, EXPERT_FF)
    down = (EXPERT_FF, D_MODEL)
    any_spec = pl.BlockSpec(memory_space=pl.ANY)
    return pl.pallas_call(
        functools.partial(_expert_kernel, n),
        grid_spec=pltpu.PrefetchScalarGridSpec(
            num_scalar_prefetch=4, grid=(n_steps,),
            in_specs=[any_spec,
                      w_spec(up, 0), w_spec(up, 0), w_spec(down, 0),
                      w_spec(up, 1), w_spec(up, 1), w_spec(down, 1),
                      pl.BlockSpec((1, 2, D_MODEL), lambda i, p, e0, e1, f: (layer, 0, 0))],
            out_specs=any_spec,
            scratch_shapes=[pltpu.VMEM((PIPE, MOE_TILE, ROW_W), F32),
                            pltpu.VMEM((PIPE, MOE_TILE, D_MODEL), F32),
                            pltpu.VMEM((D_MODEL, 4 * EXPERT_FF), BF16),
                            pltpu.VMEM((2 * EXPERT_FF, D_MODEL), BF16),
                            pltpu.SemaphoreType.DMA((PIPE,)), pltpu.SemaphoreType.DMA((PIPE,)),
                            pltpu.SemaphoreType.DMA(())]),
        out_shape=jax.ShapeDtypeStruct((n + PAD_ROWS + PIPE * MOE_TILE, D_MODEL), F32),
        compiler_params=pltpu.CompilerParams(
            dimension_semantics=("arbitrary",), vmem_limit_bytes=VMEM_LIMIT),
        name="moe_experts",
    )(perm, elo, ehi, flag, x1_all, w_gate, w_up, w_down, w_gate, w_up, w_down, ln)


def _moe(x1_all, route_p, route_s, w_gate, w_up, w_down, ln2, layer):
    n = x1_all.shape[0] - PAD_ROWS
    n_tiles = n // MOE_TILE + N_CLASSES
    n_steps = n_tiles + 1
    cls2d = jnp.concatenate([route_p[0], route_s[0]]).reshape(n // LANES, LANES)
    pos2d, stats = _plan(cls2d)

    cnt = stats[0, :N_CLASSES].astype(I32)
    tiles = stats[1, :N_CLASSES].astype(I32)
    first_tile = stats[2, :N_CLASSES].astype(I32)
    end_tile = first_tile + tiles
    n_used = end_tile[N_CLASSES - 1]
    step = jnp.arange(n_steps, dtype=I32)
    tcls = jnp.sum((end_tile[None, :] <= jnp.minimum(step, n_used - 1)[:, None]).astype(I32), axis=1)
    valid = step < n_used
    changed = jnp.concatenate([jnp.ones((1,), jnp.bool_), tcls[1:] != tcls[:-1]]) & valid
    flag = valid.astype(I32) + 2 * changed.astype(I32) + 4 * (step == n_used).astype(I32)
    group = tcls // N_PAIRS
    pair = tcls % N_PAIRS
    elo = group * PER_GROUP + jnp.asarray(PAIR_LO, I32)[pair]
    ehi = group * PER_GROUP + jnp.asarray(PAIR_HI, I32)[pair]

    perm = _perm(n, pos2d.reshape(n), first_tile * MOE_TILE, cnt, tiles * MOE_TILE - cnt, n_used,
                 n_tiles)
    return _experts(n, perm, elo, ehi, flag, x1_all, w_gate, w_up, w_down, ln2, layer, n_steps)


def kernel(x_prompt, x_sample, state_hgrn, state_conv, w_in, w_out, lower_bounds, hgrn_norm_g,
           conv_w, conv_norm_g, ln1_g, ln1_b, ln2_g, ln2_b, w_router, router_bias,
           w_gate, w_up, w_down):
    batch, seq, _ = x_prompt.shape
    dec_batch, dec_seq, _ = x_sample.shape
    assert seq % PROMPT_TILE == 0 and dec_batch % SAMPLE_SEQS == 0
    assert CONV_K - 1 <= dec_seq <= SAMPLE_ROWS
    assert (batch * seq + dec_batch * SAMPLE_ROWS) % MOE_TILE == 0

    lb = jnp.cumsum(jax.nn.softmax(lower_bounds.astype(F32), axis=0), axis=0)
    lb = lb - lb[0:1]
    params = jnp.stack([jnp.log(lb), jnp.log1p(-lb), 1.0 - lb, hgrn_norm_g, conv_norm_g,
                        conv_w[:, 0], conv_w[:, 1], conv_w[:, 2]], axis=1)
    ln1 = jnp.stack([ln1_g, ln1_b], axis=1)
    ln2 = jnp.stack([ln2_g, ln2_b], axis=1)
    wr_hi = w_router.astype(BF16)
    wr_lo = (w_router - wr_hi.astype(F32)).astype(BF16)
    wr = jnp.concatenate([wr_hi.T, wr_lo.T], axis=0)
    rb = router_bias.astype(F32).reshape(N_EXPERTS, 1)

    n_p, n_s = batch * seq, dec_batch * SAMPLE_ROWS
    n = n_p + n_s
    xp = x_prompt.reshape(n_p, D_MODEL)
    xs = jnp.pad(x_sample, ((0, 0), (0, SAMPLE_ROWS - dec_seq), (0, 0))).reshape(n_s, D_MODEL)
    xs_row0 = 0

    s_p, b_p, s_s, b_s = [], [], [], []
    for l in range(DEPTH):
        x1_all, route_p, s_l, b_l = _prompt_mixer(xp, w_in, w_out, params, ln1, wr, rb, l, batch,
                                                  seq, n + PAD_ROWS)
        s_p.append(s_l)
        b_p.append(b_l)
        x1_all, route_s, s_l, b_l = _sample_mixer(xs, xs_row0, x1_all, state_hgrn, state_conv, w_in,
                                                  w_out, params, ln1, wr, rb, l, dec_batch, dec_seq)
        s_s.append(s_l)
        b_s.append(b_l)
        xp = xs = _moe(x1_all, route_p, route_s, w_gate, w_up, w_down, ln2, l)
        xs_row0 = n_p

    y_prompt = xp[:n_p].reshape(batch, seq, D_MODEL)
    y_sample = xs[n_p:n].reshape(dec_batch, SAMPLE_ROWS, D_MODEL)[:, :dec_seq]
    return (y_prompt, y_sample, jnp.stack(s_p), jnp.stack(b_p), jnp.stack(s_s), jnp.stack(b_s))
```

```python
import functools

import jax
import jax.numpy as jnp
from jax import lax
from jax.experimental import pallas as pl
from jax.experimental.pallas import tpu as pltpu

F32 = jnp.float32
BF16 = jnp.bfloat16
I32 = jnp.int32

D_MODEL = 1024
DEPTH = 2
HEADS = 4
HEAD_DIM = 128
HGRN_W = HEADS * HEAD_DIM
CONV_DIM = 512
CONV_GROUPS = 8
CONV_K = 3
N_IN = 7 * 512
N_EXPERTS = 16
N_GROUPS = 4
PER_GROUP = 4
N_PAIRS = 6
N_CLASSES = N_GROUPS * N_PAIRS
PAIR_LO = (0, 0, 0, 1, 1, 2)
PAIR_HI = (1, 2, 3, 2, 3, 3)
EXPERT_FF = 512
ALPHA = (2 * DEPTH) ** 0.25
LN_EPS = 1e-5
RMS_EPS = 1e-6
SAFE_EXPONENT = 80.0

LANES = 128
SUBLANES = 8
CHUNK = 64
PROMPT_TILE = 256
SAMPLE_ROWS = 8
SAMPLE_SEQS = 16
MOE_TILE = 256
ROW_W = D_MODEL + LANES
PAD_ROWS = N_CLASSES * MOE_TILE
PIPE = 3
VMEM_LIMIT = 56 * 1024 * 1024

OFF_Q, OFF_F, OFF_I, OFF_G, OFF_GB, OFF_GC, OFF_CX = (i * 512 for i in range(7))

P_LOGLB, P_LOG1MLB, P_OMLB, P_HNORM, P_CNORM, P_CW0, P_CW1, P_CW2 = range(8)

NT_DIMS = (((1,), (1,)), ((), ()))
TN_DIMS = (((0,), (0,)), ((), ()))


def _sigmoid(x):
    return 1.0 / (1.0 + jnp.exp(-x))


def _seg_cumsum(x, seg):
    row = lax.broadcasted_iota(I32, x.shape, 0)
    pos = row & (seg - 1)
    sh = 1
    while sh < seg:
        x = x + jnp.where(pos >= sh, pltpu.roll(x, sh, axis=0), 0.0)
        sh *= 2
    return x


def _gate_terms(z, p_ref):
    e = jnp.exp(-jnp.abs(z))
    inv = 1.0 / (1.0 + e)
    logsig = jnp.minimum(z, 0.0) - jnp.log(1.0 + e)
    a = p_ref[0, P_LOGLB:P_LOGLB + 1, :]
    b = p_ref[0, P_LOG1MLB:P_LOG1MLB + 1, :] + logsig
    logf = jnp.maximum(a, b) + jnp.log(1.0 + jnp.exp(-jnp.abs(a - b)))
    k = p_ref[0, P_OMLB:P_OMLB + 1, :] * (jnp.where(z >= 0.0, e, 1.0) * inv)
    return logf, k


def _exact_block(qs_c, b_c, k_c, v_c, states, n_valid):
    trow = lax.broadcasted_iota(I32, (SAMPLE_ROWS, 1), 0)
    b_last = b_c[n_valid - 1:n_valid, :]
    qb = (qs_c * jnp.exp(b_c)).astype(BF16)
    kend = (k_c * jnp.exp(b_last - b_c)).astype(BF16)
    dec = jnp.exp(b_last)
    outs, new_states = [], []
    for h in range(HEADS):
        sl = slice(h * HEAD_DIM, (h + 1) * HEAD_DIM)
        st = states[h]
        o_h = lax.dot_general(qb[:, sl], st.astype(BF16), NT_DIMS, preferred_element_type=F32)
        for t in range(n_valid):
            dlt = jnp.minimum(b_c[:, sl] - b_c[t:t + 1, sl], 0.0)
            a_col = jnp.sum(qs_c[:, sl] * k_c[t:t + 1, sl] * jnp.exp(dlt), axis=-1, keepdims=True)
            a_col = jnp.where(trow >= t, a_col, 0.0)
            o_h = o_h + a_col * v_c[t:t + 1, sl]
        d_st = lax.dot_general(v_c[:, sl].astype(BF16), kend[:, sl], TN_DIMS,
                               preferred_element_type=F32)
        outs.append(o_h)
        new_states.append(st * dec[:, sl] + d_st)
    return outs, new_states


def _group_rms(x, n_groups):
    width = x.shape[-1] // n_groups
    x2 = x * x
    outs = []
    for s in range(x.shape[-1] // LANES):
        xs = x[:, s * LANES:(s + 1) * LANES]
        x2s = x2[:, s * LANES:(s + 1) * LANES]
        if width == LANES:
            ms = jnp.sum(x2s, axis=-1, keepdims=True) * (1.0 / width)
            scale = lax.rsqrt(ms + RMS_EPS)
        else:
            lane = lax.broadcasted_iota(I32, xs.shape, 1)
            lo = lane < width
            ms_lo = jnp.sum(jnp.where(lo, x2s, 0.0), axis=-1, keepdims=True) * (1.0 / width)
            ms_hi = jnp.sum(jnp.where(lo, 0.0, x2s), axis=-1, keepdims=True) * (1.0 / width)
            scale = jnp.where(lo, lax.rsqrt(ms_lo + RMS_EPS), lax.rsqrt(ms_hi + RMS_EPS))
        outs.append(xs * scale)
    return jnp.concatenate(outs, axis=-1)


def _layer_norm(r, g, b):
    mu = jnp.mean(r, axis=-1, keepdims=True)
    rc = r - mu
    var = jnp.mean(rc * rc, axis=-1, keepdims=True)
    return rc * lax.rsqrt(var + LN_EPS) * g + b


def _route(x1, wr_ref, rb_ref):
    x_hi = x1.astype(BF16)
    x_lo = (x1 - x_hi.astype(F32)).astype(BF16)
    wr = wr_ref[...]
    r1 = lax.dot_general(wr, x_hi, NT_DIMS, preferred_element_type=F32)
    r2 = lax.dot_general(wr, x_lo, NT_DIMS, preferred_element_type=F32)
    lt = r1[0:N_EXPERTS] + r1[N_EXPERTS:] + r2[0:N_EXPERTS] + r2[N_EXPERTS:] + rb_ref[...]
    lg = [lt[e:e + 1, :] for e in range(N_EXPERTS)]
    mx = lg[0]
    for e in range(1, N_EXPERTS):
        mx = jnp.maximum(mx, lg[e])
    ex = [jnp.exp(l - mx) for l in lg]
    best = None
    gi = None
    for g in range(N_GROUPS):
        a, b, c, d = ex[PER_GROUP * g:PER_GROUP * (g + 1)]
        s = jnp.maximum(jnp.maximum(jnp.maximum(a + b, a + c), jnp.maximum(a + d, b + c)),
                        jnp.maximum(b + d, c + d))
        if g == 0:
            best, gi = s, jnp.zeros(s.shape, I32)
        else:
            upd = s > best
            best = jnp.where(upd, s, best)
            gi = jnp.where(upd, g, gi)
    v = []
    for i in range(PER_GROUP):
        vi = ex[3 * PER_GROUP + i]
        for g in (2, 1, 0):
            vi = jnp.where(gi == g, ex[PER_GROUP * g + i], vi)
        v.append(vi)
    w1, i1 = v[0], jnp.zeros(v[0].shape, I32)
    for i in range(1, PER_GROUP):
        upd = v[i] > w1
        w1 = jnp.where(upd, v[i], w1)
        i1 = jnp.where(upd, i, i1)
    w2, i2 = None, None
    for i in range(PER_GROUP):
        vi = jnp.where(i1 == i, -1.0, v[i])
        if i == 0:
            w2, i2 = vi, jnp.zeros(vi.shape, I32)
        else:
            upd = vi > w2
            w2 = jnp.where(upd, vi, w2)
            i2 = jnp.where(upd, i, i2)
    inv = 1.0 / (w1 + w2)
    first_lo = i1 < i2
    lo = jnp.where(first_lo, i1, i2)
    hi = jnp.where(first_lo, i2, i1)
    g_lo = jnp.where(first_lo, w1, w2) * inv
    g_hi = jnp.where(first_lo, w2, w1) * inv
    pair = jnp.where(lo == 0, 0, jnp.where(lo == 1, 3, 5)) + hi - lo - 1
    return gi * N_PAIRS + pair, g_lo, g_hi


def _post_mix(xt, o, g, yc_in, p_ref, wout_scr, ln_ref, wr_ref, rb_ref, x1_ref, route_ref,
              fill=None):
    rows = xt.shape[0]
    o = _group_rms(o, HEADS) * p_ref[0, P_HNORM:P_HNORM + 1, :]
    o = o * (g * _sigmoid(g))
    yc = _group_rms(yc_in, CONV_GROUPS) * p_ref[0, P_CNORM:P_CNORM + 1, :]
    mix = jnp.concatenate([o, yc], axis=-1).astype(BF16)
    h = jnp.dot(mix, wout_scr[...], preferred_element_type=F32)
    x1 = _layer_norm(ALPHA * xt + h, ln_ref[0, 0:1, :], ln_ref[0, 1:2, :])
    if fill is not None:
        fill()
    cls, g_lo, g_hi = _route(x1, wr_ref, rb_ref)
    x1_ref[:, 0:D_MODEL] = x1
    sub = lax.broadcasted_iota(I32, (LANES, rows), 0)
    gates_t = jnp.where(sub == 0, g_lo, jnp.where(sub == 1, g_hi, 0.0))
    x1_ref[:, D_MODEL:ROW_W] = jnp.transpose(gates_t)
    sub8 = lax.broadcasted_iota(I32, (SUBLANES, rows), 0)
    route_ref[...] = jnp.where(sub8 == 0, cls.astype(F32), 0.0)


def _prompt_mixer_kernel(n_real, n_t, x_ref, xn_ref, *refs):
    i = pl.program_id(0)
    x1_ref = refs[6]

    for parity in range(2):
        @pl.when((i < n_real) & ((i & 1) == parity))
        def _():
            _prompt_tile(i, lax.rem(i, n_t), n_t, parity, x_ref, xn_ref, *refs)

    @pl.when(i >= n_real)
    def _():
        x1_ref[...] = jnp.zeros_like(x1_ref)


def _prompt_tile(i, j, n_t, parity, x_ref, xn_ref, win_ref, wout_ref, p_ref, ln_ref, wr_ref, rb_ref,
                 x1_ref, route_ref, s_ref, buf_ref,
                 win_scr, wout_scr, proj_a_scr, proj_b_scr, qs_scr, b_scr, k_scr, o_scr, yc_scr, st_scr,
                 st0_scr, ubuf_scr):
    proj_scr, proj_next_scr = (proj_a_scr, proj_b_scr) if parity == 0 else (proj_b_scr, proj_a_scr)

    if parity == 0:
        @pl.when(i == 0)
        def _():
            win_scr[...] = win_ref[0].astype(BF16)
            wout_scr[...] = wout_ref[0].astype(BF16)
            proj_scr[...] = jnp.dot(x_ref[...].astype(BF16), win_scr[...],
                                    preferred_element_type=F32)

    @pl.when(j == 0)
    def _():
        st_scr[...] = jnp.zeros_like(st_scr)
        ubuf_scr[...] = jnp.zeros_like(ubuf_scr)

    xn_bf = xn_ref[...].astype(BF16)

    def proj_piece(g):
        cols = slice(g * 512, (g + 1) * 512)
        proj_next_scr[:, cols] = jnp.dot(xn_bf, win_scr[:, cols], preferred_element_type=F32)

    xt = x_ref[...]

    q = proj_scr[:, OFF_Q:OFF_Q + 512]
    qs_scr[...] = q * _sigmoid(q)
    logf, k = _gate_terms(proj_scr[:, OFF_F:OFF_F + 512], p_ref)
    k_scr[...] = k
    b_scr[...] = _seg_cumsum(logf, CHUNK)
    proj_piece(0)

    tri = (lax.broadcasted_iota(I32, (CHUNK, CHUNK), 0)
           >= lax.broadcasted_iota(I32, (CHUNK, CHUNK), 1))

    states = [st_scr[h] for h in range(HEADS)]
    for h in range(HEADS):
        st0_scr[h] = states[h]
    worst = jnp.zeros((1, 512), F32)
    for c in range(PROMPT_TILE // CHUNK):
        rows = slice(c * CHUNK, (c + 1) * CHUNK)
        qs_c = qs_scr[rows, :]
        b_c = b_scr[rows, :]
        k_c = k_scr[rows, :]
        v_c = proj_scr[rows, OFF_I:OFF_I + 512].astype(BF16)
        b_mid = b_c[CHUNK // 2 - 1:CHUNK // 2, :]
        b_last = b_c[CHUNK - 1:CHUNK, :]
        worst = jnp.maximum(worst, jnp.maximum(-b_mid, b_mid - b_last))
        qd = (qs_c * jnp.exp(b_c - b_mid)).astype(BF16)
        kd = (k_c * jnp.exp(b_mid - b_c)).astype(BF16)
        qb = (qs_c * jnp.exp(b_c)).astype(BF16)
        kend = (k_c * jnp.exp(b_last - b_c)).astype(BF16)
        dec = jnp.exp(b_last)
        for h in range(HEADS):
            sl = slice(h * HEAD_DIM, (h + 1) * HEAD_DIM)
            att = lax.dot_general(qd[:, sl], kd[:, sl], NT_DIMS, preferred_element_type=F32)
            att = jnp.where(tri, att, 0.0).astype(BF16)
            st = states[h]
            o_h = jnp.dot(att, v_c[:, sl], preferred_element_type=F32)
            o_h = o_h + lax.dot_general(qb[:, sl], st.astype(BF16), NT_DIMS,
                                        preferred_element_type=F32)
            o_scr[rows, sl] = o_h
            d_st = lax.dot_general(v_c[:, sl], kend[:, sl], TN_DIMS, preferred_element_type=F32)
            states[h] = st * dec[:, sl] + d_st
        proj_piece(c + 1)
    for h in range(HEADS):
        st_scr[h] = states[h]

    u = proj_scr[:, OFF_GC:OFF_GC + 512] * proj_scr[:, OFF_CX:OFF_CX + 512]
    row = lax.broadcasted_iota(I32, u.shape, 0)
    prev2 = ubuf_scr[SUBLANES - 2:SUBLANES - 1, :]
    prev1 = ubuf_scr[SUBLANES - 1:SUBLANES, :]
    u1 = jnp.where(row == 0, prev1, pltpu.roll(u, 1, axis=0))
    u2 = jnp.where(row == 0, prev2, jnp.where(row == 1, prev1, pltpu.roll(u, 2, axis=0)))
    y = (p_ref[0, P_CW0:P_CW0 + 1, :] * u2 + p_ref[0, P_CW1:P_CW1 + 1, :] * u1
         + p_ref[0, P_CW2:P_CW2 + 1, :] * u)
    ubuf_scr[...] = u[PROMPT_TILE - SUBLANES:, :]
    yc_scr[...] = proj_scr[:, OFF_GB:OFF_GB + 512] * y
    proj_piece(5)

    def finish(x_tile, fill=None):
        _post_mix(x_tile, o_scr[...], proj_scr[:, OFF_G:OFF_G + 512], yc_scr[...], p_ref, wout_scr,
                  ln_ref, wr_ref, rb_ref, x1_ref, route_ref, fill)

    finish(xt, lambda: proj_piece(6))

    @pl.when(jnp.logical_not(jnp.max(worst) <= SAFE_EXPONENT))
    def _():
        for h in range(HEADS):
            st_scr[h] = st0_scr[h]

        def block_body(blk, carry):
            r0 = pl.multiple_of(blk * SAMPLE_ROWS, SAMPLE_ROWS)
            rows = pl.ds(r0, SAMPLE_ROWS)
            b_blk = b_scr[rows, :]
            b_prev = b_scr[pl.ds(jnp.maximum(r0 - 1, 0), 1), :]
            b_prev = jnp.where((r0 & (CHUNK - 1)) == 0, 0.0, b_prev)
            outs, new_states = _exact_block(
                qs_scr[rows, :], b_blk - b_prev, k_scr[rows, :], proj_scr[rows, OFF_I:OFF_I + 512],
                [st_scr[h] for h in range(HEADS)], SAMPLE_ROWS)
            for h in range(HEADS):
                o_scr[rows, h * HEAD_DIM:(h + 1) * HEAD_DIM] = outs[h]
                st_scr[h] = new_states[h]
            return carry

        lax.fori_loop(0, PROMPT_TILE // SAMPLE_ROWS, block_body, 0)
        finish(x_ref[...])

    @pl.when(j == n_t - 1)
    def _():
        for h in range(HEADS):
            s_ref[0, h] = jnp.transpose(st_scr[h])
        buf_ref[0] = ubuf_scr[SUBLANES - (CONV_K - 1):, :]


def _sample_mixer_kernel(n_valid, x_ref, s0_ref, cbuf_ref, win_ref, wout_ref, p_ref, ln_ref,
                         wr_ref, rb_ref, x1_alias_ref,
                         x1_ref, route_ref, s_ref, buf_ref,
                         win_scr, wout_scr, proj_scr, qs_scr, b_scr, k_scr, o_scr, y_scr):
    tile = SAMPLE_SEQS * SAMPLE_ROWS

    @pl.when(pl.program_id(0) == 0)
    def _():
        win_scr[...] = win_ref[0].astype(BF16)
        wout_scr[...] = wout_ref[0].astype(BF16)

    xt = x_ref[...]
    proj_scr[...] = jnp.dot(xt.astype(BF16), win_scr[...], preferred_element_type=F32)

    q = proj_scr[:, OFF_Q:OFF_Q + 512]
    qs_scr[...] = q * _sigmoid(q)
    logf, k = _gate_terms(proj_scr[:, OFF_F:OFF_F + 512], p_ref)
    valid = (lax.broadcasted_iota(I32, (tile, 512), 0) & (SAMPLE_ROWS - 1)) < n_valid
    k_scr[...] = jnp.where(valid, k, 0.0)
    b_scr[...] = _seg_cumsum(jnp.where(valid, logf, 0.0), SAMPLE_ROWS)

    urow = lax.broadcasted_iota(I32, (SAMPLE_ROWS, 512), 0)

    def seq_body(s, carry):
        r0 = pl.multiple_of(s * SAMPLE_ROWS, SAMPLE_ROWS)
        rows = pl.ds(r0, SAMPLE_ROWS)
        qs_c = qs_scr[rows, :]
        b_c = b_scr[rows, :]
        k_c = k_scr[rows, :]
        v_c = proj_scr[rows, OFF_I:OFF_I + 512]
        states = [jnp.transpose(s0_ref[0, s, h]) for h in range(HEADS)]
        outs, states = _exact_block(qs_c, b_c, k_c, v_c, states, n_valid)
        for h in range(HEADS):
            o_scr[rows, h * HEAD_DIM:(h + 1) * HEAD_DIM] = outs[h]
            s_ref[s, h] = jnp.transpose(states[h])
        u = proj_scr[rows, OFF_GC:OFF_GC + 512] * proj_scr[rows, OFF_CX:OFF_CX + 512]
        prev2 = cbuf_ref[0, s, 0:1, :]
        prev1 = cbuf_ref[0, s, 1:2, :]
        u1 = jnp.where(urow == 0, prev1, pltpu.roll(u, 1, axis=0))
        u2 = jnp.where(urow == 0, prev2, jnp.where(urow == 1, prev1, pltpu.roll(u, 2, axis=0)))
        y_scr[rows, :] = (p_ref[0, P_CW0:P_CW0 + 1, :] * u2 + p_ref[0, P_CW1:P_CW1 + 1, :] * u1
                          + p_ref[0, P_CW2:P_CW2 + 1, :] * u)
        buf_ref[s] = u[n_valid - (CONV_K - 1):n_valid, :]
        return carry

    lax.fori_loop(0, SAMPLE_SEQS, seq_body, 0)

    yc_in = proj_scr[:, OFF_GB:OFF_GB + 512] * y_scr[...]
    _post_mix(xt, o_scr[...], proj_scr[:, OFF_G:OFF_G + 512], yc_in, p_ref, wout_scr, ln_ref,
              wr_ref, rb_ref, x1_ref, route_ref)


def _plan_kernel(cls_ref, pos_ref, stats_ref):
    n_rows = cls_ref.shape[0]
    cls = cls_ref[...]
    upper = (lax.broadcasted_iota(I32, (LANES, LANES), 0)
             < lax.broadcasted_iota(I32, (LANES, LANES), 1)).astype(BF16)
    lower = (lax.broadcasted_iota(I32, (n_rows, n_rows), 1)
             < lax.broadcasted_iota(I32, (n_rows, n_rows), 0)).astype(BF16)
    lane = lax.broadcasted_iota(I32, (n_rows, LANES), 1)
    row_tot = jnp.zeros((n_rows, LANES), F32)
    for c in range(N_CLASSES):
        oh = jnp.where(cls == c, 1.0, 0.0)
        row_tot = jnp.where(lane == c, jnp.sum(oh, axis=1, keepdims=True), row_tot)
    before = jnp.dot(lower, row_tot.astype(BF16), preferred_element_type=F32)
    cnt = jnp.sum(row_tot, axis=0, keepdims=True)
    tiles = jnp.floor((cnt + (MOE_TILE - 1)) * (1.0 / MOE_TILE))
    first_tile = jnp.dot(jnp.broadcast_to(tiles, (SUBLANES, LANES)).astype(BF16), upper,
                         preferred_element_type=F32)[0:1]
    base = before + first_tile * MOE_TILE
    pos = jnp.zeros((n_rows, LANES), F32)
    for c in range(N_CLASSES):
        oh = jnp.where(cls == c, 1.0, 0.0)
        local = jnp.dot(oh.astype(BF16), upper, preferred_element_type=F32)
        pos = pos + oh * (base[:, c:c + 1] + local)
    pos_ref[...] = pos.astype(I32)
    sub = lax.broadcasted_iota(I32, (SUBLANES, LANES), 0)
    stats_ref[...] = jnp.where(sub == 0, cnt, jnp.where(sub == 1, tiles,
                                                        jnp.where(sub == 2, first_tile, 0.0)))


def _row_copy(src_ref, src_row, dst_ref, dst_row, sem):
    return pltpu.make_async_copy(src_ref.at[pl.ds(src_row, 1), :],
                                 dst_ref.at[pl.ds(dst_row, 1), :], sem)


def _perm_kernel(n, pos_ref, cstart_ref, ccnt_ref, cpad_ref, nused_ref, perm_ref):
    def stand_in_body(r, carry):
        perm_ref[r] = n + PAD_ROWS + r
        return carry

    lax.fori_loop(0, MOE_TILE, stand_in_body, 0, unroll=8)

    def unused_body(p, carry):
        perm_ref[p] = n
        return carry

    lax.fori_loop((nused_ref[0] + 1) * MOE_TILE, perm_ref.shape[0], unused_body, 0)

    def cls_body(c, k):
        base = MOE_TILE + cstart_ref[c] + ccnt_ref[c]

        def pad_body(q, carry):
            perm_ref[base + q] = n + k + q
            return carry

        lax.fori_loop(0, cpad_ref[c], pad_body, 0)
        return k + cpad_ref[c]

    lax.fori_loop(0, N_CLASSES, cls_body, 0)

    def tok_body(t, carry):
        perm_ref[MOE_TILE + pos_ref[t]] = t
        return carry

    lax.fori_loop(0, n, tok_body, 0, unroll=8)


def _expert_kernel(n, perm_ref, elo_ref, ehi_ref, flag_ref,
                   x_ref, wg_lo, wu_lo, wd_lo, wg_hi, wu_hi, wd_hi, ln_ref, out_ref,
                   xbuf, ybuf, w1_scr, wd_scr, gsem, ssem, zsem):
    i = pl.program_id(0)
    flag = flag_ref[i]
    slot = lax.rem(i, PIPE)
    nxt = lax.rem(i + 1, PIPE)
    other = lax.rem(i + 2, PIPE)

    def gather_wait(s):
        pltpu.make_async_copy(x_ref.at[pl.ds(0, MOE_TILE), :], xbuf.at[s], gsem.at[s]).wait()

    def scatter_wait(s):
        pltpu.make_async_copy(ybuf.at[s], out_ref.at[pl.ds(0, MOE_TILE), :], ssem.at[s]).wait()

    def gather_start(tile, s):
        for r in range(MOE_TILE):
            _row_copy(x_ref, perm_ref[(tile + 1) * MOE_TILE + r], xbuf.at[s], r,
                      gsem.at[s]).start(priority=r % 2)

    def scatter_start(tile, s):
        for r in range(MOE_TILE):
            _row_copy(ybuf.at[s], r, out_ref, perm_ref[(tile + 1) * MOE_TILE + r],
                      ssem.at[s]).start(priority=r % 2)

    @pl.when(i == 0)
    def _():
        ybuf[...] = jnp.zeros_like(ybuf)
        for c in range(PAD_ROWS // MOE_TILE):
            pltpu.make_async_copy(ybuf.at[0], out_ref.at[pl.ds(n + c * MOE_TILE, MOE_TILE), :],
                                  zsem).start()
        for c in range(PAD_ROWS // MOE_TILE):
            pltpu.make_async_copy(ybuf.at[0], out_ref.at[pl.ds(n + c * MOE_TILE, MOE_TILE), :],
                                  zsem).wait()
        for s in range(PIPE - 1):
            pltpu.make_async_copy(
                ybuf.at[s], out_ref.at[pl.ds(n + PAD_ROWS + (s + 1) * MOE_TILE, MOE_TILE), :],
                ssem.at[s]).start()
        gather_start(0, 0)
        gather_start(1, 1)

    @pl.when((flag & 2) != 0)
    def _():
        w1_scr[:, 0 * EXPERT_FF:1 * EXPERT_FF] = wg_lo[0, 0].astype(BF16)
        w1_scr[:, 1 * EXPERT_FF:2 * EXPERT_FF] = wu_lo[0, 0].astype(BF16)
        w1_scr[:, 2 * EXPERT_FF:3 * EXPERT_FF] = wg_hi[0, 0].astype(BF16)
        w1_scr[:, 3 * EXPERT_FF:4 * EXPERT_FF] = wu_hi[0, 0].astype(BF16)
        wd_scr[0:EXPERT_FF, :] = wd_lo[0, 0].astype(BF16)
        wd_scr[EXPERT_FF:, :] = wd_hi[0, 0].astype(BF16)

    def tile_body(slot, other):
        gather_wait(slot)
        scatter_wait(slot)
        gather_start(i + 2, other)
        scatter_start(i - 1, other)
        xs = xbuf[slot]
        x = xs[:, 0:D_MODEL]
        g_lo = xs[:, D_MODEL:D_MODEL + 1]
        g_hi = xs[:, D_MODEL + 1:D_MODEL + 2]
        hc = jnp.dot(x.astype(BF16), w1_scr[...], preferred_element_type=F32)
        hg_lo, hu_lo = hc[:, 0:EXPERT_FF], hc[:, EXPERT_FF:2 * EXPERT_FF]
        hg_hi, hu_hi = hc[:, 2 * EXPERT_FF:3 * EXPERT_FF], hc[:, 3 * EXPERT_FF:]
        h_lo = hg_lo * _sigmoid(hg_lo) * hu_lo * g_lo
        h_hi = hg_hi * _sigmoid(hg_hi) * hu_hi * g_hi
        h = jnp.concatenate([h_lo, h_hi], axis=-1).astype(BF16)
        y = jnp.dot(h, wd_scr[...], preferred_element_type=F32)
        ybuf[slot] = _layer_norm(ALPHA * x + y, ln_ref[0, 0:1, :], ln_ref[0, 1:2, :])

    for k in range(PIPE):
        @pl.when(((flag & 1) != 0) & (slot == k))
        def _():
            tile_body(k, (k + 2) % PIPE)

    @pl.when((flag & 4) != 0)
    def _():
        scatter_start(i - 1, other)
        for s in (slot, nxt, other):
            scatter_wait(s)
        for s in (slot, nxt):
            gather_wait(s)


def _layer_spec(shape, layer, n_grid, single_buffer=False):
    idx = (layer,) + (0,) * len(shape)
    if n_grid == 1:
        index_map = lambda i: idx
    else:
        index_map = lambda i, j: idx
    if single_buffer:
        return pl.BlockSpec((1,) + shape, index_map, pipeline_mode=pl.Buffered(1))
    return pl.BlockSpec((1,) + shape, index_map)


def _prompt_mixer(x, w_in, w_out, params, ln, wr, rb, layer, batch, seq, n_rows_out):
    n_t = seq // PROMPT_TILE
    n_real = batch * n_t
    last = n_real - 1
    row_in = pl.BlockSpec((PROMPT_TILE, D_MODEL), lambda i: (jnp.minimum(i, last), 0))
    row_next = pl.BlockSpec((PROMPT_TILE, D_MODEL), lambda i: (jnp.minimum(i + 1, last), 0))
    row_out = pl.BlockSpec((PROMPT_TILE, ROW_W), lambda i: (i, 0))
    route_out = pl.BlockSpec((SUBLANES, PROMPT_TILE), lambda i: (0, jnp.minimum(i, last)))
    full1 = lambda shape: pl.BlockSpec(shape, lambda i: (0, 0))
    seq_of = lambda i: jnp.minimum(i, last) // n_t
    return pl.pallas_call(
        functools.partial(_prompt_mixer_kernel, n_real, n_t),
        grid=(n_rows_out // PROMPT_TILE,),
        in_specs=[row_in, row_next,
                  _layer_spec((D_MODEL, N_IN), layer, 1, True),
                  _layer_spec((D_MODEL, D_MODEL), layer, 1, True),
                  _layer_spec((8, 512), layer, 1), _layer_spec((2, D_MODEL), layer, 1),
                  full1((2 * N_EXPERTS, D_MODEL)), full1((N_EXPERTS, 1))],
        out_specs=[row_out, route_out,
                   pl.BlockSpec((1, HEADS, HEAD_DIM, HEAD_DIM), lambda i: (seq_of(i), 0, 0, 0)),
                   pl.BlockSpec((1, CONV_K - 1, CONV_DIM), lambda i: (seq_of(i), 0, 0))],
        out_shape=[jax.ShapeDtypeStruct((n_rows_out, ROW_W), F32),
                   jax.ShapeDtypeStruct((SUBLANES, batch * seq), F32),
                   jax.ShapeDtypeStruct((batch, HEADS, HEAD_DIM, HEAD_DIM), F32),
                   jax.ShapeDtypeStruct((batch, CONV_K - 1, CONV_DIM), F32)],
        scratch_shapes=[pltpu.VMEM((D_MODEL, N_IN), BF16), pltpu.VMEM((D_MODEL, D_MODEL), BF16),
                        pltpu.VMEM((PROMPT_TILE, N_IN), F32), pltpu.VMEM((PROMPT_TILE, N_IN), F32)]
        + [pltpu.VMEM((PROMPT_TILE, 512), F32)] * 5
        + [pltpu.VMEM((HEADS, HEAD_DIM, HEAD_DIM), F32)] * 2
        + [pltpu.VMEM((SUBLANES, CONV_DIM), F32)],
        compiler_params=pltpu.CompilerParams(
            dimension_semantics=("arbitrary",), vmem_limit_bytes=VMEM_LIMIT),
        name="prompt_mixer",
    )(x, x, w_in, w_out, params, ln, wr, rb)


def _sample_mixer(x, x_row0, x1_all, s0, cbuf, w_in, w_out, params, ln, wr, rb, layer, batch,
                  n_valid):
    tile = SAMPLE_SEQS * SAMPLE_ROWS
    n = batch * SAMPLE_ROWS
    x1_row0 = x1_all.shape[0] - PAD_ROWS - n
    in_blk0, out_blk0 = x_row0 // tile, x1_row0 // tile
    row_in = pl.BlockSpec((tile, D_MODEL), lambda i: (in_blk0 + i, 0))
    row_out = pl.BlockSpec((tile, ROW_W), lambda i: (out_blk0 + i, 0))
    route_out = pl.BlockSpec((SUBLANES, tile), lambda i: (0, i))
    st_spec = pl.BlockSpec((1, SAMPLE_SEQS, HEADS, HEAD_DIM, HEAD_DIM),
                           lambda i: (layer, i, 0, 0, 0))
    cb_spec = pl.BlockSpec((1, SAMPLE_SEQS, CONV_K - 1, CONV_DIM), lambda i: (layer, i, 0, 0))
    st_out = pl.BlockSpec((SAMPLE_SEQS, HEADS, HEAD_DIM, HEAD_DIM), lambda i: (i, 0, 0, 0))
    cb_out = pl.BlockSpec((SAMPLE_SEQS, CONV_K - 1, CONV_DIM), lambda i: (i, 0, 0))
    full1 = lambda shape: pl.BlockSpec(shape, lambda i: (0, 0))
    return pl.pallas_call(
        functools.partial(_sample_mixer_kernel, n_valid),
        grid=(batch // SAMPLE_SEQS,),
        in_specs=[row_in, st_spec, cb_spec,
                  _layer_spec((D_MODEL, N_IN), layer, 1, True),
                  _layer_spec((D_MODEL, D_MODEL), layer, 1, True),
                  _layer_spec((8, 512), layer, 1), _layer_spec((2, D_MODEL), layer, 1),
                  full1((2 * N_EXPERTS, D_MODEL)), full1((N_EXPERTS, 1)),
                  pl.BlockSpec(memory_space=pl.ANY)],
        out_specs=[row_out, route_out, st_out, cb_out],
        input_output_aliases={9: 0},
        out_shape=[jax.ShapeDtypeStruct(x1_all.shape, F32),
                   jax.ShapeDtypeStruct((SUBLANES, n), F32),
                   jax.ShapeDtypeStruct((batch, HEADS, HEAD_DIM, HEAD_DIM), F32),
                   jax.ShapeDtypeStruct((batch, CONV_K - 1, CONV_DIM), F32)],
        scratch_shapes=[pltpu.VMEM((D_MODEL, N_IN), BF16), pltpu.VMEM((D_MODEL, D_MODEL), BF16),
                        pltpu.VMEM((tile, N_IN), F32)] + [pltpu.VMEM((tile, 512), F32)] * 5,
        compiler_params=pltpu.CompilerParams(
            dimension_semantics=("arbitrary",), vmem_limit_bytes=VMEM_LIMIT),
        name="sample_mixer",
    )(x, s0, cbuf, w_in, w_out, params, ln, wr, rb, x1_all)


def _plan(cls2d):
    n_rows = cls2d.shape[0]
    return pl.pallas_call(
        _plan_kernel,
        out_shape=[jax.ShapeDtypeStruct((n_rows, LANES), I32),
                   jax.ShapeDtypeStruct((SUBLANES, LANES), F32)],
        name="moe_plan",
    )(cls2d)


def _perm(n, pos, cstart, ccnt, cpad, n_used, n_tiles):
    smem = pl.BlockSpec(memory_space=pltpu.SMEM)
    return pl.pallas_call(
        functools.partial(_perm_kernel, n),
        in_specs=[smem] * 5, out_specs=smem,
        out_shape=jax.ShapeDtypeStruct(((n_tiles + PIPE) * MOE_TILE,), I32),
        name="moe_perm",
    )(pos, cstart, ccnt, cpad, n_used.reshape(1))


def _experts(n, perm, elo, ehi, flag, x1_all, w_gate, w_up, w_down, ln, layer, n_steps):
    def w_spec(shape, table_idx):
        def index_map(i, perm_r, elo_r, ehi_r, flag_r):
            return (layer, (elo_r, ehi_r)[table_idx][i], 0, 0)
        return pl.BlockSpec((1, 1) + shape, index_map)

    up = (D_MODEL, EXPERT_FF)
    down = (EXPERT_FF, D_MODEL)
    any_spec = pl.BlockSpec(memory_space=pl.ANY)
    return pl.pallas_call(
        functools.partial(_expert_kernel, n),
        grid_spec=pltpu.PrefetchScalarGridSpec(
            num_scalar_prefetch=4, grid=(n_steps,),
            in_specs=[any_spec,
                      w_spec(up, 0), w_spec(up, 0), w_spec(down, 0),
                      w_spec(up, 1), w_spec(up, 1), w_spec(down, 1),
                      pl.BlockSpec((1, 2, D_MODEL), lambda i, p, e0, e1, f: (layer, 0, 0))],
            out_specs=any_spec,
            scratch_shapes=[pltpu.VMEM((PIPE, MOE_TILE, ROW_W), F32),
                            pltpu.VMEM((PIPE, MOE_TILE, D_MODEL), F32),
                            pltpu.VMEM((D_MODEL, 4 * EXPERT_FF), BF16),
                            pltpu.VMEM((2 * EXPERT_FF, D_MODEL), BF16),
                            pltpu.SemaphoreType.DMA((PIPE,)), pltpu.SemaphoreType.DMA((PIPE,)),
                            pltpu.SemaphoreType.DMA(())]),
        out_shape=jax.ShapeDtypeStruct((n + PAD_ROWS + PIPE * MOE_TILE, D_MODEL), F32),
        compiler_params=pltpu.CompilerParams(
            dimension_semantics=("arbitrary",), vmem_limit_bytes=VMEM_LIMIT),
        name="moe_experts",
    )(perm, elo, ehi, flag, x1_all, w_gate, w_up, w_down, w_gate, w_up, w_down, ln)


def _moe(x1_all, route_p, route_s, w_gate, w_up, w_down, ln2, layer):
    n = x1_all.shape[0] - PAD_ROWS
    n_tiles = n // MOE_TILE + N_CLASSES
    n_steps = n_tiles + 1
    cls2d = jnp.concatenate([route_p[0], route_s[0]]).reshape(n // LANES, LANES)
    pos2d, stats = _plan(cls2d)

    cnt = stats[0, :N_CLASSES].astype(I32)
    tiles = stats[1, :N_CLASSES].astype(I32)
    first_tile = stats[2, :N_CLASSES].astype(I32)
    end_tile = first_tile + tiles
    n_used = end_tile[N_CLASSES - 1]
    step = jnp.arange(n_steps, dtype=I32)
    tcls = jnp.sum((end_tile[None, :] <= jnp.minimum(step, n_used - 1)[:, None]).astype(I32), axis=1)
    valid = step < n_used
    changed = jnp.concatenate([jnp.ones((1,), jnp.bool_), tcls[1:] != tcls[:-1]]) & valid
    flag = valid.astype(I32) + 2 * changed.astype(I32) + 4 * (step == n_used).astype(I32)
    group = tcls // N_PAIRS
    pair = tcls % N_PAIRS
    elo = group * PER_GROUP + jnp.asarray(PAIR_LO, I32)[pair]
    ehi = group * PER_GROUP + jnp.asarray(PAIR_HI, I32)[pair]

    perm = _perm(n, pos2d.reshape(n), first_tile * MOE_TILE, cnt, tiles * MOE_TILE - cnt, n_used,
                 n_tiles)
    return _experts(n, perm, elo, ehi, flag, x1_all, w_gate, w_up, w_down, ln2, layer, n_steps)


def kernel(x_prompt, x_sample, state_hgrn, state_conv, w_in, w_out, lower_bounds, hgrn_norm_g,
           conv_w, conv_norm_g, ln1_g, ln1_b, ln2_g, ln2_b, w_router, router_bias,
           w_gate, w_up, w_down):
    batch, seq, _ = x_prompt.shape
    dec_batch, dec_seq, _ = x_sample.shape
    assert seq % PROMPT_TILE == 0 and dec_batch % SAMPLE_SEQS == 0
    assert CONV_K - 1 <= dec_seq <= SAMPLE_ROWS
    assert (batch * seq + dec_batch * SAMPLE_ROWS) % MOE_TILE == 0

    lb = jnp.cumsum(jax.nn.softmax(lower_bounds.astype(F32), axis=0), axis=0)
    lb = lb - lb[0:1]
    params = jnp.stack([jnp.log(lb), jnp.log1p(-lb), 1.0 - lb, hgrn_norm_g, conv_norm_g,
                        conv_w[:, 0], conv_w[:, 1], conv_w[:, 2]], axis=1)
    ln1 = jnp.stack([ln1_g, ln1_b], axis=1)
    ln2 = jnp.stack([ln2_g, ln2_b], axis=1)
    wr_hi = w_router.astype(BF16)
    wr_lo = (w_router - wr_hi.astype(F32)).astype(BF16)
    wr = jnp.concatenate([wr_hi.T, wr_lo.T], axis=0)
    rb = router_bias.astype(F32).reshape(N_EXPERTS, 1)

    n_p, n_s = batch * seq, dec_batch * SAMPLE_ROWS
    n = n_p + n_s
    xp = x_prompt.reshape(n_p, D_MODEL)
    xs = jnp.pad(x_sample, ((0, 0), (0, SAMPLE_ROWS - dec_seq), (0, 0))).reshape(n_s, D_MODEL)
    xs_row0 = 0

    s_p, b_p, s_s, b_s = [], [], [], []
    for l in range(DEPTH):
        x1_all, route_p, s_l, b_l = _prompt_mixer(xp, w_in, w_out, params, ln1, wr, rb, l, batch,
                                                  seq, n + PAD_ROWS)
        s_p.append(s_l)
        b_p.append(b_l)
        x1_all, route_s, s_l, b_l = _sample_mixer(xs, xs_row0, x1_all, state_hgrn, state_conv, w_in,
                                                  w_out, params, ln1, wr, rb, l, dec_batch, dec_seq)
        s_s.append(s_l)
        b_s.append(b_l)
        xp = xs = _moe(x1_all, route_p, route_s, w_gate, w_up, w_down, ln2, l)
        xs_row0 = n_p

    y_prompt = xp[:n_p].reshape(batch, seq, D_MODEL)
    y_sample = xs[n_p:n].reshape(dec_batch, SAMPLE_ROWS, D_MODEL)[:, :dec_seq]
    return (y_prompt, y_sample, jnp.stack(s_p), jnp.stack(b_p), jnp.stack(s_s), jnp.stack(b_s))
```

```python
import functools

import jax
import jax.numpy as jnp
from jax import lax
from jax.experimental import pallas as pl
from jax.experimental.pallas import tpu as pltpu

F32 = jnp.float32
BF16 = jnp.bfloat16
I32 = jnp.int32

D_MODEL = 1024
DEPTH = 2
HEADS = 4
HEAD_DIM = 128
HGRN_W = HEADS * HEAD_DIM
CONV_DIM = 512
CONV_GROUPS = 8
CONV_K = 3
N_IN = 7 * 512
N_EXPERTS = 16
N_GROUPS = 4
PER_GROUP = 4
N_PAIRS = 6
N_CLASSES = N_GROUPS * N_PAIRS
PAIR_LO = (0, 0, 0, 1, 1, 2)
PAIR_HI = (1, 2, 3, 2, 3, 3)
EXPERT_FF = 512
ALPHA = (2 * DEPTH) ** 0.25
LN_EPS = 1e-5
RMS_EPS = 1e-6
SAFE_EXPONENT = 80.0

LANES = 128
SUBLANES = 8
CHUNK = 64
PROMPT_TILE = 256
SAMPLE_ROWS = 8
SAMPLE_SEQS = 16
MOE_TILE = 256
ROW_W = D_MODEL + LANES
PAD_ROWS = N_CLASSES * MOE_TILE
PIPE = 3
VMEM_LIMIT = 56 * 1024 * 1024

OFF_Q, OFF_F, OFF_I, OFF_G, OFF_GB, OFF_GC, OFF_CX = (i * 512 for i in range(7))

P_LOGLB, P_LOG1MLB, P_OMLB, P_HNORM, P_CNORM, P_CW0, P_CW1, P_CW2 = range(8)

NT_DIMS = (((1,), (1,)), ((), ()))
TN_DIMS = (((0,), (0,)), ((), ()))


def _sigmoid(x):
    return 1.0 / (1.0 + jnp.exp(-x))


def _seg_cumsum(x, seg):
    row = lax.broadcasted_iota(I32, x.shape, 0)
    pos = row & (seg - 1)
    sh = 1
    while sh < seg:
        x = x + jnp.where(pos >= sh, pltpu.roll(x, sh, axis=0), 0.0)
        sh *= 2
    return x


def _gate_terms(z, p_ref):
    e = jnp.exp(-jnp.abs(z))
    inv = 1.0 / (1.0 + e)
    logsig = jnp.minimum(z, 0.0) - jnp.log(1.0 + e)
    a = p_ref[0, P_LOGLB:P_LOGLB + 1, :]
    b = p_ref[0, P_LOG1MLB:P_LOG1MLB + 1, :] + logsig
    logf = jnp.maximum(a, b) + jnp.log(1.0 + jnp.exp(-jnp.abs(a - b)))
    k = p_ref[0, P_OMLB:P_OMLB + 1, :] * (jnp.where(z >= 0.0, e, 1.0) * inv)
    return logf, k


def _exact_block(qs_c, b_c, k_c, v_c, states, n_valid):
    trow = lax.broadcasted_iota(I32, (SAMPLE_ROWS, 1), 0)
    b_last = b_c[n_valid - 1:n_valid, :]
    qb = (qs_c * jnp.exp(b_c)).astype(BF16)
    kend = (k_c * jnp.exp(b_last - b_c)).astype(BF16)
    dec = jnp.exp(b_last)
    outs, new_states = [], []
    for h in range(HEADS):
        sl = slice(h * HEAD_DIM, (h + 1) * HEAD_DIM)
        st = states[h]
        o_h = lax.dot_general(qb[:, sl], st.astype(BF16), NT_DIMS, preferred_element_type=F32)
        for t in range(n_valid):
            dlt = jnp.minimum(b_c[:, sl] - b_c[t:t + 1, sl], 0.0)
            a_col = jnp.sum(qs_c[:, sl] * k_c[t:t + 1, sl] * jnp.exp(dlt), axis=-1, keepdims=True)
            a_col = jnp.where(trow >= t, a_col, 0.0)
            o_h = o_h + a_col * v_c[t:t + 1, sl]
        d_st = lax.dot_general(v_c[:, sl].astype(BF16), kend[:, sl], TN_DIMS,
                               preferred_element_type=F32)
        outs.append(o_h)
        new_states.append(st * dec[:, sl] + d_st)
    return outs, new_states


def _group_rms(x, n_groups):
    width = x.shape[-1] // n_groups
    x2 = x * x
    outs = []
    for s in range(x.shape[-1] // LANES):
        xs = x[:, s * LANES:(s + 1) * LANES]
        x2s = x2[:, s * LANES:(s + 1) * LANES]
        if width == LANES:
            ms = jnp.sum(x2s, axis=-1, keepdims=True) * (1.0 / width)
            scale = lax.rsqrt(ms + RMS_EPS)
        else:
            lane = lax.broadcasted_iota(I32, xs.shape, 1)
            lo = lane < width
            ms_lo = jnp.sum(jnp.where(lo, x2s, 0.0), axis=-1, keepdims=True) * (1.0 / width)
            ms_hi = jnp.sum(jnp.where(lo, 0.0, x2s), axis=-1, keepdims=True) * (1.0 / width)
            scale = jnp.where(lo, lax.rsqrt(ms_lo + RMS_EPS), lax.rsqrt(ms_hi + RMS_EPS))
        outs.append(xs * scale)
    return jnp.concatenate(outs, axis=-1)


def _layer_norm(r, g, b):
    mu = jnp.mean(r, axis=-1, keepdims=True)
    rc = r - mu
    var = jnp.mean(rc * rc, axis=-1, keepdims=True)
    return rc * lax.rsqrt(var + LN_EPS) * g + b


def _route(x1, wr_ref, rb_ref):
    x_hi = x1.astype(BF16)
    x_lo = (x1 - x_hi.astype(F32)).astype(BF16)
    wr = wr_ref[...]
    r1 = lax.dot_general(wr, x_hi, NT_DIMS, preferred_element_type=F32)
    r2 = lax.dot_general(wr, x_lo, NT_DIMS, preferred_element_type=F32)
    lt = r1[0:N_EXPERTS] + r1[N_EXPERTS:] + r2[0:N_EXPERTS] + r2[N_EXPERTS:] + rb_ref[...]
    lg = [lt[e:e + 1, :] for e in range(N_EXPERTS)]
    mx = lg[0]
    for e in range(1, N_EXPERTS):
        mx = jnp.maximum(mx, lg[e])
    ex = [jnp.exp(l - mx) for l in lg]
    best = None
    gi = None
    for g in range(N_GROUPS):
        a, b, c, d = ex[PER_GROUP * g:PER_GROUP * (g + 1)]
        s = jnp.maximum(jnp.maximum(jnp.maximum(a + b, a + c), jnp.maximum(a + d, b + c)),
                        jnp.maximum(b + d, c + d))
        if g == 0:
            best, gi = s, jnp.zeros(s.shape, I32)
        else:
            upd = s > best
            best = jnp.where(upd, s, best)
            gi = jnp.where(upd, g, gi)
    v = []
    for i in range(PER_GROUP):
        vi = ex[3 * PER_GROUP + i]
        for g in (2, 1, 0):
            vi = jnp.where(gi == g, ex[PER_GROUP * g + i], vi)
        v.append(vi)
    w1, i1 = v[0], jnp.zeros(v[0].shape, I32)
    for i in range(1, PER_GROUP):
        upd = v[i] > w1
        w1 = jnp.where(upd, v[i], w1)
        i1 = jnp.where(upd, i, i1)
    w2, i2 = None, None
    for i in range(PER_GROUP):
        vi = jnp.where(i1 == i, -1.0, v[i])
        if i == 0:
            w2, i2 = vi, jnp.zeros(vi.shape, I32)
        else:
            upd = vi > w2
            w2 = jnp.where(upd, vi, w2)
            i2 = jnp.where(upd, i, i2)
    inv = 1.0 / (w1 + w2)
    first_lo = i1 < i2
    lo = jnp.where(first_lo, i1, i2)
    hi = jnp.where(first_lo, i2, i1)
    g_lo = jnp.where(first_lo, w1, w2) * inv
    g_hi = jnp.where(first_lo, w2, w1) * inv
    pair = jnp.where(lo == 0, 0, jnp.where(lo == 1, 3, 5)) + hi - lo - 1
    return gi * N_PAIRS + pair, g_lo, g_hi


def _post_mix(xt, o, g, yc_in, p_ref, wout_scr, ln_ref, wr_ref, rb_ref, x1_ref, route_ref,
              fill=None):
    rows = xt.shape[0]
    o = _group_rms(o, HEADS) * p_ref[0, P_HNORM:P_HNORM + 1, :]
    o = o * (g * _sigmoid(g))
    yc = _group_rms(yc_in, CONV_GROUPS) * p_ref[0, P_CNORM:P_CNORM + 1, :]
    mix = jnp.concatenate([o, yc], axis=-1).astype(BF16)
    h = jnp.dot(mix, wout_scr[...], preferred_element_type=F32)
    x1 = _layer_norm(ALPHA * xt + h, ln_ref[0, 0:1, :], ln_ref[0, 1:2, :])
    if fill is not None:
        fill()
    cls, g_lo, g_hi = _route(x1, wr_ref, rb_ref)
    x1_ref[:, 0:D_MODEL] = x1
    sub = lax.broadcasted_iota(I32, (LANES, rows), 0)
    gates_t = jnp.where(sub == 0, g_lo, jnp.where(sub == 1, g_hi, 0.0))
    x1_ref[:, D_MODEL:ROW_W] = jnp.transpose(gates_t)
    sub8 = lax.broadcasted_iota(I32, (SUBLANES, rows), 0)
    route_ref[...] = jnp.where(sub8 == 0, cls.astype(F32), 0.0)


def _prompt_mixer_kernel(n_real, n_t, x_ref, xn_ref, *refs):
    i = pl.program_id(0)
    x1_ref = refs[6]

    for parity in range(2):
        @pl.when((i < n_real) & ((i & 1) == parity))
        def _():
            _prompt_tile(i, lax.rem(i, n_t), n_t, parity, x_ref, xn_ref, *refs)

    @pl.when(i >= n_real)
    def _():
        x1_ref[...] = jnp.zeros_like(x1_ref)


def _prompt_tile(i, j, n_t, parity, x_ref, xn_ref, win_ref, wout_ref, p_ref, ln_ref, wr_ref, rb_ref,
                 x1_ref, route_ref, s_ref, buf_ref,
                 win_scr, wout_scr, proj_a_scr, proj_b_scr, qs_scr, b_scr, k_scr, o_scr, yc_scr, st_scr,
                 st0_scr, ubuf_scr):
    proj_scr, proj_next_scr = (proj_a_scr, proj_b_scr) if parity == 0 else (proj_b_scr, proj_a_scr)

    if parity == 0:
        @pl.when(i == 0)
        def _():
            win_scr[...] = win_ref[0].astype(BF16)
            wout_scr[...] = wout_ref[0].astype(BF16)
            proj_scr[...] = jnp.dot(x_ref[...].astype(BF16), win_scr[...],
                                    preferred_element_type=F32)

    @pl.when(j == 0)
    def _():
        st_scr[...] = jnp.zeros_like(st_scr)
        ubuf_scr[...] = jnp.zeros_like(ubuf_scr)

    xn_bf = xn_ref[...].astype(BF16)

    def proj_piece(g):
        cols = slice(g * 512, (g + 1) * 512)
        proj_next_scr[:, cols] = jnp.dot(xn_bf, win_scr[:, cols], preferred_element_type=F32)

    xt = x_ref[...]

    q = proj_scr[:, OFF_Q:OFF_Q + 512]
    qs_scr[...] = q * _sigmoid(q)
    logf, k = _gate_terms(proj_scr[:, OFF_F:OFF_F + 512], p_ref)
    k_scr[...] = k
    b_scr[...] = _seg_cumsum(logf, CHUNK)
    proj_piece(0)

    tri = (lax.broadcasted_iota(I32, (CHUNK, CHUNK), 0)
           >= lax.broadcasted_iota(I32, (CHUNK, CHUNK), 1))

    states = [st_scr[h] for h in range(HEADS)]
    for h in range(HEADS):
        st0_scr[h] = states[h]
    worst = jnp.zeros((1, 512), F32)
    for c in range(PROMPT_TILE // CHUNK):
        rows = slice(c * CHUNK, (c + 1) * CHUNK)
        qs_c = qs_scr[rows, :]
        b_c = b_scr[rows, :]
        k_c = k_scr[rows, :]
        v_c = proj_scr[rows, OFF_I:OFF_I + 512].astype(BF16)
        b_mid = b_c[CHUNK // 2 - 1:CHUNK // 2, :]
        b_last = b_c[CHUNK - 1:CHUNK, :]
        worst = jnp.maximum(worst, jnp.maximum(-b_mid, b_mid - b_last))
        qd = (qs_c * jnp.exp(b_c - b_mid)).astype(BF16)
        kd = (k_c * jnp.exp(b_mid - b_c)).astype(BF16)
        qb = (qs_c * jnp.exp(b_c)).astype(BF16)
        kend = (k_c * jnp.exp(b_last - b_c)).astype(BF16)
        dec = jnp.exp(b_last)
        for h in range(HEADS):
            sl = slice(h * HEAD_DIM, (h + 1) * HEAD_DIM)
            att = lax.dot_general(qd[:, sl], kd[:, sl], NT_DIMS, preferred_element_type=F32)
            att = jnp.where(tri, att, 0.0).astype(BF16)
            st = states[h]
            o_h = jnp.dot(att, v_c[:, sl], preferred_element_type=F32)
            o_h = o_h + lax.dot_general(qb[:, sl], st.astype(BF16), NT_DIMS,
                                        preferred_element_type=F32)
            o_scr[rows, sl] = o_h
            d_st = lax.dot_general(v_c[:, sl], kend[:, sl], TN_DIMS, preferred_element_type=F32)
            states[h] = st * dec[:, sl] + d_st
        proj_piece(c + 1)
    for h in range(HEADS):
        st_scr[h] = states[h]

    u = proj_scr[:, OFF_GC:OFF_GC + 512] * proj_scr[:, OFF_CX:OFF_CX + 512]
    row = lax.broadcasted_iota(I32, u.shape, 0)
    prev2 = ubuf_scr[SUBLANES - 2:SUBLANES - 1, :]
    prev1 = ubuf_scr[SUBLANES - 1:SUBLANES, :]
    u1 = jnp.where(row == 0, prev1, pltpu.roll(u, 1, axis=0))
    u2 = jnp.where(row == 0, prev2, jnp.where(row == 1, prev1, pltpu.roll(u, 2, axis=0)))
    y = (p_ref[0, P_CW0:P_CW0 + 1, :] * u2 + p_ref[0, P_CW1:P_CW1 + 1, :] * u1
         + p_ref[0, P_CW2:P_CW2 + 1, :] * u)
    ubuf_scr[...] = u[PROMPT_TILE - SUBLANES:, :]
    yc_scr[...] = proj_scr[:, OFF_GB:OFF_GB + 512] * y
    proj_piece(5)

    def finish(x_tile, fill=None):
        _post_mix(x_tile, o_scr[...], proj_scr[:, OFF_G:OFF_G + 512], yc_scr[...], p_ref, wout_scr,
                  ln_ref, wr_ref, rb_ref, x1_ref, route_ref, fill)

    finish(xt, lambda: proj_piece(6))

    @pl.when(jnp.logical_not(jnp.max(worst) <= SAFE_EXPONENT))
    def _():
        for h in range(HEADS):
            st_scr[h] = st0_scr[h]

        def block_body(blk, carry):
            r0 = pl.multiple_of(blk * SAMPLE_ROWS, SAMPLE_ROWS)
            rows = pl.ds(r0, SAMPLE_ROWS)
            b_blk = b_scr[rows, :]
            b_prev = b_scr[pl.ds(jnp.maximum(r0 - 1, 0), 1), :]
            b_prev = jnp.where((r0 & (CHUNK - 1)) == 0, 0.0, b_prev)
            outs, new_states = _exact_block(
                qs_scr[rows, :], b_blk - b_prev, k_scr[rows, :], proj_scr[rows, OFF_I:OFF_I + 512],
                [st_scr[h] for h in range(HEADS)], SAMPLE_ROWS)
            for h in range(HEADS):
                o_scr[rows, h * HEAD_DIM:(h + 1) * HEAD_DIM] = outs[h]
                st_scr[h] = new_states[h]
            return carry

        lax.fori_loop(0, PROMPT_TILE // SAMPLE_ROWS, block_body, 0)
        finish(x_ref[...])

    @pl.when(j == n_t - 1)
    def _():
        for h in range(HEADS):
            s_ref[0, h] = jnp.transpose(st_scr[h])
        buf_ref[0] = ubuf_scr[SUBLANES - (CONV_K - 1):, :]


def _sample_mixer_kernel(n_valid, n_real, n_alias, x_ref, s0_ref, cbuf_ref, win_ref, wout_ref,
                         p_ref, ln_ref, wr_ref, rb_ref, *refs):
    x1_ref, route_ref, s_ref = refs[n_alias:n_alias + 3]

    @pl.when(pl.program_id(0) < n_real)
    def _():
        _sample_tile(n_valid, x_ref, s0_ref, cbuf_ref, win_ref, wout_ref, p_ref, ln_ref, wr_ref,
                     rb_ref, *refs[n_alias:])

    @pl.when(pl.program_id(0) >= n_real)
    def _():
        s_ref[...] = jnp.zeros_like(s_ref)


def _sample_tile(n_valid, x_ref, s0_ref, cbuf_ref, win_ref, wout_ref, p_ref, ln_ref, wr_ref, rb_ref,
                 x1_ref, route_ref, s_ref, buf_ref,
                 win_scr, wout_scr, proj_scr, qs_scr, b_scr, k_scr, o_scr, y_scr):
    tile = SAMPLE_SEQS * SAMPLE_ROWS

    @pl.when(pl.program_id(0) == 0)
    def _():
        win_scr[...] = win_ref[0].astype(BF16)
        wout_scr[...] = wout_ref[0].astype(BF16)

    xt = x_ref[...]
    proj_scr[...] = jnp.dot(xt.astype(BF16), win_scr[...], preferred_element_type=F32)

    q = proj_scr[:, OFF_Q:OFF_Q + 512]
    qs_scr[...] = q * _sigmoid(q)
    logf, k = _gate_terms(proj_scr[:, OFF_F:OFF_F + 512], p_ref)
    valid = (lax.broadcasted_iota(I32, (tile, 512), 0) & (SAMPLE_ROWS - 1)) < n_valid
    k_scr[...] = jnp.where(valid, k, 0.0)
    b_scr[...] = _seg_cumsum(jnp.where(valid, logf, 0.0), SAMPLE_ROWS)

    urow = lax.broadcasted_iota(I32, (SAMPLE_ROWS, 512), 0)

    def seq_body(s, carry):
        r0 = pl.multiple_of(s * SAMPLE_ROWS, SAMPLE_ROWS)
        rows = pl.ds(r0, SAMPLE_ROWS)
        qs_c = qs_scr[rows, :]
        b_c = b_scr[rows, :]
        k_c = k_scr[rows, :]
        v_c = proj_scr[rows, OFF_I:OFF_I + 512]
        states = [jnp.transpose(s0_ref[0, s, h]) for h in range(HEADS)]
        outs, states = _exact_block(qs_c, b_c, k_c, v_c, states, n_valid)
        for h in range(HEADS):
            o_scr[rows, h * HEAD_DIM:(h + 1) * HEAD_DIM] = outs[h]
            s_ref[0, s, h] = jnp.transpose(states[h])
        u = proj_scr[rows, OFF_GC:OFF_GC + 512] * proj_scr[rows, OFF_CX:OFF_CX + 512]
        prev2 = cbuf_ref[0, s, 0:1, :]
        prev1 = cbuf_ref[0, s, 1:2, :]
        u1 = jnp.where(urow == 0, prev1, pltpu.roll(u, 1, axis=0))
        u2 = jnp.where(urow == 0, prev2, jnp.where(urow == 1, prev1, pltpu.roll(u, 2, axis=0)))
        y_scr[rows, :] = (p_ref[0, P_CW0:P_CW0 + 1, :] * u2 + p_ref[0, P_CW1:P_CW1 + 1, :] * u1
                          + p_ref[0, P_CW2:P_CW2 + 1, :] * u)
        buf_ref[s] = u[n_valid - (CONV_K - 1):n_valid, :]
        return carry

    lax.fori_loop(0, SAMPLE_SEQS, seq_body, 0)

    yc_in = proj_scr[:, OFF_GB:OFF_GB + 512] * y_scr[...]
    _post_mix(xt, o_scr[...], proj_scr[:, OFF_G:OFF_G + 512], yc_in, p_ref, wout_scr, ln_ref,
              wr_ref, rb_ref, x1_ref, route_ref)


def _plan_kernel(cls_ref, pos_ref, stats_ref):
    n_rows = cls_ref.shape[0]
    cls = cls_ref[...]
    upper = (lax.broadcasted_iota(I32, (LANES, LANES), 0)
             < lax.broadcasted_iota(I32, (LANES, LANES), 1)).astype(BF16)
    lower = (lax.broadcasted_iota(I32, (n_rows, n_rows), 1)
             < lax.broadcasted_iota(I32, (n_rows, n_rows), 0)).astype(BF16)
    lane = lax.broadcasted_iota(I32, (n_rows, LANES), 1)
    row_tot = jnp.zeros((n_rows, LANES), F32)
    for c in range(N_CLASSES):
        oh = jnp.where(cls == c, 1.0, 0.0)
        row_tot = jnp.where(lane == c, jnp.sum(oh, axis=1, keepdims=True), row_tot)
    before = jnp.dot(lower, row_tot.astype(BF16), preferred_element_type=F32)
    cnt = jnp.sum(row_tot, axis=0, keepdims=True)
    tiles = jnp.floor((cnt + (MOE_TILE - 1)) * (1.0 / MOE_TILE))
    first_tile = jnp.dot(jnp.broadcast_to(tiles, (SUBLANES, LANES)).astype(BF16), upper,
                         preferred_element_type=F32)[0:1]
    base = before + first_tile * MOE_TILE
    pos = jnp.zeros((n_rows, LANES), F32)
    for c in range(N_CLASSES):
        oh = jnp.where(cls == c, 1.0, 0.0)
        local = jnp.dot(oh.astype(BF16), upper, preferred_element_type=F32)
        pos = pos + oh * (base[:, c:c + 1] + local)
    pos_ref[...] = pos.astype(I32)
    sub = lax.broadcasted_iota(I32, (SUBLANES, LANES), 0)
    stats_ref[...] = jnp.where(sub == 0, cnt, jnp.where(sub == 1, tiles,
                                                        jnp.where(sub == 2, first_tile, 0.0)))


def _row_copy(src_ref, src_row, dst_ref, dst_row, sem):
    return pltpu.make_async_copy(src_ref.at[pl.ds(src_row, 1), :],
                                 dst_ref.at[pl.ds(dst_row, 1), :], sem)


def _perm_kernel(n, pos_ref, cstart_ref, ccnt_ref, cpad_ref, nused_ref, perm_ref):
    def stand_in_body(r, carry):
        perm_ref[r] = n + PAD_ROWS + r
        return carry

    lax.fori_loop(0, MOE_TILE, stand_in_body, 0, unroll=8)

    def unused_body(p, carry):
        perm_ref[p] = n
        return carry

    lax.fori_loop((nused_ref[0] + 1) * MOE_TILE, perm_ref.shape[0], unused_body, 0)

    def cls_body(c, k):
        base = MOE_TILE + cstart_ref[c] + ccnt_ref[c]

        def pad_body(q, carry):
            perm_ref[base + q] = n + k + q
            return carry

        lax.fori_loop(0, cpad_ref[c], pad_body, 0)
        return k + cpad_ref[c]

    lax.fori_loop(0, N_CLASSES, cls_body, 0)

    def tok_body(t, carry):
        perm_ref[MOE_TILE + pos_ref[t]] = t
        return carry

    lax.fori_loop(0, n, tok_body, 0, unroll=8)


def _expert_kernel(n, perm_ref, elo_ref, ehi_ref, flag_ref,
                   x_ref, wg_lo, wu_lo, wd_lo, wg_hi, wu_hi, wd_hi, ln_ref, out_ref,
                   xbuf, ybuf, w1_scr, wd_scr, gsem, ssem, zsem):
    i = pl.program_id(0)
    flag = flag_ref[i]
    slot = lax.rem(i, PIPE)
    nxt = lax.rem(i + 1, PIPE)
    other = lax.rem(i + 2, PIPE)

    def gather_wait(s):
        pltpu.make_async_copy(x_ref.at[pl.ds(0, MOE_TILE), :], xbuf.at[s], gsem.at[s]).wait()

    def scatter_wait(s):
        pltpu.make_async_copy(ybuf.at[s], out_ref.at[pl.ds(0, MOE_TILE), :], ssem.at[s]).wait()

    def gather_start(tile, s):
        for r in range(MOE_TILE):
            _row_copy(x_ref, perm_ref[(tile + 1) * MOE_TILE + r], xbuf.at[s], r,
                      gsem.at[s]).start(priority=r % 2)

    def scatter_start(tile, s):
        for r in range(MOE_TILE):
            _row_copy(ybuf.at[s], r, out_ref, perm_ref[(tile + 1) * MOE_TILE + r],
                      ssem.at[s]).start(priority=r % 2)

    @pl.when(i == 0)
    def _():
        ybuf[...] = jnp.zeros_like(ybuf)
        for c in range(PAD_ROWS // MOE_TILE):
            pltpu.make_async_copy(ybuf.at[0], out_ref.at[pl.ds(n + c * MOE_TILE, MOE_TILE), :],
                                  zsem).start()
        for c in range(PAD_ROWS // MOE_TILE):
            pltpu.make_async_copy(ybuf.at[0], out_ref.at[pl.ds(n + c * MOE_TILE, MOE_TILE), :],
                                  zsem).wait()
        for s in range(PIPE - 1):
            pltpu.make_async_copy(
                ybuf.at[s], out_ref.at[pl.ds(n + PAD_ROWS + (s + 1) * MOE_TILE, MOE_TILE), :],
                ssem.at[s]).start()
        gather_start(0, 0)
        gather_start(1, 1)

    @pl.when((flag & 2) != 0)
    def _():
        w1_scr[:, 0 * EXPERT_FF:1 * EXPERT_FF] = wg_lo[0, 0].astype(BF16)
        w1_scr[:, 1 * EXPERT_FF:2 * EXPERT_FF] = wu_lo[0, 0].astype(BF16)
        w1_scr[:, 2 * EXPERT_FF:3 * EXPERT_FF] = wg_hi[0, 0].astype(BF16)
        w1_scr[:, 3 * EXPERT_FF:4 * EXPERT_FF] = wu_hi[0, 0].astype(BF16)
        wd_scr[0:EXPERT_FF, :] = wd_lo[0, 0].astype(BF16)
        wd_scr[EXPERT_FF:, :] = wd_hi[0, 0].astype(BF16)

    def tile_body(slot, other):
        gather_wait(slot)
        scatter_wait(slot)
        gather_start(i + 2, other)
        scatter_start(i - 1, other)
        xs = xbuf[slot]
        x = xs[:, 0:D_MODEL]
        g_lo = xs[:, D_MODEL:D_MODEL + 1]
        g_hi = xs[:, D_MODEL + 1:D_MODEL + 2]
        hc = jnp.dot(x.astype(BF16), w1_scr[...], preferred_element_type=F32)
        hg_lo, hu_lo = hc[:, 0:EXPERT_FF], hc[:, EXPERT_FF:2 * EXPERT_FF]
        hg_hi, hu_hi = hc[:, 2 * EXPERT_FF:3 * EXPERT_FF], hc[:, 3 * EXPERT_FF:]
        h_lo = hg_lo * _sigmoid(hg_lo) * hu_lo * g_lo
        h_hi = hg_hi * _sigmoid(hg_hi) * hu_hi * g_hi
        h = jnp.concatenate([h_lo, h_hi], axis=-1).astype(BF16)
        y = jnp.dot(h, wd_scr[...], preferred_element_type=F32)
        ybuf[slot] = _layer_norm(ALPHA * x + y, ln_ref[0, 0:1, :], ln_ref[0, 1:2, :])

    for k in range(PIPE):
        @pl.when(((flag & 1) != 0) & (slot == k))
        def _():
            tile_body(k, (k + 2) % PIPE)

    @pl.when((flag & 4) != 0)
    def _():
        scatter_start(i - 1, other)
        for s in (slot, nxt, other):
            scatter_wait(s)
        for s in (slot, nxt):
            gather_wait(s)


def _layer_spec(shape, layer, n_grid, single_buffer=False):
    idx = (layer,) + (0,) * len(shape)
    if n_grid == 1:
        index_map = lambda i: idx
    else:
        index_map = lambda i, j: idx
    if single_buffer:
        return pl.BlockSpec((1,) + shape, index_map, pipeline_mode=pl.Buffered(1))
    return pl.BlockSpec((1,) + shape, index_map)


def _prompt_mixer(x, w_in, w_out, params, ln, wr, rb, layer, batch, seq, n_rows_out):
    n_t = seq // PROMPT_TILE
    n_real = batch * n_t
    last = n_real - 1
    row_in = pl.BlockSpec((PROMPT_TILE, D_MODEL), lambda i: (jnp.minimum(i, last), 0))
    row_next = pl.BlockSpec((PROMPT_TILE, D_MODEL), lambda i: (jnp.minimum(i + 1, last), 0))
    row_out = pl.BlockSpec((PROMPT_TILE, ROW_W), lambda i: (i, 0))
    route_out = pl.BlockSpec((SUBLANES, PROMPT_TILE), lambda i: (0, jnp.minimum(i, last)))
    full1 = lambda shape: pl.BlockSpec(shape, lambda i: (0, 0))
    seq_of = lambda i: jnp.minimum(i, last) // n_t
    return pl.pallas_call(
        functools.partial(_prompt_mixer_kernel, n_real, n_t),
        grid=(n_rows_out // PROMPT_TILE,),
        in_specs=[row_in, row_next,
                  _layer_spec((D_MODEL, N_IN), layer, 1, True),
                  _layer_spec((D_MODEL, D_MODEL), layer, 1, True),
                  _layer_spec((8, 512), layer, 1), _layer_spec((2, D_MODEL), layer, 1),
                  full1((2 * N_EXPERTS, D_MODEL)), full1((N_EXPERTS, 1))],
        out_specs=[row_out, route_out,
                   pl.BlockSpec((1, HEADS, HEAD_DIM, HEAD_DIM), lambda i: (seq_of(i), 0, 0, 0)),
                   pl.BlockSpec((1, CONV_K - 1, CONV_DIM), lambda i: (seq_of(i), 0, 0))],
        out_shape=[jax.ShapeDtypeStruct((n_rows_out, ROW_W), F32),
                   jax.ShapeDtypeStruct((SUBLANES, batch * seq), F32),
                   jax.ShapeDtypeStruct((batch, HEADS, HEAD_DIM, HEAD_DIM), F32),
                   jax.ShapeDtypeStruct((batch, CONV_K - 1, CONV_DIM), F32)],
        scratch_shapes=[pltpu.VMEM((D_MODEL, N_IN), BF16), pltpu.VMEM((D_MODEL, D_MODEL), BF16),
                        pltpu.VMEM((PROMPT_TILE, N_IN), F32), pltpu.VMEM((PROMPT_TILE, N_IN), F32)]
        + [pltpu.VMEM((PROMPT_TILE, 512), F32)] * 5
        + [pltpu.VMEM((HEADS, HEAD_DIM, HEAD_DIM), F32)] * 2
        + [pltpu.VMEM((SUBLANES, CONV_DIM), F32)],
        compiler_params=pltpu.CompilerParams(
            dimension_semantics=("arbitrary",), vmem_limit_bytes=VMEM_LIMIT),
        name="prompt_mixer",
    )(x, x, w_in, w_out, params, ln, wr, rb)


def _sample_mixer(x, x_row0, x1_all, s_all, s0, cbuf, w_in, w_out, params, ln, wr, rb, layer,
                  batch, n_valid):
    tile = SAMPLE_SEQS * SAMPLE_ROWS
    n = batch * SAMPLE_ROWS
    n_real = batch // SAMPLE_SEQS
    last = n_real - 1
    x1_row0 = x1_all.shape[0] - PAD_ROWS - n
    in_blk0, out_blk0 = x_row0 // tile, x1_row0 // tile
    real = lambda i: jnp.minimum(i, last)
    row_in = pl.BlockSpec((tile, D_MODEL), lambda i: (in_blk0 + real(i), 0))
    row_out = pl.BlockSpec((tile, ROW_W), lambda i: (out_blk0 + real(i), 0))
    route_out = pl.BlockSpec((SUBLANES, tile), lambda i: (0, real(i)))
    st_spec = pl.BlockSpec((1, SAMPLE_SEQS, HEADS, HEAD_DIM, HEAD_DIM),
                           lambda i: (layer, real(i), 0, 0, 0))
    cb_spec = pl.BlockSpec((1, SAMPLE_SEQS, CONV_K - 1, CONV_DIM),
                           lambda i: (layer, real(i), 0, 0))
    if s_all is None:
        n_steps = DEPTH * n_real
        st_out = pl.BlockSpec((1, SAMPLE_SEQS, HEADS, HEAD_DIM, HEAD_DIM),
                              lambda i: (i // n_real, lax.rem(i, n_real), 0, 0, 0))
        aliased = [x1_all]
    else:
        n_steps = n_real
        st_out = st_spec
        aliased = [x1_all, s_all]
    cb_out = pl.BlockSpec((SAMPLE_SEQS, CONV_K - 1, CONV_DIM), lambda i: (real(i), 0, 0))
    full1 = lambda shape: pl.BlockSpec(shape, lambda i: (0, 0))
    any_spec = pl.BlockSpec(memory_space=pl.ANY)
    return pl.pallas_call(
        functools.partial(_sample_mixer_kernel, n_valid, n_real, len(aliased)),
        grid=(n_steps,),
        in_specs=[row_in, st_spec, cb_spec,
                  _layer_spec((D_MODEL, N_IN), layer, 1, True),
                  _layer_spec((D_MODEL, D_MODEL), layer, 1, True),
                  _layer_spec((8, 512), layer, 1), _layer_spec((2, D_MODEL), layer, 1),
                  full1((2 * N_EXPERTS, D_MODEL)), full1((N_EXPERTS, 1))]
        + [any_spec] * len(aliased),
        out_specs=[row_out, route_out, st_out, cb_out],
        input_output_aliases={9: 0, 10: 2} if s_all is not None else {9: 0},
        out_shape=[jax.ShapeDtypeStruct(x1_all.shape, F32),
                   jax.ShapeDtypeStruct((SUBLANES, n), F32),
                   jax.ShapeDtypeStruct((DEPTH, batch, HEADS, HEAD_DIM, HEAD_DIM), F32),
                   jax.ShapeDtypeStruct((batch, CONV_K - 1, CONV_DIM), F32)],
        scratch_shapes=[pltpu.VMEM((D_MODEL, N_IN), BF16), pltpu.VMEM((D_MODEL, D_MODEL), BF16),
                        pltpu.VMEM((tile, N_IN), F32)] + [pltpu.VMEM((tile, 512), F32)] * 5,
        compiler_params=pltpu.CompilerParams(
            dimension_semantics=("arbitrary",), vmem_limit_bytes=VMEM_LIMIT),
        name="sample_mixer",
    )(x, s0, cbuf, w_in, w_out, params, ln, wr, rb, *aliased)


def _plan(cls2d):
    n_rows = cls2d.shape[0]
    return pl.pallas_call(
        _plan_kernel,
        out_shape=[jax.ShapeDtypeStruct((n_rows, LANES), I32),
                   jax.ShapeDtypeStruct((SUBLANES, LANES), F32)],
        name="moe_plan",
    )(cls2d)


def _perm(n, pos, cstart, ccnt, cpad, n_used, n_tiles):
    smem = pl.BlockSpec(memory_space=pltpu.SMEM)
    return pl.pallas_call(
        functools.partial(_perm_kernel, n),
        in_specs=[smem] * 5, out_specs=smem,
        out_shape=jax.ShapeDtypeStruct(((n_tiles + PIPE) * MOE_TILE,), I32),
        name="moe_perm",
    )(pos, cstart, ccnt, cpad, n_used.reshape(1))


def _experts(n, perm, elo, ehi, flag, x1_all, w_gate, w_up, w_down, ln, layer, n_steps):
    def w_spec(shape, table_idx):
        def index_map(i, perm_r, elo_r, ehi_r, flag_r):
            return (layer, (elo_r, ehi_r)[table_idx][i], 0, 0)
        return pl.BlockSpec((1, 1) + shape, index_map)

    up = (D_MODEL, EXPERT_FF)
    down = (EXPERT_FF, D_MODEL)
    any_spec = pl.BlockSpec(memory_space=pl.ANY)
    return pl.pallas_call(
        functools.partial(_expert_kernel, n),
        grid_spec=pltpu.PrefetchScalarGridSpec(
            num_scalar_prefetch=4, grid=(n_steps,),
            in_specs=[any_spec,
                      w_spec(up, 0), w_spec(up, 0), w_spec(down, 0),
                      w_spec(up, 1), w_spec(up, 1), w_spec(down, 1),
                      pl.BlockSpec((1, 2, D_MODEL), lambda i, p, e0, e1, f: (layer, 0, 0))],
            out_specs=any_spec,
            scratch_shapes=[pltpu.VMEM((PIPE, MOE_TILE, ROW_W), F32),
                            pltpu.VMEM((PIPE, MOE_TILE, D_MODEL), F32),
                            pltpu.VMEM((D_MODEL, 4 * EXPERT_FF), BF16),
                            pltpu.VMEM((2 * EXPERT_FF, D_MODEL), BF16),
                            pltpu.SemaphoreType.DMA((PIPE,)), pltpu.SemaphoreType.DMA((PIPE,)),
                            pltpu.SemaphoreType.DMA(())]),
        out_shape=jax.ShapeDtypeStruct((n + PAD_ROWS + PIPE * MOE_TILE, D_MODEL), F32),
        compiler_params=pltpu.CompilerParams(
            dimension_semantics=("arbitrary",), vmem_limit_bytes=VMEM_LIMIT),
        name="moe_experts",
    )(perm, elo, ehi, flag, x1_all, w_gate, w_up, w_down, w_gate, w_up, w_down, ln)


def _moe(x1_all, route_p, route_s, w_gate, w_up, w_down, ln2, layer):
    n = x1_all.shape[0] - PAD_ROWS
    n_tiles = n // MOE_TILE + N_CLASSES
    n_steps = n_tiles + 1
    cls2d = jnp.concatenate([route_p[0], route_s[0]]).reshape(n // LANES, LANES)
    pos2d, stats = _plan(cls2d)

    cnt = stats[0, :N_CLASSES].astype(I32)
    tiles = stats[1, :N_CLASSES].astype(I32)
    first_tile = stats[2, :N_CLASSES].astype(I32)
    end_tile = first_tile + tiles
    n_used = end_tile[N_CLASSES - 1]
    step = jnp.arange(n_steps, dtype=I32)
    tcls = jnp.sum((end_tile[None, :] <= jnp.minimum(step, n_used - 1)[:, None]).astype(I32), axis=1)
    valid = step < n_used
    changed = jnp.concatenate([jnp.ones((1,), jnp.bool_), tcls[1:] != tcls[:-1]]) & valid
    flag = valid.astype(I32) + 2 * changed.astype(I32) + 4 * (step == n_used).astype(I32)
    group = tcls // N_PAIRS
    pair = tcls % N_PAIRS
    elo = group * PER_GROUP + jnp.asarray(PAIR_LO, I32)[pair]
    ehi = group * PER_GROUP + jnp.asarray(PAIR_HI, I32)[pair]

    perm = _perm(n, pos2d.reshape(n), first_tile * MOE_TILE, cnt, tiles * MOE_TILE - cnt, n_used,
                 n_tiles)
    return _experts(n, perm, elo, ehi, flag, x1_all, w_gate, w_up, w_down, ln2, layer, n_steps)


def kernel(x_prompt, x_sample, state_hgrn, state_conv, w_in, w_out, lower_bounds, hgrn_norm_g,
           conv_w, conv_norm_g, ln1_g, ln1_b, ln2_g, ln2_b, w_router, router_bias,
           w_gate, w_up, w_down):
    batch, seq, _ = x_prompt.shape
    dec_batch, dec_seq, _ = x_sample.shape
    assert seq % PROMPT_TILE == 0 and dec_batch % SAMPLE_SEQS == 0
    assert CONV_K - 1 <= dec_seq <= SAMPLE_ROWS
    assert (batch * seq + dec_batch * SAMPLE_ROWS) % MOE_TILE == 0

    lb = jnp.cumsum(jax.nn.softmax(lower_bounds.astype(F32), axis=0), axis=0)
    lb = lb - lb[0:1]
    params = jnp.stack([jnp.log(lb), jnp.log1p(-lb), 1.0 - lb, hgrn_norm_g, conv_norm_g,
                        conv_w[:, 0], conv_w[:, 1], conv_w[:, 2]], axis=1)
    ln1 = jnp.stack([ln1_g, ln1_b], axis=1)
    ln2 = jnp.stack([ln2_g, ln2_b], axis=1)
    wr_hi = w_router.astype(BF16)
    wr_lo = (w_router - wr_hi.astype(F32)).astype(BF16)
    wr = jnp.concatenate([wr_hi.T, wr_lo.T], axis=0)
    rb = router_bias.astype(F32).reshape(N_EXPERTS, 1)

    n_p, n_s = batch * seq, dec_batch * SAMPLE_ROWS
    n = n_p + n_s
    xp = x_prompt.reshape(n_p, D_MODEL)
    xs = jnp.pad(x_sample, ((0, 0), (0, SAMPLE_ROWS - dec_seq), (0, 0))).reshape(n_s, D_MODEL)
    xs_row0 = 0

    s_p, b_p, b_s = [], [], []
    s_s = None
    for l in range(DEPTH):
        x1_all, route_p, s_l, b_l = _prompt_mixer(xp, w_in, w_out, params, ln1, wr, rb, l, batch,
                                                  seq, n + PAD_ROWS)
        s_p.append(s_l)
        b_p.append(b_l)
        x1_all, route_s, s_s, b_l = _sample_mixer(xs, xs_row0, x1_all, s_s, state_hgrn, state_conv,
                                                  w_in, w_out, params, ln1, wr, rb, l, dec_batch,
                                                  dec_seq)
        b_s.append(b_l)
        xp = xs = _moe(x1_all, route_p, route_s, w_gate, w_up, w_down, ln2, l)
        xs_row0 = n_p

    y_prompt = xp[:n_p].reshape(batch, seq, D_MODEL)
    y_sample = xs[n_p:n].reshape(dec_batch, SAMPLE_ROWS, D_MODEL)[:, :dec_seq]
    return (y_prompt, y_sample, jnp.stack(s_p), jnp.stack(b_p), s_s, jnp.stack(b_s))
```

```python
import functools

import jax
import jax.numpy as jnp
from jax import lax
from jax.experimental import pallas as pl
from jax.experimental.pallas import tpu as pltpu

F32 = jnp.float32
BF16 = jnp.bfloat16
I32 = jnp.int32

D_MODEL = 1024
DEPTH = 2
HEADS = 4
HEAD_DIM = 128
HGRN_W = HEADS * HEAD_DIM
CONV_DIM = 512
CONV_GROUPS = 8
CONV_K = 3
N_IN = 7 * 512
N_EXPERTS = 16
N_GROUPS = 4
PER_GROUP = 4
N_PAIRS = 6
N_CLASSES = N_GROUPS * N_PAIRS
PAIR_LO = (0, 0, 0, 1, 1, 2)
PAIR_HI = (1, 2, 3, 2, 3, 3)
EXPERT_FF = 512
ALPHA = (2 * DEPTH) ** 0.25
LN_EPS = 1e-5
RMS_EPS = 1e-6
SAFE_EXPONENT = 80.0

LANES = 128
SUBLANES = 8
CHUNK = 64
PROMPT_TILE = 256
SAMPLE_ROWS = 8
SAMPLE_SEQS = 16
MOE_TILE = 256
ROW_W = D_MODEL + LANES
PAD_ROWS = N_CLASSES * MOE_TILE
PIPE = 3
VMEM_LIMIT = 56 * 1024 * 1024

OFF_Q, OFF_F, OFF_I, OFF_G, OFF_GB, OFF_GC, OFF_CX = (i * 512 for i in range(7))

P_LOGLB, P_LOG1MLB, P_OMLB, P_HNORM, P_CNORM, P_CW0, P_CW1, P_CW2 = range(8)

NT_DIMS = (((1,), (1,)), ((), ()))
TN_DIMS = (((0,), (0,)), ((), ()))


def _sigmoid(x):
    return 1.0 / (1.0 + jnp.exp(-x))


def _seg_cumsum(x, seg):
    row = lax.broadcasted_iota(I32, x.shape, 0)
    pos = row & (seg - 1)
    sh = 1
    while sh < seg:
        x = x + jnp.where(pos >= sh, pltpu.roll(x, sh, axis=0), 0.0)
        sh *= 2
    return x


def _gate_terms(z, p_ref):
    e = jnp.exp(-jnp.abs(z))
    inv = 1.0 / (1.0 + e)
    logsig = jnp.minimum(z, 0.0) - jnp.log(1.0 + e)
    a = p_ref[0, P_LOGLB:P_LOGLB + 1, :]
    b = p_ref[0, P_LOG1MLB:P_LOG1MLB + 1, :] + logsig
    logf = jnp.maximum(a, b) + jnp.log(1.0 + jnp.exp(-jnp.abs(a - b)))
    k = p_ref[0, P_OMLB:P_OMLB + 1, :] * (jnp.where(z >= 0.0, e, 1.0) * inv)
    return logf, k


def _exact_block(qs_c, b_c, k_c, v_c, states, n_valid):
    trow = lax.broadcasted_iota(I32, (SAMPLE_ROWS, 1), 0)
    b_last = b_c[n_valid - 1:n_valid, :]
    qb = (qs_c * jnp.exp(b_c)).astype(BF16)
    kend = (k_c * jnp.exp(b_last - b_c)).astype(BF16)
    dec = jnp.exp(b_last)
    outs, new_states = [], []
    for h in range(HEADS):
        sl = slice(h * HEAD_DIM, (h + 1) * HEAD_DIM)
        st = states[h]
        o_h = lax.dot_general(qb[:, sl], st.astype(BF16), NT_DIMS, preferred_element_type=F32)
        for t in range(n_valid):
            dlt = jnp.minimum(b_c[:, sl] - b_c[t:t + 1, sl], 0.0)
            a_col = jnp.sum(qs_c[:, sl] * k_c[t:t + 1, sl] * jnp.exp(dlt), axis=-1, keepdims=True)
            a_col = jnp.where(trow >= t, a_col, 0.0)
            o_h = o_h + a_col * v_c[t:t + 1, sl]
        d_st = lax.dot_general(v_c[:, sl].astype(BF16), kend[:, sl], TN_DIMS,
                               preferred_element_type=F32)
        outs.append(o_h)
        new_states.append(st * dec[:, sl] + d_st)
    return outs, new_states


def _group_rms(x, n_groups):
    width = x.shape[-1] // n_groups
    x2 = x * x
    outs = []
    for s in range(x.shape[-1] // LANES):
        xs = x[:, s * LANES:(s + 1) * LANES]
        x2s = x2[:, s * LANES:(s + 1) * LANES]
        if width == LANES:
            ms = jnp.sum(x2s, axis=-1, keepdims=True) * (1.0 / width)
            scale = lax.rsqrt(ms + RMS_EPS)
        else:
            lane = lax.broadcasted_iota(I32, xs.shape, 1)
            lo = lane < width
            ms_lo = jnp.sum(jnp.where(lo, x2s, 0.0), axis=-1, keepdims=True) * (1.0 / width)
            ms_hi = jnp.sum(jnp.where(lo, 0.0, x2s), axis=-1, keepdims=True) * (1.0 / width)
            scale = jnp.where(lo, lax.rsqrt(ms_lo + RMS_EPS), lax.rsqrt(ms_hi + RMS_EPS))
        outs.append(xs * scale)
    return jnp.concatenate(outs, axis=-1)


def _layer_norm(r, g, b):
    mu = jnp.mean(r, axis=-1, keepdims=True)
    rc = r - mu
    var = jnp.mean(rc * rc, axis=-1, keepdims=True)
    return rc * lax.rsqrt(var + LN_EPS) * g + b


def _route(x1, wr_ref, rb_ref):
    x_hi = x1.astype(BF16)
    x_lo = (x1 - x_hi.astype(F32)).astype(BF16)
    wr = wr_ref[...]
    r1 = lax.dot_general(wr, x_hi, NT_DIMS, preferred_element_type=F32)
    r2 = lax.dot_general(wr, x_lo, NT_DIMS, preferred_element_type=F32)
    lt = r1[0:N_EXPERTS] + r1[N_EXPERTS:] + r2[0:N_EXPERTS] + r2[N_EXPERTS:] + rb_ref[...]
    lg = [lt[e:e + 1, :] for e in range(N_EXPERTS)]
    mx = lg[0]
    for e in range(1, N_EXPERTS):
        mx = jnp.maximum(mx, lg[e])
    ex = [jnp.exp(l - mx) for l in lg]
    best = None
    gi = None
    for g in range(N_GROUPS):
        a, b, c, d = ex[PER_GROUP * g:PER_GROUP * (g + 1)]
        s = jnp.maximum(jnp.maximum(jnp.maximum(a + b, a + c), jnp.maximum(a + d, b + c)),
                        jnp.maximum(b + d, c + d))
        if g == 0:
            best, gi = s, jnp.zeros(s.shape, I32)
        else:
            upd = s > best
            best = jnp.where(upd, s, best)
            gi = jnp.where(upd, g, gi)
    v = []
    for i in range(PER_GROUP):
        vi = ex[3 * PER_GROUP + i]
        for g in (2, 1, 0):
            vi = jnp.where(gi == g, ex[PER_GROUP * g + i], vi)
        v.append(vi)
    w1, i1 = v[0], jnp.zeros(v[0].shape, I32)
    for i in range(1, PER_GROUP):
        upd = v[i] > w1
        w1 = jnp.where(upd, v[i], w1)
        i1 = jnp.where(upd, i, i1)
    w2, i2 = None, None
    for i in range(PER_GROUP):
        vi = jnp.where(i1 == i, -1.0, v[i])
        if i == 0:
            w2, i2 = vi, jnp.zeros(vi.shape, I32)
        else:
            upd = vi > w2
            w2 = jnp.where(upd, vi, w2)
            i2 = jnp.where(upd, i, i2)
    inv = 1.0 / (w1 + w2)
    first_lo = i1 < i2
    lo = jnp.where(first_lo, i1, i2)
    hi = jnp.where(first_lo, i2, i1)
    g_lo = jnp.where(first_lo, w1, w2) * inv
    g_hi = jnp.where(first_lo, w2, w1) * inv
    pair = jnp.where(lo == 0, 0, jnp.where(lo == 1, 3, 5)) + hi - lo - 1
    return gi * N_PAIRS + pair, g_lo, g_hi


def _post_mix(xt, o, g, yc_in, p_ref, wout_scr, ln_ref, wr_ref, rb_ref, x1_ref, route_ref,
              fill=None):
    rows = xt.shape[0]
    o = _group_rms(o, HEADS) * p_ref[0, P_HNORM:P_HNORM + 1, :]
    o = o * (g * _sigmoid(g))
    yc = _group_rms(yc_in, CONV_GROUPS) * p_ref[0, P_CNORM:P_CNORM + 1, :]
    mix = jnp.concatenate([o, yc], axis=-1).astype(BF16)
    h = jnp.dot(mix, wout_scr[...], preferred_element_type=F32)
    x1 = _layer_norm(ALPHA * xt + h, ln_ref[0, 0:1, :], ln_ref[0, 1:2, :])
    if fill is not None:
        fill()
    cls, g_lo, g_hi = _route(x1, wr_ref, rb_ref)
    x1_ref[:, 0:D_MODEL] = x1
    sub = lax.broadcasted_iota(I32, (LANES, rows), 0)
    gates_t = jnp.where(sub == 0, g_lo, jnp.where(sub == 1, g_hi, 0.0))
    x1_ref[:, D_MODEL:ROW_W] = jnp.transpose(gates_t)
    sub8 = lax.broadcasted_iota(I32, (SUBLANES, rows), 0)
    route_ref[...] = jnp.where(sub8 == 0, cls.astype(F32), 0.0)


def _prompt_mixer_kernel(n_real, n_t, x_ref, xn_ref, *refs):
    i = pl.program_id(0)
    x1_ref = refs[6]

    for parity in range(2):
        @pl.when((i < n_real) & ((i & 1) == parity))
        def _():
            _prompt_tile(i, lax.rem(i, n_t), n_t, parity, x_ref, xn_ref, *refs)

    @pl.when(i >= n_real)
    def _():
        x1_ref[...] = jnp.zeros_like(x1_ref)


def _prompt_tile(i, j, n_t, parity, x_ref, xn_ref, win_ref, wout_ref, p_ref, ln_ref, wr_ref, rb_ref,
                 x1_ref, route_ref, s_ref, buf_ref,
                 win_scr, wout_scr, proj_a_scr, proj_b_scr, qs_scr, b_scr, k_scr, o_scr, yc_scr, st_scr,
                 st0_scr, ubuf_scr):
    proj_scr, proj_next_scr = (proj_a_scr, proj_b_scr) if parity == 0 else (proj_b_scr, proj_a_scr)

    if parity == 0:
        @pl.when(i == 0)
        def _():
            win_scr[...] = win_ref[0].astype(BF16)
            wout_scr[...] = wout_ref[0].astype(BF16)
            proj_scr[...] = jnp.dot(x_ref[...].astype(BF16), win_scr[...],
                                    preferred_element_type=F32)

    @pl.when(j == 0)
    def _():
        st_scr[...] = jnp.zeros_like(st_scr)
        ubuf_scr[...] = jnp.zeros_like(ubuf_scr)

    xn_bf = xn_ref[...].astype(BF16)

    def proj_piece(g):
        cols = slice(g * 512, (g + 1) * 512)
        proj_next_scr[:, cols] = jnp.dot(xn_bf, win_scr[:, cols], preferred_element_type=F32)

    xt = x_ref[...]

    q = proj_scr[:, OFF_Q:OFF_Q + 512]
    qs_scr[...] = q * _sigmoid(q)
    logf, k = _gate_terms(proj_scr[:, OFF_F:OFF_F + 512], p_ref)
    k_scr[...] = k
    b_scr[...] = _seg_cumsum(logf, CHUNK)
    proj_piece(0)

    tri = (lax.broadcasted_iota(I32, (CHUNK, CHUNK), 0)
           >= lax.broadcasted_iota(I32, (CHUNK, CHUNK), 1))

    states = [st_scr[h] for h in range(HEADS)]
    for h in range(HEADS):
        st0_scr[h] = states[h]
    worst = jnp.zeros((1, 512), F32)
    for c in range(PROMPT_TILE // CHUNK):
        rows = slice(c * CHUNK, (c + 1) * CHUNK)
        qs_c = qs_scr[rows, :]
        b_c = b_scr[rows, :]
        k_c = k_scr[rows, :]
        v_c = proj_scr[rows, OFF_I:OFF_I + 512].astype(BF16)
        b_mid = b_c[CHUNK // 2 - 1:CHUNK // 2, :]
        b_last = b_c[CHUNK - 1:CHUNK, :]
        worst = jnp.maximum(worst, jnp.maximum(-b_mid, b_mid - b_last))
        qd = (qs_c * jnp.exp(b_c - b_mid)).astype(BF16)
        kd = (k_c * jnp.exp(b_mid - b_c)).astype(BF16)
        qb = (qs_c * jnp.exp(b_c)).astype(BF16)
        kend = (k_c * jnp.exp(b_last - b_c)).astype(BF16)
        dec = jnp.exp(b_last)
        for h in range(HEADS):
            sl = slice(h * HEAD_DIM, (h + 1) * HEAD_DIM)
            att = lax.dot_general(qd[:, sl], kd[:, sl], NT_DIMS, preferred_element_type=F32)
            att = jnp.where(tri, att, 0.0).astype(BF16)
            st = states[h]
            o_h = jnp.dot(att, v_c[:, sl], preferred_element_type=F32)
            o_h = o_h + lax.dot_general(qb[:, sl], st.astype(BF16), NT_DIMS,
                                        preferred_element_type=F32)
            o_scr[rows, sl] = o_h
            d_st = lax.dot_general(v_c[:, sl], kend[:, sl], TN_DIMS, preferred_element_type=F32)
            states[h] = st * dec[:, sl] + d_st
        proj_piece(c + 1)
    for h in range(HEADS):
        st_scr[h] = states[h]

    u = proj_scr[:, OFF_GC:OFF_GC + 512] * proj_scr[:, OFF_CX:OFF_CX + 512]
    row = lax.broadcasted_iota(I32, u.shape, 0)
    prev2 = ubuf_scr[SUBLANES - 2:SUBLANES - 1, :]
    prev1 = ubuf_scr[SUBLANES - 1:SUBLANES, :]
    u1 = jnp.where(row == 0, prev1, pltpu.roll(u, 1, axis=0))
    u2 = jnp.where(row == 0, prev2, jnp.where(row == 1, prev1, pltpu.roll(u, 2, axis=0)))
    y = (p_ref[0, P_CW0:P_CW0 + 1, :] * u2 + p_ref[0, P_CW1:P_CW1 + 1, :] * u1
         + p_ref[0, P_CW2:P_CW2 + 1, :] * u)
    ubuf_scr[...] = u[PROMPT_TILE - SUBLANES:, :]
    yc_scr[...] = proj_scr[:, OFF_GB:OFF_GB + 512] * y

    def finish(x_tile, fill=None):
        _post_mix(x_tile, o_scr[...], proj_scr[:, OFF_G:OFF_G + 512], yc_scr[...], p_ref, wout_scr,
                  ln_ref, wr_ref, rb_ref, x1_ref, route_ref, fill)

    finish(xt, lambda: (proj_piece(5), proj_piece(6)))

    @pl.when(jnp.logical_not(jnp.max(worst) <= SAFE_EXPONENT))
    def _():
        for h in range(HEADS):
            st_scr[h] = st0_scr[h]

        def block_body(blk, carry):
            r0 = pl.multiple_of(blk * SAMPLE_ROWS, SAMPLE_ROWS)
            rows = pl.ds(r0, SAMPLE_ROWS)
            b_blk = b_scr[rows, :]
            b_prev = b_scr[pl.ds(jnp.maximum(r0 - 1, 0), 1), :]
            b_prev = jnp.where((r0 & (CHUNK - 1)) == 0, 0.0, b_prev)
            outs, new_states = _exact_block(
                qs_scr[rows, :], b_blk - b_prev, k_scr[rows, :], proj_scr[rows, OFF_I:OFF_I + 512],
                [st_scr[h] for h in range(HEADS)], SAMPLE_ROWS)
            for h in range(HEADS):
                o_scr[rows, h * HEAD_DIM:(h + 1) * HEAD_DIM] = outs[h]
                st_scr[h] = new_states[h]
            return carry

        lax.fori_loop(0, PROMPT_TILE // SAMPLE_ROWS, block_body, 0)
        finish(x_ref[...])

    @pl.when(j == n_t - 1)
    def _():
        for h in range(HEADS):
            s_ref[0, h] = jnp.transpose(st_scr[h])
        buf_ref[0] = ubuf_scr[SUBLANES - (CONV_K - 1):, :]


def _sample_mixer_kernel(n_valid, n_real, n_alias, x_ref, s0_ref, cbuf_ref, win_ref, wout_ref,
                         p_ref, ln_ref, wr_ref, rb_ref, *refs):
    x1_ref, route_ref, s_ref = refs[n_alias:n_alias + 3]

    @pl.when(pl.program_id(0) < n_real)
    def _():
        _sample_tile(n_valid, x_ref, s0_ref, cbuf_ref, win_ref, wout_ref, p_ref, ln_ref, wr_ref,
                     rb_ref, *refs[n_alias:])

    @pl.when(pl.program_id(0) >= n_real)
    def _():
        s_ref[...] = jnp.zeros_like(s_ref)


def _sample_tile(n_valid, x_ref, s0_ref, cbuf_ref, win_ref, wout_ref, p_ref, ln_ref, wr_ref, rb_ref,
                 x1_ref, route_ref, s_ref, buf_ref,
                 win_scr, wout_scr, proj_scr, qs_scr, b_scr, k_scr, o_scr, y_scr):
    tile = SAMPLE_SEQS * SAMPLE_ROWS

    @pl.when(pl.program_id(0) == 0)
    def _():
        win_scr[...] = win_ref[0].astype(BF16)
        wout_scr[...] = wout_ref[0].astype(BF16)

    xt = x_ref[...]
    proj_scr[...] = jnp.dot(xt.astype(BF16), win_scr[...], preferred_element_type=F32)

    q = proj_scr[:, OFF_Q:OFF_Q + 512]
    qs_scr[...] = q * _sigmoid(q)
    logf, k = _gate_terms(proj_scr[:, OFF_F:OFF_F + 512], p_ref)
    valid = (lax.broadcasted_iota(I32, (tile, 512), 0) & (SAMPLE_ROWS - 1)) < n_valid
    k_scr[...] = jnp.where(valid, k, 0.0)
    b_scr[...] = _seg_cumsum(jnp.where(valid, logf, 0.0), SAMPLE_ROWS)

    urow = lax.broadcasted_iota(I32, (SAMPLE_ROWS, 512), 0)

    def seq_body(s, carry):
        r0 = pl.multiple_of(s * SAMPLE_ROWS, SAMPLE_ROWS)
        rows = pl.ds(r0, SAMPLE_ROWS)
        qs_c = qs_scr[rows, :]
        b_c = b_scr[rows, :]
        k_c = k_scr[rows, :]
        v_c = proj_scr[rows, OFF_I:OFF_I + 512]
        states = [jnp.transpose(s0_ref[0, s, h]) for h in range(HEADS)]
        outs, states = _exact_block(qs_c, b_c, k_c, v_c, states, n_valid)
        for h in range(HEADS):
            o_scr[rows, h * HEAD_DIM:(h + 1) * HEAD_DIM] = outs[h]
            s_ref[0, s, h] = jnp.transpose(states[h])
        u = proj_scr[rows, OFF_GC:OFF_GC + 512] * proj_scr[rows, OFF_CX:OFF_CX + 512]
        prev2 = cbuf_ref[0, s, 0:1, :]
        prev1 = cbuf_ref[0, s, 1:2, :]
        u1 = jnp.where(urow == 0, prev1, pltpu.roll(u, 1, axis=0))
        u2 = jnp.where(urow == 0, prev2, jnp.where(urow == 1, prev1, pltpu.roll(u, 2, axis=0)))
        y_scr[rows, :] = (p_ref[0, P_CW0:P_CW0 + 1, :] * u2 + p_ref[0, P_CW1:P_CW1 + 1, :] * u1
                          + p_ref[0, P_CW2:P_CW2 + 1, :] * u)
        buf_ref[s] = u[n_valid - (CONV_K - 1):n_valid, :]
        return carry

    lax.fori_loop(0, SAMPLE_SEQS, seq_body, 0)

    yc_in = proj_scr[:, OFF_GB:OFF_GB + 512] * y_scr[...]
    _post_mix(xt, o_scr[...], proj_scr[:, OFF_G:OFF_G + 512], yc_in, p_ref, wout_scr, ln_ref,
              wr_ref, rb_ref, x1_ref, route_ref)


def _plan_kernel(cls_ref, pos_ref, stats_ref):
    n_rows = cls_ref.shape[0]
    cls = cls_ref[...]
    upper = (lax.broadcasted_iota(I32, (LANES, LANES), 0)
             < lax.broadcasted_iota(I32, (LANES, LANES), 1)).astype(BF16)
    lower = (lax.broadcasted_iota(I32, (n_rows, n_rows), 1)
             < lax.broadcasted_iota(I32, (n_rows, n_rows), 0)).astype(BF16)
    lane = lax.broadcasted_iota(I32, (n_rows, LANES), 1)
    row_tot = jnp.zeros((n_rows, LANES), F32)
    for c in range(N_CLASSES):
        oh = jnp.where(cls == c, 1.0, 0.0)
        row_tot = jnp.where(lane == c, jnp.sum(oh, axis=1, keepdims=True), row_tot)
    before = jnp.dot(lower, row_tot.astype(BF16), preferred_element_type=F32)
    cnt = jnp.sum(row_tot, axis=0, keepdims=True)
    tiles = jnp.floor((cnt + (MOE_TILE - 1)) * (1.0 / MOE_TILE))
    first_tile = jnp.dot(jnp.broadcast_to(tiles, (SUBLANES, LANES)).astype(BF16), upper,
                         preferred_element_type=F32)[0:1]
    base = before + first_tile * MOE_TILE
    pos = jnp.zeros((n_rows, LANES), F32)
    for c in range(N_CLASSES):
        oh = jnp.where(cls == c, 1.0, 0.0)
        local = jnp.dot(oh.astype(BF16), upper, preferred_element_type=F32)
        pos = pos + oh * (base[:, c:c + 1] + local)
    pos_ref[...] = pos.astype(I32)
    sub = lax.broadcasted_iota(I32, (SUBLANES, LANES), 0)
    stats_ref[...] = jnp.where(sub == 0, cnt, jnp.where(sub == 1, tiles,
                                                        jnp.where(sub == 2, first_tile, 0.0)))


def _row_copy(src_ref, src_row, dst_ref, dst_row, sem):
    return pltpu.make_async_copy(src_ref.at[pl.ds(src_row, 1), :],
                                 dst_ref.at[pl.ds(dst_row, 1), :], sem)


def _perm_kernel(n, pos_ref, cstart_ref, ccnt_ref, cpad_ref, nused_ref, perm_ref):
    def stand_in_body(r, carry):
        perm_ref[r] = n + PAD_ROWS + r
        return carry

    lax.fori_loop(0, MOE_TILE, stand_in_body, 0, unroll=8)

    def unused_body(p, carry):
        perm_ref[p] = n
        return carry

    lax.fori_loop((nused_ref[0] + 1) * MOE_TILE, perm_ref.shape[0], unused_body, 0)

    def cls_body(c, k):
        base = MOE_TILE + cstart_ref[c] + ccnt_ref[c]

        def pad_body(q, carry):
            perm_ref[base + q] = n + k + q
            return carry

        lax.fori_loop(0, cpad_ref[c], pad_body, 0)
        return k + cpad_ref[c]

    lax.fori_loop(0, N_CLASSES, cls_body, 0)

    def tok_body(t, carry):
        perm_ref[MOE_TILE + pos_ref[t]] = t
        return carry

    lax.fori_loop(0, n, tok_body, 0, unroll=8)


def _expert_kernel(n, perm_ref, elo_ref, ehi_ref, flag_ref,
                   x_ref, wg_lo, wu_lo, wd_lo, wg_hi, wu_hi, wd_hi, ln_ref, out_ref,
                   xbuf, ybuf, w1_scr, wd_scr, gsem, ssem, zsem):
    i = pl.program_id(0)
    flag = flag_ref[i]
    slot = lax.rem(i, PIPE)
    nxt = lax.rem(i + 1, PIPE)
    other = lax.rem(i + 2, PIPE)

    def gather_wait(s):
        pltpu.make_async_copy(x_ref.at[pl.ds(0, MOE_TILE), :], xbuf.at[s], gsem.at[s]).wait()

    def scatter_wait(s):
        pltpu.make_async_copy(ybuf.at[s], out_ref.at[pl.ds(0, MOE_TILE), :], ssem.at[s]).wait()

    def gather_start(tile, s):
        for r in range(MOE_TILE):
            _row_copy(x_ref, perm_ref[(tile + 1) * MOE_TILE + r], xbuf.at[s], r,
                      gsem.at[s]).start(priority=r % 2)

    def scatter_start(tile, s):
        for r in range(MOE_TILE):
            _row_copy(ybuf.at[s], r, out_ref, perm_ref[(tile + 1) * MOE_TILE + r],
                      ssem.at[s]).start(priority=r % 2)

    @pl.when(i == 0)
    def _():
        ybuf[...] = jnp.zeros_like(ybuf)
        for c in range(PAD_ROWS // MOE_TILE):
            pltpu.make_async_copy(ybuf.at[0], out_ref.at[pl.ds(n + c * MOE_TILE, MOE_TILE), :],
                                  zsem).start()
        for c in range(PAD_ROWS // MOE_TILE):
            pltpu.make_async_copy(ybuf.at[0], out_ref.at[pl.ds(n + c * MOE_TILE, MOE_TILE), :],
                                  zsem).wait()
        for s in range(PIPE - 1):
            pltpu.make_async_copy(
                ybuf.at[s], out_ref.at[pl.ds(n + PAD_ROWS + (s + 1) * MOE_TILE, MOE_TILE), :],
                ssem.at[s]).start()
        gather_start(0, 0)
        gather_start(1, 1)

    @pl.when((flag & 2) != 0)
    def _():
        w1_scr[:, 0 * EXPERT_FF:1 * EXPERT_FF] = wg_lo[0, 0].astype(BF16)
        w1_scr[:, 1 * EXPERT_FF:2 * EXPERT_FF] = wu_lo[0, 0].astype(BF16)
        w1_scr[:, 2 * EXPERT_FF:3 * EXPERT_FF] = wg_hi[0, 0].astype(BF16)
        w1_scr[:, 3 * EXPERT_FF:4 * EXPERT_FF] = wu_hi[0, 0].astype(BF16)
        wd_scr[0:EXPERT_FF, :] = wd_lo[0, 0].astype(BF16)
        wd_scr[EXPERT_FF:, :] = wd_hi[0, 0].astype(BF16)

    def tile_body(slot, other):
        gather_wait(slot)
        scatter_wait(slot)
        gather_start(i + 2, other)
        scatter_start(i - 1, other)
        xs = xbuf[slot]
        x = xs[:, 0:D_MODEL]
        g_lo = xs[:, D_MODEL:D_MODEL + 1]
        g_hi = xs[:, D_MODEL + 1:D_MODEL + 2]
        hc = jnp.dot(x.astype(BF16), w1_scr[...], preferred_element_type=F32)
        hg_lo, hu_lo = hc[:, 0:EXPERT_FF], hc[:, EXPERT_FF:2 * EXPERT_FF]
        hg_hi, hu_hi = hc[:, 2 * EXPERT_FF:3 * EXPERT_FF], hc[:, 3 * EXPERT_FF:]
        h_lo = hg_lo * _sigmoid(hg_lo) * hu_lo * g_lo
        h_hi = hg_hi * _sigmoid(hg_hi) * hu_hi * g_hi
        h = jnp.concatenate([h_lo, h_hi], axis=-1).astype(BF16)
        y = jnp.dot(h, wd_scr[...], preferred_element_type=F32)
        ybuf[slot] = _layer_norm(ALPHA * x + y, ln_ref[0, 0:1, :], ln_ref[0, 1:2, :])

    for k in range(PIPE):
        @pl.when(((flag & 1) != 0) & (slot == k))
        def _():
            tile_body(k, (k + 2) % PIPE)

    @pl.when((flag & 4) != 0)
    def _():
        scatter_start(i - 1, other)
        for s in (slot, nxt, other):
            scatter_wait(s)
        for s in (slot, nxt):
            gather_wait(s)


def _layer_spec(shape, layer, n_grid, single_buffer=False):
    idx = (layer,) + (0,) * len(shape)
    if n_grid == 1:
        index_map = lambda i: idx
    else:
        index_map = lambda i, j: idx
    if single_buffer:
        return pl.BlockSpec((1,) + shape, index_map, pipeline_mode=pl.Buffered(1))
    return pl.BlockSpec((1,) + shape, index_map)


def _prompt_mixer(x, w_in, w_out, params, ln, wr, rb, layer, batch, seq, n_rows_out):
    n_t = seq // PROMPT_TILE
    n_real = batch * n_t
    last = n_real - 1
    row_in = pl.BlockSpec((PROMPT_TILE, D_MODEL), lambda i: (jnp.minimum(i, last), 0))
    row_next = pl.BlockSpec((PROMPT_TILE, D_MODEL), lambda i: (jnp.minimum(i + 1, last), 0))
    row_out = pl.BlockSpec((PROMPT_TILE, ROW_W), lambda i: (i, 0))
    route_out = pl.BlockSpec((SUBLANES, PROMPT_TILE), lambda i: (0, jnp.minimum(i, last)))
    full1 = lambda shape: pl.BlockSpec(shape, lambda i: (0, 0))
    seq_of = lambda i: jnp.minimum(i, last) // n_t
    return pl.pallas_call(
        functools.partial(_prompt_mixer_kernel, n_real, n_t),
        grid=(n_rows_out // PROMPT_TILE,),
        in_specs=[row_in, row_next,
                  _layer_spec((D_MODEL, N_IN), layer, 1, True),
                  _layer_spec((D_MODEL, D_MODEL), layer, 1, True),
                  _layer_spec((8, 512), layer, 1), _layer_spec((2, D_MODEL), layer, 1),
                  full1((2 * N_EXPERTS, D_MODEL)), full1((N_EXPERTS, 1))],
        out_specs=[row_out, route_out,
                   pl.BlockSpec((1, HEADS, HEAD_DIM, HEAD_DIM), lambda i: (seq_of(i), 0, 0, 0)),
                   pl.BlockSpec((1, CONV_K - 1, CONV_DIM), lambda i: (seq_of(i), 0, 0))],
        out_shape=[jax.ShapeDtypeStruct((n_rows_out, ROW_W), F32),
                   jax.ShapeDtypeStruct((SUBLANES, batch * seq), F32),
                   jax.ShapeDtypeStruct((batch, HEADS, HEAD_DIM, HEAD_DIM), F32),
                   jax.ShapeDtypeStruct((batch, CONV_K - 1, CONV_DIM), F32)],
        scratch_shapes=[pltpu.VMEM((D_MODEL, N_IN), BF16), pltpu.VMEM((D_MODEL, D_MODEL), BF16),
                        pltpu.VMEM((PROMPT_TILE, N_IN), F32), pltpu.VMEM((PROMPT_TILE, N_IN), F32)]
        + [pltpu.VMEM((PROMPT_TILE, 512), F32)] * 5
        + [pltpu.VMEM((HEADS, HEAD_DIM, HEAD_DIM), F32)] * 2
        + [pltpu.VMEM((SUBLANES, CONV_DIM), F32)],
        compiler_params=pltpu.CompilerParams(
            dimension_semantics=("arbitrary",), vmem_limit_bytes=VMEM_LIMIT),
        name="prompt_mixer",
    )(x, x, w_in, w_out, params, ln, wr, rb)


def _sample_mixer(x, x_row0, x1_all, s_all, s0, cbuf, w_in, w_out, params, ln, wr, rb, layer,
                  batch, n_valid):
    tile = SAMPLE_SEQS * SAMPLE_ROWS
    n = batch * SAMPLE_ROWS
    n_real = batch // SAMPLE_SEQS
    last = n_real - 1
    x1_row0 = x1_all.shape[0] - PAD_ROWS - n
    in_blk0, out_blk0 = x_row0 // tile, x1_row0 // tile
    real = lambda i: jnp.minimum(i, last)
    row_in = pl.BlockSpec((tile, D_MODEL), lambda i: (in_blk0 + real(i), 0))
    row_out = pl.BlockSpec((tile, ROW_W), lambda i: (out_blk0 + real(i), 0))
    route_out = pl.BlockSpec((SUBLANES, tile), lambda i: (0, real(i)))
    st_spec = pl.BlockSpec((1, SAMPLE_SEQS, HEADS, HEAD_DIM, HEAD_DIM),
                           lambda i: (layer, real(i), 0, 0, 0))
    cb_spec = pl.BlockSpec((1, SAMPLE_SEQS, CONV_K - 1, CONV_DIM),
                           lambda i: (layer, real(i), 0, 0))
    if s_all is None:
        n_steps = DEPTH * n_real
        st_out = pl.BlockSpec((1, SAMPLE_SEQS, HEADS, HEAD_DIM, HEAD_DIM),
                              lambda i: (i // n_real, lax.rem(i, n_real), 0, 0, 0))
        aliased = [x1_all]
    else:
        n_steps = n_real
        st_out = st_spec
        aliased = [x1_all, s_all]
    cb_out = pl.BlockSpec((SAMPLE_SEQS, CONV_K - 1, CONV_DIM), lambda i: (real(i), 0, 0))
    full1 = lambda shape: pl.BlockSpec(shape, lambda i: (0, 0))
    any_spec = pl.BlockSpec(memory_space=pl.ANY)
    return pl.pallas_call(
        functools.partial(_sample_mixer_kernel, n_valid, n_real, len(aliased)),
        grid=(n_steps,),
        in_specs=[row_in, st_spec, cb_spec,
                  _layer_spec((D_MODEL, N_IN), layer, 1, True),
                  _layer_spec((D_MODEL, D_MODEL), layer, 1, True),
                  _layer_spec((8, 512), layer, 1), _layer_spec((2, D_MODEL), layer, 1),
                  full1((2 * N_EXPERTS, D_MODEL)), full1((N_EXPERTS, 1))]
        + [any_spec] * len(aliased),
        out_specs=[row_out, route_out, st_out, cb_out],
        input_output_aliases={9: 0, 10: 2} if s_all is not None else {9: 0},
        out_shape=[jax.ShapeDtypeStruct(x1_all.shape, F32),
                   jax.ShapeDtypeStruct((SUBLANES, n), F32),
                   jax.ShapeDtypeStruct((DEPTH, batch, HEADS, HEAD_DIM, HEAD_DIM), F32),
                   jax.ShapeDtypeStruct((batch, CONV_K - 1, CONV_DIM), F32)],
        scratch_shapes=[pltpu.VMEM((D_MODEL, N_IN), BF16), pltpu.VMEM((D_MODEL, D_MODEL), BF16),
                        pltpu.VMEM((tile, N_IN), F32)] + [pltpu.VMEM((tile, 512), F32)] * 5,
        compiler_params=pltpu.CompilerParams(
            dimension_semantics=("arbitrary",), vmem_limit_bytes=VMEM_LIMIT),
        name="sample_mixer",
    )(x, s0, cbuf, w_in, w_out, params, ln, wr, rb, *aliased)


def _plan(cls2d):
    n_rows = cls2d.shape[0]
    return pl.pallas_call(
        _plan_kernel,
        out_shape=[jax.ShapeDtypeStruct((n_rows, LANES), I32),
                   jax.ShapeDtypeStruct((SUBLANES, LANES), F32)],
        name="moe_plan",
    )(cls2d)


def _perm(n, pos, cstart, ccnt, cpad, n_used, n_tiles):
    smem = pl.BlockSpec(memory_space=pltpu.SMEM)
    return pl.pallas_call(
        functools.partial(_perm_kernel, n),
        in_specs=[smem] * 5, out_specs=smem,
        out_shape=jax.ShapeDtypeStruct(((n_tiles + PIPE) * MOE_TILE,), I32),
        name="moe_perm",
    )(pos, cstart, ccnt, cpad, n_used.reshape(1))


def _experts(n, perm, elo, ehi, flag, x1_all, w_gate, w_up, w_down, ln, layer, n_steps):
    def w_spec(shape, table_idx):
        def index_map(i, perm_r, elo_r, ehi_r, flag_r):
            return (layer, (elo_r, ehi_r)[table_idx][i], 0, 0)
        return pl.BlockSpec((1, 1) + shape, index_map)

    up = (D_MODEL, EXPERT_FF)
    down = (EXPERT_FF, D_MODEL)
    any_spec = pl.BlockSpec(memory_space=pl.ANY)
    return pl.pallas_call(
        functools.partial(_expert_kernel, n),
        grid_spec=pltpu.PrefetchScalarGridSpec(
            num_scalar_prefetch=4, grid=(n_steps,),
            in_specs=[any_spec,
                      w_spec(up, 0), w_spec(up, 0), w_spec(down, 0),
                      w_spec(up, 1), w_spec(up, 1), w_spec(down, 1),
                      pl.BlockSpec((1, 2, D_MODEL), lambda i, p, e0, e1, f: (layer, 0, 0))],
            out_specs=any_spec,
            scratch_shapes=[pltpu.VMEM((PIPE, MOE_TILE, ROW_W), F32),
                            pltpu.VMEM((PIPE, MOE_TILE, D_MODEL), F32),
                            pltpu.VMEM((D_MODEL, 4 * EXPERT_FF), BF16),
                            pltpu.VMEM((2 * EXPERT_FF, D_MODEL), BF16),
                            pltpu.SemaphoreType.DMA((PIPE,)), pltpu.SemaphoreType.DMA((PIPE,)),
                            pltpu.SemaphoreType.DMA(())]),
        out_shape=jax.ShapeDtypeStruct((n + PAD_ROWS + PIPE * MOE_TILE, D_MODEL), F32),
        compiler_params=pltpu.CompilerParams(
            dimension_semantics=("arbitrary",), vmem_limit_bytes=VMEM_LIMIT),
        name="moe_experts",
    )(perm, elo, ehi, flag, x1_all, w_gate, w_up, w_down, w_gate, w_up, w_down, ln)


def _moe(x1_all, route_p, route_s, w_gate, w_up, w_down, ln2, layer):
    n = x1_all.shape[0] - PAD_ROWS
    n_tiles = n // MOE_TILE + N_CLASSES
    n_steps = n_tiles + 1
    cls2d = jnp.concatenate([route_p[0], route_s[0]]).reshape(n // LANES, LANES)
    pos2d, stats = _plan(cls2d)

    cnt = stats[0, :N_CLASSES].astype(I32)
    tiles = stats[1, :N_CLASSES].astype(I32)
    first_tile = stats[2, :N_CLASSES].astype(I32)
    end_tile = first_tile + tiles
    n_used = end_tile[N_CLASSES - 1]
    step = jnp.arange(n_steps, dtype=I32)
    tcls = jnp.sum((end_tile[None, :] <= jnp.minimum(step, n_used - 1)[:, None]).astype(I32), axis=1)
    valid = step < n_used
    changed = jnp.concatenate([jnp.ones((1,), jnp.bool_), tcls[1:] != tcls[:-1]]) & valid
    flag = valid.astype(I32) + 2 * changed.astype(I32) + 4 * (step == n_used).astype(I32)
    group = tcls // N_PAIRS
    pair = tcls % N_PAIRS
    elo = group * PER_GROUP + jnp.asarray(PAIR_LO, I32)[pair]
    ehi = group * PER_GROUP + jnp.asarray(PAIR_HI, I32)[pair]

    perm = _perm(n, pos2d.reshape(n), first_tile * MOE_TILE, cnt, tiles * MOE_TILE - cnt, n_used,
                 n_tiles)
    return _experts(n, perm, elo, ehi, flag, x1_all, w_gate, w_up, w_down, ln2, layer, n_steps)


def kernel(x_prompt, x_sample, state_hgrn, state_conv, w_in, w_out, lower_bounds, hgrn_norm_g,
           conv_w, conv_norm_g, ln1_g, ln1_b, ln2_g, ln2_b, w_router, router_bias,
           w_gate, w_up, w_down):
    batch, seq, _ = x_prompt.shape
    dec_batch, dec_seq, _ = x_sample.shape
    assert seq % PROMPT_TILE == 0 and dec_batch % SAMPLE_SEQS == 0
    assert CONV_K - 1 <= dec_seq <= SAMPLE_ROWS
    assert (batch * seq + dec_batch * SAMPLE_ROWS) % MOE_TILE == 0

    lb = jnp.cumsum(jax.nn.softmax(lower_bounds.astype(F32), axis=0), axis=0)
    lb = lb - lb[0:1]
    params = jnp.stack([jnp.log(lb), jnp.log1p(-lb), 1.0 - lb, hgrn_norm_g, conv_norm_g,
                        conv_w[:, 0], conv_w[:, 1], conv_w[:, 2]], axis=1)
    ln1 = jnp.stack([ln1_g, ln1_b], axis=1)
    ln2 = jnp.stack([ln2_g, ln2_b], axis=1)
    wr_hi = w_router.astype(BF16)
    wr_lo = (w_router - wr_hi.astype(F32)).astype(BF16)
    wr = jnp.concatenate([wr_hi.T, wr_lo.T], axis=0)
    rb = router_bias.astype(F32).reshape(N_EXPERTS, 1)

    n_p, n_s = batch * seq, dec_batch * SAMPLE_ROWS
    n = n_p + n_s
    xp = x_prompt.reshape(n_p, D_MODEL)
    xs = jnp.pad(x_sample, ((0, 0), (0, SAMPLE_ROWS - dec_seq), (0, 0))).reshape(n_s, D_MODEL)
    xs_row0 = 0

    s_p, b_p, b_s = [], [], []
    s_s = None
    for l in range(DEPTH):
        x1_all, route_p, s_l, b_l = _prompt_mixer(xp, w_in, w_out, params, ln1, wr, rb, l, batch,
                                                  seq, n + PAD_ROWS)
        s_p.append(s_l)
        b_p.append(b_l)
        x1_all, route_s, s_s, b_l = _sample_mixer(xs, xs_row0, x1_all, s_s, state_hgrn, state_conv,
                                                  w_in, w_out, params, ln1, wr, rb, l, dec_batch,
                                                  dec_seq)
        b_s.append(b_l)
        xp = xs = _moe(x1_all, route_p, route_s, w_gate, w_up, w_down, ln2, l)
        xs_row0 = n_p

    y_prompt = xp[:n_p].reshape(batch, seq, D_MODEL)
    y_sample = xs[n_p:n].reshape(dec_batch, SAMPLE_ROWS, D_MODEL)[:, :dec_seq]
    return (y_prompt, y_sample, jnp.stack(s_p), jnp.stack(b_p), s_s, jnp.stack(b_s))
```

```python
import functools

import jax
import jax.numpy as jnp
from jax import lax
from jax.experimental import pallas as pl
from jax.experimental.pallas import tpu as pltpu

F32 = jnp.float32
BF16 = jnp.bfloat16
I32 = jnp.int32

D_MODEL = 1024
DEPTH = 2
HEADS = 4
HEAD_DIM = 128
HGRN_W = HEADS * HEAD_DIM
CONV_DIM = 512
CONV_GROUPS = 8
CONV_K = 3
N_IN = 7 * 512
N_EXPERTS = 16
N_GROUPS = 4
PER_GROUP = 4
N_PAIRS = 6
N_CLASSES = N_GROUPS * N_PAIRS
PAIR_LO = (0, 0, 0, 1, 1, 2)
PAIR_HI = (1, 2, 3, 2, 3, 3)
EXPERT_FF = 512
ALPHA = (2 * DEPTH) ** 0.25
LN_EPS = 1e-5
RMS_EPS = 1e-6
SAFE_EXPONENT = 80.0

LANES = 128
SUBLANES = 8
CHUNK = 64
PROMPT_TILE = 256
SAMPLE_ROWS = 8
SAMPLE_SEQS = 16
MOE_TILE = 256
ROW_W = D_MODEL + LANES
PAD_ROWS = N_CLASSES * MOE_TILE
PIPE = 3
VMEM_LIMIT = 56 * 1024 * 1024

OFF_Q, OFF_F, OFF_I, OFF_G, OFF_GB, OFF_GC, OFF_CX = (i * 512 for i in range(7))

P_LOGLB, P_LOG1MLB, P_OMLB, P_HNORM, P_CNORM, P_CW0, P_CW1, P_CW2 = range(8)

NT_DIMS = (((1,), (1,)), ((), ()))
TN_DIMS = (((0,), (0,)), ((), ()))


def _sigmoid(x):
    return 1.0 / (1.0 + jnp.exp(-x))


def _seg_cumsum(x, seg):
    row = lax.broadcasted_iota(I32, x.shape, 0)
    pos = row & (seg - 1)
    sh = 1
    while sh < seg:
        x = x + jnp.where(pos >= sh, pltpu.roll(x, sh, axis=0), 0.0)
        sh *= 2
    return x


def _gate_terms(z, p_ref):
    e = jnp.exp(-jnp.abs(z))
    inv = 1.0 / (1.0 + e)
    logsig = jnp.minimum(z, 0.0) - jnp.log(1.0 + e)
    a = p_ref[0, P_LOGLB:P_LOGLB + 1, :]
    b = p_ref[0, P_LOG1MLB:P_LOG1MLB + 1, :] + logsig
    logf = jnp.maximum(a, b) + jnp.log(1.0 + jnp.exp(-jnp.abs(a - b)))
    k = p_ref[0, P_OMLB:P_OMLB + 1, :] * (jnp.where(z >= 0.0, e, 1.0) * inv)
    return logf, k


def _exact_block(qs_c, b_c, k_c, v_c, states, n_valid):
    trow = lax.broadcasted_iota(I32, (SAMPLE_ROWS, 1), 0)
    b_last = b_c[n_valid - 1:n_valid, :]
    qb = (qs_c * jnp.exp(b_c)).astype(BF16)
    kend = (k_c * jnp.exp(b_last - b_c)).astype(BF16)
    dec = jnp.exp(b_last)
    outs, new_states = [], []
    for h in range(HEADS):
        sl = slice(h * HEAD_DIM, (h + 1) * HEAD_DIM)
        st = states[h]
        o_h = lax.dot_general(qb[:, sl], st.astype(BF16), NT_DIMS, preferred_element_type=F32)
        for t in range(n_valid):
            dlt = jnp.minimum(b_c[:, sl] - b_c[t:t + 1, sl], 0.0)
            a_col = jnp.sum(qs_c[:, sl] * k_c[t:t + 1, sl] * jnp.exp(dlt), axis=-1, keepdims=True)
            a_col = jnp.where(trow >= t, a_col, 0.0)
            o_h = o_h + a_col * v_c[t:t + 1, sl]
        d_st = lax.dot_general(v_c[:, sl].astype(BF16), kend[:, sl], TN_DIMS,
                               preferred_element_type=F32)
        outs.append(o_h)
        new_states.append(st * dec[:, sl] + d_st)
    return outs, new_states


def _group_rms(x, n_groups):
    width = x.shape[-1] // n_groups
    x2 = x * x
    outs = []
    for s in range(x.shape[-1] // LANES):
        xs = x[:, s * LANES:(s + 1) * LANES]
        x2s = x2[:, s * LANES:(s + 1) * LANES]
        if width == LANES:
            ms = jnp.sum(x2s, axis=-1, keepdims=True) * (1.0 / width)
            scale = lax.rsqrt(ms + RMS_EPS)
        else:
            lane = lax.broadcasted_iota(I32, xs.shape, 1)
            lo = lane < width
            ms_lo = jnp.sum(jnp.where(lo, x2s, 0.0), axis=-1, keepdims=True) * (1.0 / width)
            ms_hi = jnp.sum(jnp.where(lo, 0.0, x2s), axis=-1, keepdims=True) * (1.0 / width)
            scale = jnp.where(lo, lax.rsqrt(ms_lo + RMS_EPS), lax.rsqrt(ms_hi + RMS_EPS))
        outs.append(xs * scale)
    return jnp.concatenate(outs, axis=-1)


def _layer_norm(r, g, b):
    mu = jnp.mean(r, axis=-1, keepdims=True)
    rc = r - mu
    var = jnp.mean(rc * rc, axis=-1, keepdims=True)
    return rc * lax.rsqrt(var + LN_EPS) * g + b


def _route(x1, wr_ref, rb_ref):
    x_hi = x1.astype(BF16)
    x_lo = (x1 - x_hi.astype(F32)).astype(BF16)
    wr = wr_ref[...]
    r1 = lax.dot_general(wr, x_hi, NT_DIMS, preferred_element_type=F32)
    r2 = lax.dot_general(wr, x_lo, NT_DIMS, preferred_element_type=F32)
    lt = r1[0:N_EXPERTS] + r1[N_EXPERTS:] + r2[0:N_EXPERTS] + r2[N_EXPERTS:] + rb_ref[...]
    lg = [lt[e:e + 1, :] for e in range(N_EXPERTS)]
    mx = lg[0]
    for e in range(1, N_EXPERTS):
        mx = jnp.maximum(mx, lg[e])
    ex = [jnp.exp(l - mx) for l in lg]
    best = None
    gi = None
    for g in range(N_GROUPS):
        a, b, c, d = ex[PER_GROUP * g:PER_GROUP * (g + 1)]
        s = jnp.maximum(jnp.maximum(jnp.maximum(a + b, a + c), jnp.maximum(a + d, b + c)),
                        jnp.maximum(b + d, c + d))
        if g == 0:
            best, gi = s, jnp.zeros(s.shape, I32)
        else:
            upd = s > best
            best = jnp.where(upd, s, best)
            gi = jnp.where(upd, g, gi)
    v = []
    for i in range(PER_GROUP):
        vi = ex[3 * PER_GROUP + i]
        for g in (2, 1, 0):
            vi = jnp.where(gi == g, ex[PER_GROUP * g + i], vi)
        v.append(vi)
    w1, i1 = v[0], jnp.zeros(v[0].shape, I32)
    for i in range(1, PER_GROUP):
        upd = v[i] > w1
        w1 = jnp.where(upd, v[i], w1)
        i1 = jnp.where(upd, i, i1)
    w2, i2 = None, None
    for i in range(PER_GROUP):
        vi = jnp.where(i1 == i, -1.0, v[i])
        if i == 0:
            w2, i2 = vi, jnp.zeros(vi.shape, I32)
        else:
            upd = vi > w2
            w2 = jnp.where(upd, vi, w2)
            i2 = jnp.where(upd, i, i2)
    inv = 1.0 / (w1 + w2)
    first_lo = i1 < i2
    lo = jnp.where(first_lo, i1, i2)
    hi = jnp.where(first_lo, i2, i1)
    g_lo = jnp.where(first_lo, w1, w2) * inv
    g_hi = jnp.where(first_lo, w2, w1) * inv
    pair = jnp.where(lo == 0, 0, jnp.where(lo == 1, 3, 5)) + hi - lo - 1
    return gi * N_PAIRS + pair, g_lo, g_hi


def _post_mix(xt, o, g, yc_in, p_ref, wout_scr, ln_ref, wr_ref, rb_ref, x1_ref, route_ref,
              fill=None):
    rows = xt.shape[0]
    o = _group_rms(o, HEADS) * p_ref[0, P_HNORM:P_HNORM + 1, :]
    o = o * (g * _sigmoid(g))
    yc = _group_rms(yc_in, CONV_GROUPS) * p_ref[0, P_CNORM:P_CNORM + 1, :]
    mix = jnp.concatenate([o, yc], axis=-1).astype(BF16)
    h = jnp.dot(mix, wout_scr[...], preferred_element_type=F32)
    x1 = _layer_norm(ALPHA * xt + h, ln_ref[0, 0:1, :], ln_ref[0, 1:2, :])
    if fill is not None:
        fill()
    cls, g_lo, g_hi = _route(x1, wr_ref, rb_ref)
    x1_ref[:, 0:D_MODEL] = x1
    sub = lax.broadcasted_iota(I32, (LANES, rows), 0)
    gates_t = jnp.where(sub == 0, g_lo, jnp.where(sub == 1, g_hi, 0.0))
    x1_ref[:, D_MODEL:ROW_W] = jnp.transpose(gates_t)
    sub8 = lax.broadcasted_iota(I32, (SUBLANES, rows), 0)
    route_ref[...] = jnp.where(sub8 == 0, cls.astype(F32), 0.0)


def _prompt_mixer_kernel(n_real, n_t, x_ref, xn_ref, *refs):
    i = pl.program_id(0)
    x1_ref = refs[6]

    for parity in range(2):
        @pl.when((i < n_real) & ((i & 1) == parity))
        def _():
            _prompt_tile(i, lax.rem(i, n_t), n_t, parity, x_ref, xn_ref, *refs)

    @pl.when(i >= n_real)
    def _():
        x1_ref[...] = jnp.zeros_like(x1_ref)


def _prompt_tile(i, j, n_t, parity, x_ref, xn_ref, win_ref, wout_ref, p_ref, ln_ref, wr_ref, rb_ref,
                 x1_ref, route_ref, s_ref, buf_ref,
                 win_scr, wout_scr, proj_a_scr, proj_b_scr, qs_scr, b_scr, k_scr, o_scr, yc_scr, st_scr,
                 st0_scr, ubuf_scr):
    proj_scr, proj_next_scr = (proj_a_scr, proj_b_scr) if parity == 0 else (proj_b_scr, proj_a_scr)

    if parity == 0:
        @pl.when(i == 0)
        def _():
            win_scr[...] = win_ref[0].astype(BF16)
            wout_scr[...] = wout_ref[0].astype(BF16)
            proj_scr[...] = jnp.dot(x_ref[...].astype(BF16), win_scr[...],
                                    preferred_element_type=F32)

    @pl.when(j == 0)
    def _():
        st_scr[...] = jnp.zeros_like(st_scr)
        ubuf_scr[...] = jnp.zeros_like(ubuf_scr)

    xn_bf = xn_ref[...].astype(BF16)

    def proj_piece(g):
        cols = slice(g * 512, (g + 1) * 512)
        proj_next_scr[:, cols] = jnp.dot(xn_bf, win_scr[:, cols], preferred_element_type=F32)

    xt = x_ref[...]

    q = proj_scr[:, OFF_Q:OFF_Q + 512]
    qs_scr[...] = q * _sigmoid(q)
    logf, k = _gate_terms(proj_scr[:, OFF_F:OFF_F + 512], p_ref)
    k_scr[...] = k
    b_scr[...] = _seg_cumsum(logf, CHUNK)
    proj_piece(0)

    tri = (lax.broadcasted_iota(I32, (CHUNK, CHUNK), 0)
           >= lax.broadcasted_iota(I32, (CHUNK, CHUNK), 1))

    states = [st_scr[h] for h in range(HEADS)]
    for h in range(HEADS):
        st0_scr[h] = states[h]
    worst = jnp.zeros((1, 512), F32)
    for c in range(PROMPT_TILE // CHUNK):
        rows = slice(c * CHUNK, (c + 1) * CHUNK)
        qs_c = qs_scr[rows, :]
        b_c = b_scr[rows, :]
        k_c = k_scr[rows, :]
        v_c = proj_scr[rows, OFF_I:OFF_I + 512].astype(BF16)
        b_mid = b_c[CHUNK // 2 - 1:CHUNK // 2, :]
        b_last = b_c[CHUNK - 1:CHUNK, :]
        worst = jnp.maximum(worst, jnp.maximum(-b_mid, b_mid - b_last))
        qd = (qs_c * jnp.exp(b_c - b_mid)).astype(BF16)
        kd = (k_c * jnp.exp(b_mid - b_c)).astype(BF16)
        qb = (qs_c * jnp.exp(b_c)).astype(BF16)
        kend = (k_c * jnp.exp(b_last - b_c)).astype(BF16)
        dec = jnp.exp(b_last)
        for h in range(HEADS):
            sl = slice(h * HEAD_DIM, (h + 1) * HEAD_DIM)
            att = lax.dot_general(qd[:, sl], kd[:, sl], NT_DIMS, preferred_element_type=F32)
            att = jnp.where(tri, att, 0.0).astype(BF16)
            st = states[h]
            o_h = jnp.dot(att, v_c[:, sl], preferred_element_type=F32)
            o_h = o_h + lax.dot_general(qb[:, sl], st.astype(BF16), NT_DIMS,
                                        preferred_element_type=F32)
            o_scr[rows, sl] = o_h
            d_st = lax.dot_general(v_c[:, sl], kend[:, sl], TN_DIMS, preferred_element_type=F32)
            states[h] = st * dec[:, sl] + d_st
        proj_piece(c + 1)
    for h in range(HEADS):
        st_scr[h] = states[h]

    u = proj_scr[:, OFF_GC:OFF_GC + 512] * proj_scr[:, OFF_CX:OFF_CX + 512]
    row = lax.broadcasted_iota(I32, u.shape, 0)
    prev2 = ubuf_scr[SUBLANES - 2:SUBLANES - 1, :]
    prev1 = ubuf_scr[SUBLANES - 1:SUBLANES, :]
    u1 = jnp.where(row == 0, prev1, pltpu.roll(u, 1, axis=0))
    u2 = jnp.where(row == 0, prev2, jnp.where(row == 1, prev1, pltpu.roll(u, 2, axis=0)))
    y = (p_ref[0, P_CW0:P_CW0 + 1, :] * u2 + p_ref[0, P_CW1:P_CW1 + 1, :] * u1
         + p_ref[0, P_CW2:P_CW2 + 1, :] * u)
    ubuf_scr[...] = u[PROMPT_TILE - SUBLANES:, :]
    yc_scr[...] = proj_scr[:, OFF_GB:OFF_GB + 512] * y

    def finish(x_tile, fill=None):
        _post_mix(x_tile, o_scr[...], proj_scr[:, OFF_G:OFF_G + 512], yc_scr[...], p_ref, wout_scr,
                  ln_ref, wr_ref, rb_ref, x1_ref, route_ref, fill)

    finish(xt, lambda: (proj_piece(5), proj_piece(6)))

    @pl.when(jnp.logical_not(jnp.max(worst) <= SAFE_EXPONENT))
    def _():
        for h in range(HEADS):
            st_scr[h] = st0_scr[h]

        def block_body(blk, carry):
            r0 = pl.multiple_of(blk * SAMPLE_ROWS, SAMPLE_ROWS)
            rows = pl.ds(r0, SAMPLE_ROWS)
            b_blk = b_scr[rows, :]
            b_prev = b_scr[pl.ds(jnp.maximum(r0 - 1, 0), 1), :]
            b_prev = jnp.where((r0 & (CHUNK - 1)) == 0, 0.0, b_prev)
            outs, new_states = _exact_block(
                qs_scr[rows, :], b_blk - b_prev, k_scr[rows, :], proj_scr[rows, OFF_I:OFF_I + 512],
                [st_scr[h] for h in range(HEADS)], SAMPLE_ROWS)
            for h in range(HEADS):
                o_scr[rows, h * HEAD_DIM:(h + 1) * HEAD_DIM] = outs[h]
                st_scr[h] = new_states[h]
            return carry

        lax.fori_loop(0, PROMPT_TILE // SAMPLE_ROWS, block_body, 0)
        finish(x_ref[...])

    @pl.when(j == n_t - 1)
    def _():
        for h in range(HEADS):
            s_ref[0, h] = jnp.transpose(st_scr[h])
        buf_ref[0] = ubuf_scr[SUBLANES - (CONV_K - 1):, :]


def _sample_mixer_kernel(n_valid, n_real, n_alias, x_ref, s0_ref, cbuf_ref, win_ref, wout_ref,
                         p_ref, ln_ref, wr_ref, rb_ref, *refs):
    x1_ref, route_ref, s_ref = refs[n_alias:n_alias + 3]

    @pl.when(pl.program_id(0) < n_real)
    def _():
        _sample_tile(n_valid, x_ref, s0_ref, cbuf_ref, win_ref, wout_ref, p_ref, ln_ref, wr_ref,
                     rb_ref, *refs[n_alias:])

    @pl.when(pl.program_id(0) >= n_real)
    def _():
        s_ref[...] = jnp.zeros_like(s_ref)


def _sample_tile(n_valid, x_ref, s0_ref, cbuf_ref, win_ref, wout_ref, p_ref, ln_ref, wr_ref, rb_ref,
                 x1_ref, route_ref, s_ref, buf_ref,
                 win_scr, wout_scr, proj_scr, qs_scr, b_scr, k_scr, o_scr, y_scr):
    tile = SAMPLE_SEQS * SAMPLE_ROWS

    @pl.when(pl.program_id(0) == 0)
    def _():
        win_scr[...] = win_ref[0].astype(BF16)
        wout_scr[...] = wout_ref[0].astype(BF16)

    xt = x_ref[...]
    proj_scr[...] = jnp.dot(xt.astype(BF16), win_scr[...], preferred_element_type=F32)

    q = proj_scr[:, OFF_Q:OFF_Q + 512]
    qs_scr[...] = q * _sigmoid(q)
    logf, k = _gate_terms(proj_scr[:, OFF_F:OFF_F + 512], p_ref)
    valid = (lax.broadcasted_iota(I32, (tile, 512), 0) & (SAMPLE_ROWS - 1)) < n_valid
    k_scr[...] = jnp.where(valid, k, 0.0)
    b_scr[...] = _seg_cumsum(jnp.where(valid, logf, 0.0), SAMPLE_ROWS)

    urow = lax.broadcasted_iota(I32, (SAMPLE_ROWS, 512), 0)

    def seq_body(s, carry):
        r0 = pl.multiple_of(s * SAMPLE_ROWS, SAMPLE_ROWS)
        rows = pl.ds(r0, SAMPLE_ROWS)
        qs_c = qs_scr[rows, :]
        b_c = b_scr[rows, :]
        k_c = k_scr[rows, :]
        v_c = proj_scr[rows, OFF_I:OFF_I + 512]
        states = [jnp.transpose(s0_ref[0, s, h]) for h in range(HEADS)]
        outs, states = _exact_block(qs_c, b_c, k_c, v_c, states, n_valid)
        for h in range(HEADS):
            o_scr[rows, h * HEAD_DIM:(h + 1) * HEAD_DIM] = outs[h]
            s_ref[0, s, h] = jnp.transpose(states[h])
        u = proj_scr[rows, OFF_GC:OFF_GC + 512] * proj_scr[rows, OFF_CX:OFF_CX + 512]
        prev2 = cbuf_ref[0, s, 0:1, :]
        prev1 = cbuf_ref[0, s, 1:2, :]
        u1 = jnp.where(urow == 0, prev1, pltpu.roll(u, 1, axis=0))
        u2 = jnp.where(urow == 0, prev2, jnp.where(urow == 1, prev1, pltpu.roll(u, 2, axis=0)))
        y_scr[rows, :] = (p_ref[0, P_CW0:P_CW0 + 1, :] * u2 + p_ref[0, P_CW1:P_CW1 + 1, :] * u1
                          + p_ref[0, P_CW2:P_CW2 + 1, :] * u)
        buf_ref[s] = u[n_valid - (CONV_K - 1):n_valid, :]
        return carry

    lax.fori_loop(0, SAMPLE_SEQS, seq_body, 0)

    yc_in = proj_scr[:, OFF_GB:OFF_GB + 512] * y_scr[...]
    _post_mix(xt, o_scr[...], proj_scr[:, OFF_G:OFF_G + 512], yc_in, p_ref, wout_scr, ln_ref,
              wr_ref, rb_ref, x1_ref, route_ref)


def _plan_kernel(cls_ref, pos_ref, stats_ref):
    n_rows = cls_ref.shape[0]
    cls = cls_ref[...]
    upper = (lax.broadcasted_iota(I32, (LANES, LANES), 0)
             < lax.broadcasted_iota(I32, (LANES, LANES), 1)).astype(BF16)
    lower = (lax.broadcasted_iota(I32, (n_rows, n_rows), 1)
             < lax.broadcasted_iota(I32, (n_rows, n_rows), 0)).astype(BF16)
    lane = lax.broadcasted_iota(I32, (n_rows, LANES), 1)
    row_tot = jnp.zeros((n_rows, LANES), F32)
    for c in range(N_CLASSES):
        oh = jnp.where(cls == c, 1.0, 0.0)
        row_tot = jnp.where(lane == c, jnp.sum(oh, axis=1, keepdims=True), row_tot)
    before = jnp.dot(lower, row_tot.astype(BF16), preferred_element_type=F32)
    cnt = jnp.sum(row_tot, axis=0, keepdims=True)
    tiles = jnp.floor((cnt + (MOE_TILE - 1)) * (1.0 / MOE_TILE))
    first_tile = jnp.dot(jnp.broadcast_to(tiles, (SUBLANES, LANES)).astype(BF16), upper,
                         preferred_element_type=F32)[0:1]
    base = before + first_tile * MOE_TILE
    pos = jnp.zeros((n_rows, LANES), F32)
    for c in range(N_CLASSES):
        oh = jnp.where(cls == c, 1.0, 0.0)
        local = jnp.dot(oh.astype(BF16), upper, preferred_element_type=F32)
        pos = pos + oh * (base[:, c:c + 1] + local)
    pos_ref[...] = pos.astype(I32)
    sub = lax.broadcasted_iota(I32, (SUBLANES, LANES), 0)
    stats_ref[...] = jnp.where(sub == 0, cnt, jnp.where(sub == 1, tiles,
                                                        jnp.where(sub == 2, first_tile, 0.0)))


def _row_copy(src_ref, src_row, dst_ref, dst_row, sem):
    return pltpu.make_async_copy(src_ref.at[pl.ds(src_row, 1), :],
                                 dst_ref.at[pl.ds(dst_row, 1), :], sem)


def _perm_kernel(n, pos_ref, cstart_ref, ccnt_ref, cpad_ref, nused_ref, perm_ref):
    def stand_in_body(r, carry):
        perm_ref[r] = n + PAD_ROWS + r
        return carry

    lax.fori_loop(0, MOE_TILE, stand_in_body, 0, unroll=8)

    def unused_body(p, carry):
        perm_ref[p] = n
        return carry

    lax.fori_loop((nused_ref[0] + 1) * MOE_TILE, perm_ref.shape[0], unused_body, 0)

    def cls_body(c, k):
        base = MOE_TILE + cstart_ref[c] + ccnt_ref[c]

        def pad_body(q, carry):
            perm_ref[base + q] = n + k + q
            return carry

        lax.fori_loop(0, cpad_ref[c], pad_body, 0)
        return k + cpad_ref[c]

    lax.fori_loop(0, N_CLASSES, cls_body, 0)

    def tok_body(t, carry):
        perm_ref[MOE_TILE + pos_ref[t]] = t
        return carry

    lax.fori_loop(0, n, tok_body, 0, unroll=8)


def _expert_kernel(n, perm_ref, elo_ref, ehi_ref, flag_ref,
                   x_ref, wg_lo, wu_lo, wd_lo, wg_hi, wu_hi, wd_hi, ln_ref, out_ref,
                   xbuf, ybuf, w1_scr, wd_scr, gsem, ssem, zsem):
    i = pl.program_id(0)
    flag = flag_ref[i]
    slot = lax.rem(i, PIPE)
    nxt = lax.rem(i + 1, PIPE)
    other = lax.rem(i + 2, PIPE)

    def gather_wait(s):
        pltpu.make_async_copy(x_ref.at[pl.ds(0, MOE_TILE), :], xbuf.at[s], gsem.at[s]).wait()

    def scatter_wait(s):
        pltpu.make_async_copy(ybuf.at[s], out_ref.at[pl.ds(0, MOE_TILE), :], ssem.at[s]).wait()

    def gather_start(tile, s):
        for r in range(MOE_TILE):
            _row_copy(x_ref, perm_ref[(tile + 1) * MOE_TILE + r], xbuf.at[s], r,
                      gsem.at[s]).start(priority=1)

    def scatter_start(tile, s):
        for r in range(MOE_TILE):
            _row_copy(ybuf.at[s], r, out_ref, perm_ref[(tile + 1) * MOE_TILE + r],
                      ssem.at[s]).start(priority=0)

    @pl.when(i == 0)
    def _():
        ybuf[...] = jnp.zeros_like(ybuf)
        for c in range(PAD_ROWS // MOE_TILE):
            pltpu.make_async_copy(ybuf.at[0], out_ref.at[pl.ds(n + c * MOE_TILE, MOE_TILE), :],
                                  zsem).start()
        for c in range(PAD_ROWS // MOE_TILE):
            pltpu.make_async_copy(ybuf.at[0], out_ref.at[pl.ds(n + c * MOE_TILE, MOE_TILE), :],
                                  zsem).wait()
        for s in range(PIPE - 1):
            pltpu.make_async_copy(
                ybuf.at[s], out_ref.at[pl.ds(n + PAD_ROWS + (s + 1) * MOE_TILE, MOE_TILE), :],
                ssem.at[s]).start()
        gather_start(0, 0)
        gather_start(1, 1)

    @pl.when((flag & 2) != 0)
    def _():
        w1_scr[:, 0 * EXPERT_FF:1 * EXPERT_FF] = wg_lo[0, 0].astype(BF16)
        w1_scr[:, 1 * EXPERT_FF:2 * EXPERT_FF] = wu_lo[0, 0].astype(BF16)
        w1_scr[:, 2 * EXPERT_FF:3 * EXPERT_FF] = wg_hi[0, 0].astype(BF16)
        w1_scr[:, 3 * EXPERT_FF:4 * EXPERT_FF] = wu_hi[0, 0].astype(BF16)
        wd_scr[0:EXPERT_FF, :] = wd_lo[0, 0].astype(BF16)
        wd_scr[EXPERT_FF:, :] = wd_hi[0, 0].astype(BF16)

    def tile_body(slot, other):
        gather_wait(slot)
        scatter_wait(slot)
        gather_start(i + 2, other)
        scatter_start(i - 1, other)
        xs = xbuf[slot]
        x = xs[:, 0:D_MODEL]
        g_lo = xs[:, D_MODEL:D_MODEL + 1]
        g_hi = xs[:, D_MODEL + 1:D_MODEL + 2]
        hc = jnp.dot(x.astype(BF16), w1_scr[...], preferred_element_type=F32)
        hg_lo, hu_lo = hc[:, 0:EXPERT_FF], hc[:, EXPERT_FF:2 * EXPERT_FF]
        hg_hi, hu_hi = hc[:, 2 * EXPERT_FF:3 * EXPERT_FF], hc[:, 3 * EXPERT_FF:]
        h_lo = hg_lo * _sigmoid(hg_lo) * hu_lo * g_lo
        h_hi = hg_hi * _sigmoid(hg_hi) * hu_hi * g_hi
        h = jnp.concatenate([h_lo, h_hi], axis=-1).astype(BF16)
        y = jnp.dot(h, wd_scr[...], preferred_element_type=F32)
        ybuf[slot] = _layer_norm(ALPHA * x + y, ln_ref[0, 0:1, :], ln_ref[0, 1:2, :])

    for k in range(PIPE):
        @pl.when(((flag & 1) != 0) & (slot == k))
        def _():
            tile_body(k, (k + 2) % PIPE)

    @pl.when((flag & 4) != 0)
    def _():
        scatter_start(i - 1, other)
        for s in (slot, nxt, other):
            scatter_wait(s)
        for s in (slot, nxt):
            gather_wait(s)


def _layer_spec(shape, layer, n_grid, single_buffer=False):
    idx = (layer,) + (0,) * len(shape)
    if n_grid == 1:
        index_map = lambda i: idx
    else:
        index_map = lambda i, j: idx
    if single_buffer:
        return pl.BlockSpec((1,) + shape, index_map, pipeline_mode=pl.Buffered(1))
    return pl.BlockSpec((1,) + shape, index_map)


def _prompt_mixer(x, w_in, w_out, params, ln, wr, rb, layer, batch, seq, n_rows_out):
    n_t = seq // PROMPT_TILE
    n_real = batch * n_t
    last = n_real - 1
    row_in = pl.BlockSpec((PROMPT_TILE, D_MODEL), lambda i: (jnp.minimum(i, last), 0))
    row_next = pl.BlockSpec((PROMPT_TILE, D_MODEL), lambda i: (jnp.minimum(i + 1, last), 0))
    row_out = pl.BlockSpec((PROMPT_TILE, ROW_W), lambda i: (i, 0))
    route_out = pl.BlockSpec((SUBLANES, PROMPT_TILE), lambda i: (0, jnp.minimum(i, last)))
    full1 = lambda shape: pl.BlockSpec(shape, lambda i: (0, 0))
    seq_of = lambda i: jnp.minimum(i, last) // n_t
    return pl.pallas_call(
        functools.partial(_prompt_mixer_kernel, n_real, n_t),
        grid=(n_rows_out // PROMPT_TILE,),
        in_specs=[row_in, row_next,
                  _layer_spec((D_MODEL, N_IN), layer, 1, True),
                  _layer_spec((D_MODEL, D_MODEL), layer, 1, True),
                  _layer_spec((8, 512), layer, 1), _layer_spec((2, D_MODEL), layer, 1),
                  full1((2 * N_EXPERTS, D_MODEL)), full1((N_EXPERTS, 1))],
        out_specs=[row_out, route_out,
                   pl.BlockSpec((1, HEADS, HEAD_DIM, HEAD_DIM), lambda i: (seq_of(i), 0, 0, 0)),
                   pl.BlockSpec((1, CONV_K - 1, CONV_DIM), lambda i: (seq_of(i), 0, 0))],
        out_shape=[jax.ShapeDtypeStruct((n_rows_out, ROW_W), F32),
                   jax.ShapeDtypeStruct((SUBLANES, batch * seq), F32),
                   jax.ShapeDtypeStruct((batch, HEADS, HEAD_DIM, HEAD_DIM), F32),
                   jax.ShapeDtypeStruct((batch, CONV_K - 1, CONV_DIM), F32)],
        scratch_shapes=[pltpu.VMEM((D_MODEL, N_IN), BF16), pltpu.VMEM((D_MODEL, D_MODEL), BF16),
                        pltpu.VMEM((PROMPT_TILE, N_IN), F32), pltpu.VMEM((PROMPT_TILE, N_IN), F32)]
        + [pltpu.VMEM((PROMPT_TILE, 512), F32)] * 5
        + [pltpu.VMEM((HEADS, HEAD_DIM, HEAD_DIM), F32)] * 2
        + [pltpu.VMEM((SUBLANES, CONV_DIM), F32)],
        compiler_params=pltpu.CompilerParams(
            dimension_semantics=("arbitrary",), vmem_limit_bytes=VMEM_LIMIT),
        name="prompt_mixer",
    )(x, x, w_in, w_out, params, ln, wr, rb)


def _sample_mixer(x, x_row0, x1_all, s_all, s0, cbuf, w_in, w_out, params, ln, wr, rb, layer,
                  batch, n_valid):
    tile = SAMPLE_SEQS * SAMPLE_ROWS
    n = batch * SAMPLE_ROWS
    n_real = batch // SAMPLE_SEQS
    last = n_real - 1
    x1_row0 = x1_all.shape[0] - PAD_ROWS - n
    in_blk0, out_blk0 = x_row0 // tile, x1_row0 // tile
    real = lambda i: jnp.minimum(i, last)
    row_in = pl.BlockSpec((tile, D_MODEL), lambda i: (in_blk0 + real(i), 0))
    row_out = pl.BlockSpec((tile, ROW_W), lambda i: (out_blk0 + real(i), 0))
    route_out = pl.BlockSpec((SUBLANES, tile), lambda i: (0, real(i)))
    st_spec = pl.BlockSpec((1, SAMPLE_SEQS, HEADS, HEAD_DIM, HEAD_DIM),
                           lambda i: (layer, real(i), 0, 0, 0))
    cb_spec = pl.BlockSpec((1, SAMPLE_SEQS, CONV_K - 1, CONV_DIM),
                           lambda i: (layer, real(i), 0, 0))
    if s_all is None:
        n_steps = DEPTH * n_real
        st_out = pl.BlockSpec((1, SAMPLE_SEQS, HEADS, HEAD_DIM, HEAD_DIM),
                              lambda i: (i // n_real, lax.rem(i, n_real), 0, 0, 0))
        aliased = [x1_all]
    else:
        n_steps = n_real
        st_out = st_spec
        aliased = [x1_all, s_all]
    cb_out = pl.BlockSpec((SAMPLE_SEQS, CONV_K - 1, CONV_DIM), lambda i: (real(i), 0, 0))
    full1 = lambda shape: pl.BlockSpec(shape, lambda i: (0, 0))
    any_spec = pl.BlockSpec(memory_space=pl.ANY)
    return pl.pallas_call(
        functools.partial(_sample_mixer_kernel, n_valid, n_real, len(aliased)),
        grid=(n_steps,),
        in_specs=[row_in, st_spec, cb_spec,
                  _layer_spec((D_MODEL, N_IN), layer, 1, True),
                  _layer_spec((D_MODEL, D_MODEL), layer, 1, True),
                  _layer_spec((8, 512), layer, 1), _layer_spec((2, D_MODEL), layer, 1),
                  full1((2 * N_EXPERTS, D_MODEL)), full1((N_EXPERTS, 1))]
        + [any_spec] * len(aliased),
        out_specs=[row_out, route_out, st_out, cb_out],
        input_output_aliases={9: 0, 10: 2} if s_all is not None else {9: 0},
        out_shape=[jax.ShapeDtypeStruct(x1_all.shape, F32),
                   jax.ShapeDtypeStruct((SUBLANES, n), F32),
                   jax.ShapeDtypeStruct((DEPTH, batch, HEADS, HEAD_DIM, HEAD_DIM), F32),
                   jax.ShapeDtypeStruct((batch, CONV_K - 1, CONV_DIM), F32)],
        scratch_shapes=[pltpu.VMEM((D_MODEL, N_IN), BF16), pltpu.VMEM((D_MODEL, D_MODEL), BF16),
                        pltpu.VMEM((tile, N_IN), F32)] + [pltpu.VMEM((tile, 512), F32)] * 5,
        compiler_params=pltpu.CompilerParams(
            dimension_semantics=("arbitrary",), vmem_limit_bytes=VMEM_LIMIT),
        name="sample_mixer",
    )(x, s0, cbuf, w_in, w_out, params, ln, wr, rb, *aliased)


def _plan(cls2d):
    n_rows = cls2d.shape[0]
    return pl.pallas_call(
        _plan_kernel,
        out_shape=[jax.ShapeDtypeStruct((n_rows, LANES), I32),
                   jax.ShapeDtypeStruct((SUBLANES, LANES), F32)],
        name="moe_plan",
    )(cls2d)


def _perm(n, pos, cstart, ccnt, cpad, n_used, n_tiles):
    smem = pl.BlockSpec(memory_space=pltpu.SMEM)
    return pl.pallas_call(
        functools.partial(_perm_kernel, n),
        in_specs=[smem] * 5, out_specs=smem,
        out_shape=jax.ShapeDtypeStruct(((n_tiles + PIPE) * MOE_TILE,), I32),
        name="moe_perm",
    )(pos, cstart, ccnt, cpad, n_used.reshape(1))


def _experts(n, perm, elo, ehi, flag, x1_all, w_gate, w_up, w_down, ln, layer, n_steps):
    def w_spec(shape, table_idx):
        def index_map(i, perm_r, elo_r, ehi_r, flag_r):
            return (layer, (elo_r, ehi_r)[table_idx][i], 0, 0)
        return pl.BlockSpec((1, 1) + shape, index_map)

    up = (D_MODEL, EXPERT_FF)
    down = (EXPERT_FF, D_MODEL)
    any_spec = pl.BlockSpec(memory_space=pl.ANY)
    return pl.pallas_call(
        functools.partial(_expert_kernel, n),
        grid_spec=pltpu.PrefetchScalarGridSpec(
            num_scalar_prefetch=4, grid=(n_steps,),
            in_specs=[any_spec,
                      w_spec(up, 0), w_spec(up, 0), w_spec(down, 0),
                      w_spec(up, 1), w_spec(up, 1), w_spec(down, 1),
                      pl.BlockSpec((1, 2, D_MODEL), lambda i, p, e0, e1, f: (layer, 0, 0))],
            out_specs=any_spec,
            scratch_shapes=[pltpu.VMEM((PIPE, MOE_TILE, ROW_W), F32),
                            pltpu.VMEM((PIPE, MOE_TILE, D_MODEL), F32),
                            pltpu.VMEM((D_MODEL, 4 * EXPERT_FF), BF16),
                            pltpu.VMEM((2 * EXPERT_FF, D_MODEL), BF16),
                            pltpu.SemaphoreType.DMA((PIPE,)), pltpu.SemaphoreType.DMA((PIPE,)),
                            pltpu.SemaphoreType.DMA(())]),
        out_shape=jax.ShapeDtypeStruct((n + PAD_ROWS + PIPE * MOE_TILE, D_MODEL), F32),
        compiler_params=pltpu.CompilerParams(
            dimension_semantics=("arbitrary",), vmem_limit_bytes=VMEM_LIMIT),
        name="moe_experts",
    )(perm, elo, ehi, flag, x1_all, w_gate, w_up, w_down, w_gate, w_up, w_down, ln)


def _moe(x1_all, route_p, route_s, w_gate, w_up, w_down, ln2, layer):
    n = x1_all.shape[0] - PAD_ROWS
    n_tiles = n // MOE_TILE + N_CLASSES
    n_steps = n_tiles + 1
    cls2d = jnp.concatenate([route_p[0], route_s[0]]).reshape(n // LANES, LANES)
    pos2d, stats = _plan(cls2d)

    cnt = stats[0, :N_CLASSES].astype(I32)
    tiles = stats[1, :N_CLASSES].astype(I32)
    first_tile = stats[2, :N_CLASSES].astype(I32)
    end_tile = first_tile + tiles
    n_used = end_tile[N_CLASSES - 1]
    step = jnp.arange(n_steps, dtype=I32)
    tcls = jnp.sum((end_tile[None, :] <= jnp.minimum(step, n_used - 1)[:, None]).astype(I32), axis=1)
    valid = step < n_used
    changed = jnp.concatenate([jnp.ones((1,), jnp.bool_), tcls[1:] != tcls[:-1]]) & valid
    flag = valid.astype(I32) + 2 * changed.astype(I32) + 4 * (step == n_used).astype(I32)
    group = tcls // N_PAIRS
    pair = tcls % N_PAIRS
    elo = group * PER_GROUP + jnp.asarray(PAIR_LO, I32)[pair]
    ehi = group * PER_GROUP + jnp.asarray(PAIR_HI, I32)[pair]

    perm = _perm(n, pos2d.reshape(n), first_tile * MOE_TILE, cnt, tiles * MOE_TILE - cnt, n_used,
                 n_tiles)
    return _experts(n, perm, elo, ehi, flag, x1_all, w_gate, w_up, w_down, ln2, layer, n_steps)


def kernel(x_prompt, x_sample, state_hgrn, state_conv, w_in, w_out, lower_bounds, hgrn_norm_g,
           conv_w, conv_norm_g, ln1_g, ln1_b, ln2_g, ln2_b, w_router, router_bias,
           w_gate, w_up, w_down):
    batch, seq, _ = x_prompt.shape
    dec_batch, dec_seq, _ = x_sample.shape
    assert seq % PROMPT_TILE == 0 and dec_batch % SAMPLE_SEQS == 0
    assert CONV_K - 1 <= dec_seq <= SAMPLE_ROWS
    assert (batch * seq + dec_batch * SAMPLE_ROWS) % MOE_TILE == 0

    lb = jnp.cumsum(jax.nn.softmax(lower_bounds.astype(F32), axis=0), axis=0)
    lb = lb - lb[0:1]
    params = jnp.stack([jnp.log(lb), jnp.log1p(-lb), 1.0 - lb, hgrn_norm_g, conv_norm_g,
                        conv_w[:, 0], conv_w[:, 1], conv_w[:, 2]], axis=1)
    ln1 = jnp.stack([ln1_g, ln1_b], axis=1)
    ln2 = jnp.stack([ln2_g, ln2_b], axis=1)
    wr_hi = w_router.astype(BF16)
    wr_lo = (w_router - wr_hi.astype(F32)).astype(BF16)
    wr = jnp.concatenate([wr_hi.T, wr_lo.T], axis=0)
    rb = router_bias.astype(F32).reshape(N_EXPERTS, 1)

    n_p, n_s = batch * seq, dec_batch * SAMPLE_ROWS
    n = n_p + n_s
    xp = x_prompt.reshape(n_p, D_MODEL)
    xs = jnp.pad(x_sample, ((0, 0), (0, SAMPLE_ROWS - dec_seq), (0, 0))).reshape(n_s, D_MODEL)
    xs_row0 = 0

    s_p, b_p, b_s = [], [], []
    s_s = None
    for l in range(DEPTH):
        x1_all, route_p, s_l, b_l = _prompt_mixer(xp, w_in, w_out, params, ln1, wr, rb, l, batch,
                                                  seq, n + PAD_ROWS)
        s_p.append(s_l)
        b_p.append(b_l)
        x1_all, route_s, s_s, b_l = _sample_mixer(xs, xs_row0, x1_all, s_s, state_hgrn, state_conv,
                                                  w_in, w_out, params, ln1, wr, rb, l, dec_batch,
                                                  dec_seq)
        b_s.append(b_l)
        xp = xs = _moe(x1_all, route_p, route_s, w_gate, w_up, w_down, ln2, l)
        xs_row0 = n_p

    y_prompt = xp[:n_p].reshape(batch, seq, D_MODEL)
    y_sample = xs[n_p:n].reshape(dec_batch, SAMPLE_ROWS, D_MODEL)[:, :dec_seq]
    return (y_prompt, y_sample, jnp.stack(s_p), jnp.stack(b_p), s_s, jnp.stack(b_s))
```
